```python
import jax
import jax.numpy as jnp
from jax import lax
import numpy as np

D_MODEL = 1024
BATCH = 2
SEQ = 8192
DEPTH = 2

F32 = jnp.float32
GRID_W = 64
CTX_LEN = 256
D_FF = 4 * D_MODEL
MIX_WIDTH = D_MODEL
NORM_EPS = 1e-6
GN_EPS = 64e-5
ROPE_THETA = 10000.0

ML_HEAD_DIM = 128
ML_HEADS = (MIX_WIDTH // 2) // ML_HEAD_DIM
ML_WIDTH = ML_HEADS * ML_HEAD_DIM
ML_CHUNK = 128

RW_HEAD_DIM = 64
RW_HEADS = (MIX_WIDTH - ML_WIDTH) // RW_HEAD_DIM
RW_WIDTH = RW_HEADS * RW_HEAD_DIM
RW_DECAY_RANK = 64
RW_ICLR_RANK = 64
RW_GATE_RANK = 128
RW_SHIFT_WIDTH = 3

REC_SPLITS = (ML_WIDTH,) * 4 + (ML_HEADS,) * 4 + (RW_WIDTH,) * 3 + (RW_DECAY_RANK,) * 2 + (RW_ICLR_RANK,) * 2 + (RW_GATE_RANK,)
REC_IN = sum(REC_SPLITS)

ATT_HEAD_DIM = 64
WIN_Q_HEADS = (MIX_WIDTH // 2) // ATT_HEAD_DIM
WIN_KV_HEADS = WIN_Q_HEADS // 4
GLB_Q_HEADS = (MIX_WIDTH - WIN_Q_HEADS * ATT_HEAD_DIM) // ATT_HEAD_DIM
GLB_KV_HEADS = GLB_Q_HEADS // 4
WINDOW = 128
Q_BLOCK = 128
ATT_SPLITS = (WIN_Q_HEADS * ATT_HEAD_DIM, WIN_KV_HEADS * ATT_HEAD_DIM, WIN_KV_HEADS * ATT_HEAD_DIM,
              GLB_Q_HEADS * ATT_HEAD_DIM, GLB_KV_HEADS * ATT_HEAD_DIM, GLB_KV_HEADS * ATT_HEAD_DIM)
ATT_IN = sum(ATT_SPLITS)

N_REC_LAYERS = (DEPTH + 1) // 2
N_ATT_LAYERS = DEPTH // 2

kernel_name = 'hybrid_mlstm_rwkv7_swa_gqa_dit'


def split_cols(z, sizes):
    return jnp.split(z, [int(i) for i in np.cumsum(sizes)[:-1]], axis=-1)


def rms_norm(x, g):
    xf = x.astype(F32)
    y = xf * lax.rsqrt(jnp.mean(xf * xf, -1, keepdims=True) + NORM_EPS)
    return (y * g.astype(F32)).astype(x.dtype)


def head_layernorm(y, g):
    yc = y - jnp.mean(y, -1, keepdims=True)
    y = yc * lax.rsqrt(jnp.mean(yc * yc, -1, keepdims=True) + GN_EPS)
    return y.reshape(y.shape[:2] + (-1,)) * g


def adaln(cvec, w_mod, b_mod):
    m = jax.nn.silu(cvec) @ w_mod + b_mod
    return jnp.split(m, 6, axis=-1)


def sq_relu_mlp(h, w1, w2):
    return jnp.square(jax.nn.relu(h @ w1)) @ w2


def centred_dwconv(z, w):
    pad = w.shape[0] // 2
    return lax.conv_general_dilated(z, w[:, None, :], window_strides=(1,), padding=[(pad, pad)],
                                    dimension_numbers=('NWC', 'WIO', 'NWC'), feature_group_count=z.shape[-1])


def mlstm_chunkwise(q, k, v, ig, fg, state):
    B, T, H, dk = q.shape
    dv = v.shape[-1]
    L = ML_CHUNK
    nc = T // L

    def chunks(a):
        a = a.reshape((B, nc, L, H) + a.shape[3:])
        return jnp.moveaxis(jnp.moveaxis(a, 1, 0), 2, 3)

    xs = (chunks(q), chunks(k * dk ** -0.5), chunks(v), chunks(ig), chunks(jax.nn.log_sigmoid(fg)))
    causal = jnp.tril(jnp.ones((L, L), dtype=bool))

    def step(carry, inp):
        C, n, m = carry
        qb, kb, vb, ib, lf = inp
        b = jnp.cumsum(lf, axis=-1)
        d_in = jnp.where(causal, b[..., :, None] - b[..., None, :] + ib[..., None, :], -jnp.inf)
        d_prev = b + m[..., None]
        m_t = jnp.maximum(d_prev, jnp.max(d_in, axis=-1))
        s = jnp.einsum('bhtd,bhsd->bhts', qb, kb) * jnp.exp(d_in - m_t[..., None])
        w_prev = jnp.exp(d_prev - m_t)
        num = jnp.einsum('bhts,bhsv->bhtv', s, vb) + w_prev[..., None] * jnp.einsum('bhtd,bhdv->bhtv', qb, C)
        den = jnp.sum(s, -1) + w_prev * jnp.einsum('bhtd,bhd->bht', qb, n)
        h = num / jnp.maximum(jnp.abs(den), jnp.exp(-m_t))[..., None]
        b_end = b[..., -1]
        d_end = b_end[..., None] - b + ib
        m_new = jnp.maximum(b_end + m, jnp.max(d_end, -1))
        w_end = jnp.exp(d_end - m_new[..., None])
        keep = jnp.exp(b_end + m - m_new)
        C = keep[..., None, None] * C + jnp.einsum('bhs,bhsd,bhsv->bhdv', w_end, kb, vb)
        n = keep[..., None] * n + jnp.einsum('bhs,bhsd->bhd', w_end, kb)
        return (C, n, m_new), h

    state, hs = lax.scan(step, state, xs)
    h = jnp.swapaxes(jnp.moveaxis(hs, 0, 1), 2, 3).reshape(B, T, H, dv)
    return h, state


def rwkv7_scan(r, decay, k, v, kk, a, state, reverse):
    def step(S, inp):
        r_t, w_t, k_t, v_t, kk_t, a_t = inp
        s_kk = jnp.einsum('bhij,bhj->bhi', S, kk_t)
        S = S * w_t[:, :, None, :] - s_kk[..., None] * (kk_t * a_t)[:, :, None, :] + v_t[..., None] * k_t[:, :, None, :]
        return S, jnp.einsum('bhij,bhj->bhi', S, r_t)

    xs = tuple(jnp.moveaxis(t, 1, 0) for t in (r, decay, k, v, kk, a))
    state, ys = lax.scan(step, state, xs, reverse=reverse)
    return jnp.moveaxis(ys, 0, 1), state


def recurrent_streams(h, w_in, gate_b, conv_w, w0, w2, a0, a2, k_k, k_a):
    B, T, _ = h.shape
    (q, k, v, o, i_f, f_f, i_b, f_b, r, kr, vr,
     wd_f, wd_b, ad_f, ad_b, gd) = split_cols((h @ w_in).astype(F32), REC_SPLITS)
    r, kr, vr = jnp.split(centred_dwconv(jnp.concatenate([r, kr, vr], -1), conv_w.astype(F32)), 3, axis=-1)
    ml = lambda z: z.reshape(B, T, ML_HEADS, ML_HEAD_DIM)
    rw = lambda z: z.reshape(B, T, RW_HEADS, RW_HEAD_DIM)
    kk = rw(kr * k_k)
    kk = kk * lax.rsqrt(jnp.sum(kk * kk, -1, keepdims=True) + NORM_EPS)
    decay, iclr, key = [], [], []
    for d, (wd, ad) in enumerate(((wd_f, ad_f), (wd_b, ad_b))):
        w_log = -jax.nn.softplus(-(w0[d] + jnp.tanh(wd) @ w2[d])) - 0.5
        a = jax.nn.sigmoid(a0[d] + ad @ a2[d])
        decay.append(rw(jnp.exp(-jnp.exp(w_log))))
        iclr.append(rw(a))
        key.append(rw(kr * (1.0 + (a - 1.0) * k_a)))
    return {'q': ml(q), 'k': ml(k), 'v': ml(v), 'o': o,
            'ig': (i_f + gate_b[0], i_b + gate_b[2]), 'fg': (f_f + gate_b[1], f_b + gate_b[3]),
            'r': rw(r), 'k_raw': rw(kr), 'v_rw': rw(vr), 'kk': kk,
            'decay': decay, 'iclr': iclr, 'key': key, 'gd': gd}


def recurrent_mix(h_lat, h_ctx, w_in, gate_b, ml_norm_g, conv_w, w0, w2, a0, a2, g2, k_k, k_a, r_k,
                  rw_norm_g, w_out, need_ctx):
    sc = recurrent_streams(h_ctx, w_in, gate_b, conv_w, w0, w2, a0, a2, k_k, k_a)
    sl = recurrent_streams(h_lat, w_in, gate_b, conv_w, w0, w2, a0, a2, k_k, k_a)
    B = h_lat.shape[0]
    ml_c, ml_l, rw_c, rw_l = [], [], [], []
    for d, reverse in enumerate((False, True)):
        flip = (lambda z: jnp.flip(z, 1)) if reverse else (lambda z: z)
        st = (jnp.zeros((B, ML_HEADS, ML_HEAD_DIM, ML_HEAD_DIM), F32),
              jnp.zeros((B, ML_HEADS, ML_HEAD_DIM), F32), jnp.zeros((B, ML_HEADS), F32))
        h_c, st = mlstm_chunkwise(flip(sc['q']), flip(sc['k']), flip(sc['v']), flip(sc['ig'][d]), flip(sc['fg'][d]), st)
        h_l, _ = mlstm_chunkwise(flip(sl['q']), flip(sl['k']), flip(sl['v']), flip(sl['ig'][d]), flip(sl['fg'][d]), st)
        ml_c.append(flip(h_c))
        ml_l.append(flip(h_l))
        S = jnp.zeros((B, RW_HEADS, RW_HEAD_DIM, RW_HEAD_DIM), F32)
        y_c, S = rwkv7_scan(sc['r'], sc['decay'][d], sc['key'][d], sc['v_rw'], sc['kk'], sc['iclr'][d], S, reverse)
        y_l, _ = rwkv7_scan(sl['r'], sl['decay'][d], sl['key'][d], sl['v_rw'], sl['kk'], sl['iclr'][d], S, reverse)
        rw_c.append(y_c)
        rw_l.append(y_l)

    def merge(s, ml_h, rw_y):
        ml_out = head_layernorm(ml_h[0] + ml_h[1], ml_norm_g) * jax.nn.sigmoid(s['o'])
        bonus = jnp.sum(s['r'] * s['k_raw'] * r_k.reshape(RW_HEADS, RW_HEAD_DIM), -1, keepdims=True) * s['v_rw']
        rw_out = (head_layernorm(rw_y[0] + rw_y[1], rw_norm_g) + bonus.reshape(bonus.shape[:2] + (RW_WIDTH,))) \
            * (jax.nn.sigmoid(s['gd']) @ g2)
        return jnp.concatenate([ml_out, rw_out], -1) @ w_out

    y_lat = merge(sl, ml_l, rw_l).astype(h_lat.dtype)
    y_ctx = merge(sc, ml_c, rw_c).astype(h_ctx.dtype) if need_ctx else None
    return y_lat, y_ctx


def axial_rope(n_tokens, head_dim):
    rows = n_tokens // GRID_W
    row = jnp.repeat(jnp.arange(rows), GRID_W).astype(F32)
    col = jnp.tile(jnp.arange(GRID_W), rows).astype(F32)
    n_freq = head_dim // 4
    inv_freq = ROPE_THETA ** (-jnp.arange(n_freq, dtype=F32) / n_freq)
    ang = jnp.concatenate([row[:, None] * inv_freq, col[:, None] * inv_freq], -1)
    return jnp.cos(ang), jnp.sin(ang)


def apply_rope(x, cos, sin):
    xp = x.reshape(x.shape[:-1] + (-1, 2))
    x0, x1 = xp[..., 0], xp[..., 1]
    c, s = cos[None, :, None, :], sin[None, :, None, :]
    return jnp.stack([x0 * c - x1 * s, x0 * s + x1 * c], -1).reshape(x.shape).astype(x.dtype)


def qk_rms(z, g):
    zf = z.astype(F32)
    return (zf * lax.rsqrt(jnp.mean(zf * zf, -1, keepdims=True) + NORM_EPS) * g.astype(F32)).astype(z.dtype)


def attention_streams(h, w_in, win_q_norm, win_k_norm, glb_q_norm, glb_k_norm, rope):
    B, T, _ = h.shape
    qw, kw, vw, qg, kg, vg = split_cols(h @ w_in, ATT_SPLITS)
    hd = lambda z: z.reshape(B, T, -1, ATT_HEAD_DIM)
    qw, kw = qk_rms(hd(qw), win_q_norm), qk_rms(hd(kw), win_k_norm)
    qg, kg = qk_rms(hd(qg), glb_q_norm), qk_rms(hd(kg), glb_k_norm)
    if rope is not None:
        cos, sin = rope
        qw, kw, qg, kg = (apply_rope(t, cos, sin) for t in (qw, kw, qg, kg))
    grp = lambda z, kvh: z.reshape(B, T, kvh, -1, ATT_HEAD_DIM)
    return grp(qw, WIN_KV_HEADS), kw, hd(vw), grp(qg, GLB_KV_HEADS), kg, hd(vg)


def window_attention(q, k, v, k_ctx, v_ctx, sink):
    B, T, KV, G, dh = q.shape
    L = Q_BLOCK
    nb = T // L
    Lc = k_ctx.shape[1]
    scale = dh ** -0.5
    qb = q.reshape(B, nb, L, KV, G, dh)

    def band(z):
        zp = jnp.pad(z, ((0, 0), (L, L), (0, 0), (0, 0))).reshape(B, nb + 2, L, KV, dh)
        return jnp.concatenate([zp[:, :-2], zp[:, 1:-1], zp[:, 2:]], axis=2)

    kb, vb = band(k), band(v)
    s_loc = jnp.einsum('bnqkgd,bnskd->bnkgqs', qb, kb).astype(F32) * scale
    q_pos = jnp.arange(nb)[:, None] * L + jnp.arange(L)[None, :]
    k_pos = jnp.arange(nb)[:, None] * L + jnp.arange(3 * L)[None, :] - L
    valid = ((jnp.abs(k_pos[:, None, :] - q_pos[:, :, None]) <= WINDOW)
             & (k_pos >= 0)[:, None, :] & (k_pos < T)[:, None, :])
    s_loc = jnp.where(valid[None, :, None, None], s_loc, -jnp.inf)
    s_ctx = jnp.einsum('bnqkgd,bckd->bnkgqc', qb, k_ctx).astype(F32) * scale
    s_sink = jnp.broadcast_to(sink.astype(F32).reshape(KV, G)[None, None, :, :, None, None], s_loc.shape[:-1] + (1,))
    p = jax.nn.softmax(jnp.concatenate([s_loc, s_ctx, s_sink], -1), axis=-1).astype(v.dtype)
    o = (jnp.einsum('bnkgqs,bnskd->bnqkgd', p[..., :3 * L], vb)
         + jnp.einsum('bnkgqc,bckd->bnqkgd', p[..., 3 * L:3 * L + Lc], v_ctx))
    return o.reshape(B, T, KV * G * dh)


def global_attention(q, k, v, k_ctx, v_ctx):
    B, T, KV, G, dh = q.shape
    L = Q_BLOCK
    nb = T // L
    scale = dh ** -0.5
    keys = jnp.concatenate([k, k_ctx], axis=1)
    vals = jnp.concatenate([v, v_ctx], axis=1)
    qb = jnp.moveaxis(q.reshape(B, nb, L, KV, G, dh), 1, 0)

    def block(q_blk):
        s = jnp.einsum('bqkgd,bskd->bkgqs', q_blk, keys).astype(F32) * scale
        p = jax.nn.softmax(s, axis=-1).astype(vals.dtype)
        return jnp.einsum('bkgqs,bskd->bqkgd', p, vals)

    o = lax.map(block, qb)
    return jnp.moveaxis(o, 0, 1).reshape(B, T, KV * G * dh)


def context_attention(q, k, v, sink):
    B, Lc, KV, G, dh = q.shape
    s = jnp.einsum('bqkgd,bskd->bkgqs', q, k).astype(F32) * dh ** -0.5
    if sink is not None:
        s = jnp.concatenate([s, jnp.broadcast_to(sink.astype(F32).reshape(KV, G)[None, :, :, None, None], s.shape[:-1] + (1,))], -1)
    p = jax.nn.softmax(s, axis=-1)[..., :Lc].astype(v.dtype)
    return jnp.einsum('bkgqs,bskd->bqkgd', p, v).reshape(B, Lc, KV * G * dh)


def attention_mix(h_lat, h_ctx, w_in, win_q_norm, win_k_norm, win_sink, glb_q_norm, glb_k_norm, w_out, need_ctx):
    rope = axial_rope(h_lat.shape[1], ATT_HEAD_DIM)
    qw, kw, vw, qg, kg, vg = attention_streams(h_lat, w_in, win_q_norm, win_k_norm, glb_q_norm, glb_k_norm, rope)
    cqw, ckw, cvw, cqg, ckg, cvg = attention_streams(h_ctx, w_in, win_q_norm, win_k_norm, glb_q_norm, glb_k_norm, None)
    y_lat = jnp.concatenate([window_attention(qw, kw, vw, ckw, cvw, win_sink),
                             global_attention(qg, kg, vg, ckg, cvg)], -1) @ w_out
    y_ctx = None
    if need_ctx:
        y_ctx = jnp.concatenate([context_attention(cqw, ckw, cvw, win_sink),
                                 context_attention(cqg, ckg, cvg, None)], -1) @ w_out
    return y_lat, y_ctx


def setup_inputs(seed: int = 0) -> dict:
    key = jax.random.key(seed)
    keys = iter(jax.random.split(key, 40))

    def nrm(shape, scale):
        return jax.random.normal(next(keys), shape, F32) * scale

    def gain(shape):
        return 1.0 + nrm(shape, 0.02)

    nr, na = N_REC_LAYERS, N_ATT_LAYERS
    f_bias = jnp.linspace(3.0, 6.0, ML_HEADS, dtype=F32)
    zero_b = jnp.zeros((ML_HEADS,), F32)
    gate_base = jnp.stack([zero_b, f_bias, zero_b, f_bias])
    w0_base = jnp.tile(jnp.linspace(-6.0, -1.0, RW_HEAD_DIM, dtype=F32), RW_HEADS)
    conv_base = jnp.where(jnp.arange(RW_SHIFT_WIDTH) == RW_SHIFT_WIDTH // 2, 0.5, 0.25).astype(F32)[:, None]
    return {
        'x': nrm((BATCH, SEQ, D_MODEL), 1.0),
        'c': nrm((BATCH, D_MODEL), 1.0),
        'ctx': nrm((BATCH, CTX_LEN, D_MODEL), 1.0),
        'c_ctx': nrm((D_MODEL,), 1.0),
        'norm_mix': gain((DEPTH, D_MODEL)),
        'norm_ffn': gain((DEPTH, D_MODEL)),
        'mod_w': nrm((DEPTH, D_MODEL, 6 * D_MODEL), 0.5 * D_MODEL ** -0.5),
        'mod_b': nrm((DEPTH, 6 * D_MODEL), 0.02),
        'out_w': nrm((DEPTH, MIX_WIDTH, D_MODEL), MIX_WIDTH ** -0.5),
        'ffn_w1': nrm((DEPTH, D_MODEL, D_FF), D_MODEL ** -0.5),
        'ffn_w2': nrm((DEPTH, D_FF, D_MODEL), D_FF ** -0.5),
        'rec_in_w': nrm((nr, D_MODEL, REC_IN), D_MODEL ** -0.5),
        'ml_gate_b': gate_base[None] + nrm((nr, 4, ML_HEADS), 0.1),
        'ml_norm_g': gain((nr, ML_WIDTH)),
        'rw_conv': conv_base[None] + nrm((nr, RW_SHIFT_WIDTH, 3 * RW_WIDTH), 0.1),
        'rw_w0': w0_base[None, None] + nrm((nr, 2, RW_WIDTH), 0.1),
        'rw_w2': nrm((nr, 2, RW_DECAY_RANK, RW_WIDTH), 0.5 * RW_DECAY_RANK ** -0.5),
        'rw_a0': nrm((nr, 2, RW_WIDTH), 0.1),
        'rw_a2': nrm((nr, 2, RW_ICLR_RANK, RW_WIDTH), 0.5 * RW_ICLR_RANK ** -0.5),
        'rw_g2': nrm((nr, RW_GATE_RANK, RW_WIDTH), RW_GATE_RANK ** -0.5),
        'rw_kk': 0.85 + nrm((nr, RW_WIDTH), 0.05),
        'rw_ka': 1.0 + nrm((nr, RW_WIDTH), 0.05),
        'rw_rk': nrm((nr, RW_WIDTH), 0.05),
        'rw_norm_g': gain((nr, RW_WIDTH)),
        'att_in_w': nrm((na, D_MODEL, ATT_IN), D_MODEL ** -0.5),
        'win_q_norm': gain((na, ATT_HEAD_DIM)),
        'win_k_norm': gain((na, ATT_HEAD_DIM)),
        'win_sink': nrm((na, WIN_Q_HEADS), 0.5),
        'glb_q_norm': gain((na, ATT_HEAD_DIM)),
        'glb_k_norm': gain((na, ATT_HEAD_DIM)),
    }


def reference(x, c, ctx, c_ctx, norm_mix, norm_ffn, mod_w, mod_b, out_w, ffn_w1, ffn_w2,
              rec_in_w, ml_gate_b, ml_norm_g, rw_conv, rw_w0, rw_w2, rw_a0, rw_a2, rw_g2, rw_kk, rw_ka, rw_rk,
              rw_norm_g, att_in_w, win_q_norm, win_k_norm, win_sink, glb_q_norm, glb_k_norm):
    x_ctx = ctx
    for layer in range(DEPTH):
        need_ctx = layer < DEPTH - 1
        j = layer // 2
        sh1, sc1, g1, sh2, sc2, g2 = adaln(c[:, None, :], mod_w[layer], mod_b[layer])
        csh1, csc1, cg1, csh2, csc2, cg2 = adaln(c_ctx[None, None, :], mod_w[layer], mod_b[layer])
        h_lat = rms_norm(x, norm_mix[layer]) * (1.0 + sc1) + sh1
        h_ctx = rms_norm(x_ctx, norm_mix[layer]) * (1.0 + csc1) + csh1
        if layer % 2 == 0:
            y_lat, y_ctx = recurrent_mix(h_lat, h_ctx, rec_in_w[j], ml_gate_b[j], ml_norm_g[j], rw_conv[j],
                                         rw_w0[j], rw_w2[j], rw_a0[j], rw_a2[j], rw_g2[j], rw_kk[j], rw_ka[j],
                                         rw_rk[j], rw_norm_g[j], out_w[layer], need_ctx)
        else:
            y_lat, y_ctx = attention_mix(h_lat, h_ctx, att_in_w[j], win_q_norm[j], win_k_norm[j], win_sink[j],
                                         glb_q_norm[j], glb_k_norm[j], out_w[layer], need_ctx)
        x = x + g1 * y_lat
        x = x + g2 * sq_relu_mlp(rms_norm(x, norm_ffn[layer]) * (1.0 + sc2) + sh2, ffn_w1[layer], ffn_w2[layer])
        if need_ctx:
            x_ctx = x_ctx + cg1 * y_ctx
            x_ctx = x_ctx + cg2 * sq_relu_mlp(rms_norm(x_ctx, norm_ffn[layer]) * (1.0 + csc2) + csh2,
                                              ffn_w1[layer], ffn_w2[layer])
    return x
```

```python
import functools

import numpy as np
import jax
import jax.numpy as jnp
from jax import lax
from jax.experimental import pallas as pl
from jax.experimental.pallas import tpu as pltpu

F32 = jnp.float32
BF16 = jnp.bfloat16

GRID_W = 64
NORM_EPS = 1e-6
GN_EPS = 64e-5
ROPE_THETA = 10000.0
ML_HEAD_DIM = 128
ML_HEADS = 4
ML_WIDTH = ML_HEADS * ML_HEAD_DIM
ML_CHUNK = 128
RW_HEAD_DIM = 64
RW_HEADS = 8
RW_WIDTH = RW_HEADS * RW_HEAD_DIM
RW_CHUNK = 64
RW_DECAY_RANK = 64
RW_ICLR_RANK = 64
RW_GATE_RANK = 128
ATT_HEAD_DIM = 64
Q_HEADS = 8
KV_HEADS = 2
KV_GROUP = Q_HEADS // KV_HEADS
KV_WIDTH = KV_HEADS * ATT_HEAD_DIM
WINDOW = 128
Q_BLOCK = 128

LANES = 128
SUBLANES = 8
VMEM_LIMIT_BYTES = 56 * 1024 * 1024
ROW_TILE = 256
MASK_VALUE = -1e30


def _params(*semantics):
    return pltpu.CompilerParams(dimension_semantics=semantics, vmem_limit_bytes=VMEM_LIMIT_BYTES)


def _resident(shape):
    zeros = (0,) * len(shape)
    return pl.BlockSpec(shape, lambda *_: zeros, pipeline_mode=pl.Buffered(1))


def _dot(a, b):
    return jnp.dot(a, b, preferred_element_type=F32)


def _dot_nt(a, b):
    return lax.dot_general(a, b, (((1,), (1,)), ((), ())), preferred_element_type=F32)


def _dot_tn(a, b):
    return lax.dot_general(a, b, (((0,), (0,)), ((), ())), preferred_element_type=F32)


def _split(x, n):
    parts, rest = [], x
    for _ in range(n):
        p = rest.astype(BF16)
        parts.append(p)
        rest = rest - p.astype(F32)
    return parts


def _dot_split_lhs(x, m, n):
    return sum(_dot(p, m) for p in _split(x, n))


def _dot_split_rhs(m, x, n):
    return sum(_dot(m, p) for p in _split(x, n))


def _dot_f32(a, b):
    ah, al = _split(a, 2)
    bh, bl = _split(b, 2)
    return _dot(ah, bh) + _dot(ah, bl) + _dot(al, bh)


def _log_sigmoid(x):
    return jnp.minimum(x, 0.0) - jnp.log1p(jnp.exp(-jnp.abs(x)))


def _norm_mod(x, gain, shift, scale):
    y = x * lax.rsqrt(jnp.mean(x * x, axis=-1, keepdims=True) + NORM_EPS)
    return (y * gain) * (1.0 + scale) + shift


def _group_layernorm(y, avg, gain):
    yc = y - _dot_split_lhs(y, avg, 2)
    var = _dot_split_lhs(yc * yc, avg, 2)
    return yc * lax.rsqrt(var + GN_EPS) * gain


def _mod_kernel(cv_ref, w_ref, b_ref, o_ref):
    cv = cv_ref[...]
    o_ref[0] = _dot_f32(cv * jax.nn.sigmoid(cv), w_ref[0]) + b_ref[0]


def _modulation(cvec, mod_w, mod_b):
    depth, d, n = mod_w.shape
    tn = n // 4
    return pl.pallas_call(
        _mod_kernel,
        grid=(depth, n // tn),
        in_specs=[pl.BlockSpec((SUBLANES, d), lambda l, j: (0, 0)),
                  pl.BlockSpec((1, d, tn), lambda l, j: (l, 0, j)),
                  pl.BlockSpec((1, 1, tn), lambda l, j: (l, 0, j))],
        out_specs=pl.BlockSpec((1, SUBLANES, tn), lambda l, j: (l, 0, j)),
        out_shape=jax.ShapeDtypeStruct((depth, SUBLANES, n), F32),
        compiler_params=_params("parallel", "parallel"),
        name="adaln_modulation",
    )(cvec, mod_w, mod_b.reshape(depth, 1, n))


REC_Q, REC_V, REC_O, REC_RKV, REC_LORA, REC_GATE, REC_END = 0, 512, 1024, 1536, 3072, 3456, 3584
N_GATES = 4 * ML_HEADS


def _rec_inproj_kernel(x_ref, mod_ref, g_ref, w_ref, wkt_ref, wgt_ref, gb_ref, gbt_ref,
                       q_ref, v_ref, o_ref, rkv_ref, lora_ref, gate_ref, kt_ref, gatet_ref):
    d = x_ref.shape[-1]
    mod = mod_ref[0]
    h = _norm_mod(x_ref[0], g_ref[...], mod[:, 0:d], mod[:, d:2 * d]).astype(BF16)
    q_ref[0] = _dot(h, w_ref[:, REC_Q:REC_V]).astype(BF16)
    v_ref[0] = _dot(h, w_ref[:, REC_V:REC_O]).astype(BF16)
    o_ref[0] = _dot(h, w_ref[:, REC_O:REC_RKV])
    rkv_ref[0] = _dot(h, w_ref[:, REC_RKV:REC_LORA])
    lora_ref[0] = _dot(h, w_ref[:, REC_LORA:REC_GATE])
    gate_ref[0] = _dot(h, w_ref[:, REC_GATE:REC_END]) + gb_ref[...]
    kt_ref[0] = _dot_nt(wkt_ref[...], h).astype(BF16)
    gatet_ref[0] = _dot_nt(wgt_ref[...], h) + gbt_ref[...]


def _rec_inproj(xs, mods, gain, w_main, wkt, wgt, gb_row, gb_col, n_lat_tiles):
    bsz, s, d = xs.shape
    tm = ROW_TILE
    tok = lambda c: pl.BlockSpec((1, tm, c), lambda b, i: (b, i, 0))
    mod_row = lambda b, i: (jnp.where(i >= n_lat_tiles, bsz, b), 0, 0)
    return pl.pallas_call(
        _rec_inproj_kernel,
        grid=(bsz, s // tm),
        in_specs=[tok(d),
                  pl.BlockSpec((1, 1, mods.shape[-1]), mod_row),
                  _resident((1, d)), _resident(w_main.shape), _resident(wkt.shape), _resident(wgt.shape),
                  _resident(gb_row.shape), _resident(gb_col.shape)],
        out_specs=[tok(ML_WIDTH), tok(ML_WIDTH), tok(ML_WIDTH), tok(3 * RW_WIDTH), tok(REC_GATE - REC_LORA),
                   tok(LANES),
                   pl.BlockSpec((1, ML_WIDTH, tm), lambda b, i: (b, 0, i)),
                   pl.BlockSpec((1, N_GATES, tm), lambda b, i: (b, 0, i))],
        out_shape=[jax.ShapeDtypeStruct((bsz, s, ML_WIDTH), BF16),
                   jax.ShapeDtypeStruct((bsz, s, ML_WIDTH), BF16),
                   jax.ShapeDtypeStruct((bsz, s, ML_WIDTH), F32),
                   jax.ShapeDtypeStruct((bsz, s, 3 * RW_WIDTH), F32),
                   jax.ShapeDtypeStruct((bsz, s, REC_GATE - REC_LORA), F32),
                   jax.ShapeDtypeStruct((bsz, s, LANES), F32),
                   jax.ShapeDtypeStruct((bsz, ML_WIDTH, s), BF16),
                   jax.ShapeDtypeStruct((bsz, N_GATES, s), F32)],
        compiler_params=_params("parallel", "parallel"),
        name="rec_inproj",
    )(xs, mods, gain, w_main, wkt, wgt, gb_row, gb_col)


def _mlstm_kernel(qf_ref, vf_ref, ktf_ref, gf_ref, gtf_ref, qb_ref, vb_ref, ktb_ref, gb_ref, gtb_ref,
                  hf_ref, hb_ref, c_ref, m_ref):
    @pl.when(pl.program_id(1) == 0)
    def _():
        c_ref[...] = jnp.zeros_like(c_ref)
        m_ref[...] = jnp.zeros_like(m_ref)

    L, dh = ML_CHUNK, ML_HEAD_DIM
    t_idx = lax.broadcasted_iota(jnp.int32, (L, L), 0)
    s_idx = lax.broadcasted_iota(jnp.int32, (L, L), 1)
    ones_col = (lax.broadcasted_iota(jnp.int32, (L, LANES), 1) == 0).astype(BF16)
    scale = dh ** -0.5
    streams = ((qf_ref, vf_ref, ktf_ref, gf_ref, gtf_ref, hf_ref),
               (qb_ref, vb_ref, ktb_ref, gb_ref, gtb_ref, hb_ref))
    for d, (q_ref, v_ref, kt_ref, g_ref, gt_ref, h_ref) in enumerate(streams):
        earlier = (s_idx <= t_idx) if d == 0 else (s_idx >= t_idx)
        tri = earlier.astype(BF16)
        tri_t = ((t_idx <= s_idx) if d == 0 else (t_idx >= s_idx)).astype(BF16)
        gt = gt_ref[0]
        lf_rows = _log_sigmoid(gt)
        b_cols = _dot_split_rhs(tri, _log_sigmoid(g_ref[0]), 3)
        b_rows = _dot_split_lhs(lf_rows, tri_t, 3)
        tot = jnp.sum(lf_rows, axis=1, keepdims=True)
        q, v, kt = q_ref[0], v_ref[0], kt_ref[0]
        for hh in range(ML_HEADS):
            j = d * ML_HEADS + hh
            ic, fc = 2 * ML_HEADS * d + hh, 2 * ML_HEADS * d + ML_HEADS + hh
            b_col, b_row, i_row = b_cols[:, fc:fc + 1], b_rows[fc:fc + 1, :], gt[ic:ic + 1, :]
            b_end = tot[fc:fc + 1, :]
            m_prev = m_ref[j:j + 1, 0:1]
            d_in = jnp.where(earlier, b_col - b_row + i_row, MASK_VALUE)
            d_prev = b_col + m_prev
            m_t = jnp.maximum(d_prev, jnp.max(d_in, axis=1, keepdims=True))
            qh = q[:, hh * dh:(hh + 1) * dh]
            kth = kt[hh * dh:(hh + 1) * dh, :]
            v_aug = jnp.concatenate([v[:, hh * dh:(hh + 1) * dh], ones_col], axis=1)
            cst = c_ref[j]
            s = _dot(qh, kth) * (scale * jnp.exp(d_in - m_t))
            acc = _dot(s.astype(BF16), v_aug) + jnp.exp(d_prev - m_t) * _dot(qh, cst.astype(BF16))
            den = jnp.maximum(jnp.abs(acc[:, dh:dh + 1]), jnp.exp(-m_t))
            h_ref[0, :, hh * dh:(hh + 1) * dh] = acc[:, :dh] / den
            d_end = b_end - b_row + i_row
            m_new = jnp.maximum(b_end + m_prev, jnp.max(d_end, axis=1, keepdims=True))
            w_end = jnp.exp(d_end - m_new) * scale
            c_ref[j] = jnp.exp(b_end + m_prev - m_new) * cst + _dot((kth.astype(F32) * w_end).astype(BF16), v_aug)
            m_ref[j:j + 1, :] = jnp.broadcast_to(m_new, (1, LANES))


def _chunk_order(n_lat, n_ctx):
    n = n_lat + n_ctx
    fwd = lambda c: (c + n_lat) % n
    bwd = lambda c: n - 1 - c
    return fwd, bwd


def _mlstm(q, v, kt, gates, gates_t, t_lat):
    bsz, s, _ = q.shape
    L = ML_CHUNK
    fwd, bwd = _chunk_order(t_lat // L, (s - t_lat) // L)
    specs = []
    for order in (fwd, bwd):
        tok = lambda c, order=order: pl.BlockSpec((1, L, c), lambda b, i: (b, order(i), 0))
        specs += [tok(ML_WIDTH), tok(ML_WIDTH),
                  pl.BlockSpec((1, ML_WIDTH, L), lambda b, i, order=order: (b, 0, order(i))),
                  tok(LANES),
                  pl.BlockSpec((1, N_GATES, L), lambda b, i, order=order: (b, 0, order(i)))]
    n_chain = 2 * ML_HEADS
    return pl.pallas_call(
        _mlstm_kernel,
        grid=(bsz, s // L),
        in_specs=specs,
        out_specs=[pl.BlockSpec((1, L, ML_WIDTH), lambda b, i: (b, fwd(i), 0)),
                   pl.BlockSpec((1, L, ML_WIDTH), lambda b, i: (b, bwd(i), 0))],
        out_shape=[jax.ShapeDtypeStruct((bsz, s, ML_WIDTH), F32)] * 2,
        scratch_shapes=[pltpu.VMEM((n_chain, ML_HEAD_DIM, ML_HEAD_DIM + LANES), F32),
                        pltpu.VMEM((n_chain, LANES), F32)],
        compiler_params=_params("parallel", "arbitrary"),
        name="mlstm_bidir",
    )(q, v, kt, gates, gates_t, q, v, kt, gates, gates_t)


def _rw_prep_kernel(z_ref, zp_ref, zn_ref, lora_ref, conv_ref, vec_ref, w2_ref, a2_ref, g2_ref, ones_ref,
                    r_ref, v_ref, kk_ref, bonus_ref, gate_ref, lw_ref, key_ref, bv_ref, *, t_lat, s_tot):
    tm, w = z_ref.shape[1], RW_WIDTH
    row0 = pl.program_id(1) * tm
    has_prev = jnp.where(jnp.logical_and(row0 != 0, row0 != t_lat), 1.0, 0.0)
    has_next = jnp.where(jnp.logical_and(row0 + tm != t_lat, row0 + tm != s_tot), 1.0, 0.0)
    z = z_ref[0]
    prev_row = zp_ref[0][SUBLANES - 1:SUBLANES, :] * has_prev
    next_row = zn_ref[0][0:1, :] * has_next
    ridx = lax.broadcasted_iota(jnp.int32, z.shape, 0)
    z_prev = jnp.where(ridx == 0, prev_row, pltpu.roll(z, 1, 0))
    z_next = jnp.where(ridx == tm - 1, next_row, pltpu.roll(z, tm - 1, 0))
    cw = conv_ref[...]
    zc = cw[0:1] * z_prev + cw[1:2] * z + cw[2:3] * z_next
    r, kr, vr = zc[:, :w], zc[:, w:2 * w], zc[:, 2 * w:]
    vec = vec_ref[...]
    k_k, k_a, r_k = vec[0:1], vec[1:2], vec[2:3]
    ones = ones_ref[...]
    kkr = kr * k_k
    kk = kkr * lax.rsqrt(_dot_split_lhs(kkr * kkr, ones, 2) + NORM_EPS)
    r_ref[0] = r
    v_ref[0] = vr
    kk_ref[0] = kk
    bonus_ref[0] = _dot_split_lhs(r * kr * r_k, ones, 2) * vr
    lora = lora_ref[0]
    dec = _dot(jnp.tanh(lora[:, 0:LANES]).astype(BF16), w2_ref[...])
    icl = _dot(lora[:, LANES:2 * LANES].astype(BF16), a2_ref[...])
    gate_ref[0] = _dot(jax.nn.sigmoid(lora[:, 2 * LANES:3 * LANES]).astype(BF16), g2_ref[...])
    for d in range(2):
        w_log = _log_sigmoid(vec[3 + d:4 + d] + dec[:, d * w:(d + 1) * w]) - 0.5
        a = jax.nn.sigmoid(vec[5 + d:6 + d] + icl[:, d * w:(d + 1) * w])
        lw_ref[d, 0] = -jnp.exp(w_log)
        key_ref[d, 0] = kr * (1.0 + (a - 1.0) * k_a)
        bv_ref[d, 0] = kk * a


def _rw_prep(rkv, lora, conv_w, vecs, w2cat, a2cat, g2, ones64, t_lat):
    bsz, s, _ = rkv.shape
    tm, w = ROW_TILE, RW_WIDTH
    nsub = tm // SUBLANES
    tok = lambda c: pl.BlockSpec((1, tm, c), lambda b, i: (b, i, 0))
    dirtok = pl.BlockSpec((2, 1, tm, w), lambda b, i: (0, b, i, 0))
    last = s // SUBLANES - 1
    return pl.pallas_call(
        functools.partial(_rw_prep_kernel, t_lat=t_lat, s_tot=s),
        grid=(bsz, s // tm),
        in_specs=[tok(3 * w),
                  pl.BlockSpec((1, SUBLANES, 3 * w), lambda b, i: (b, jnp.maximum(i * nsub - 1, 0), 0)),
                  pl.BlockSpec((1, SUBLANES, 3 * w), lambda b, i: (b, jnp.minimum((i + 1) * nsub, last), 0)),
                  tok(lora.shape[-1]),
                  _resident(conv_w.shape), _resident(vecs.shape), _resident(w2cat.shape), _resident(a2cat.shape),
                  _resident(g2.shape), _resident(ones64.shape)],
        out_specs=[tok(w)] * 5 + [dirtok] * 3,
        out_shape=[jax.ShapeDtypeStruct((bsz, s, w), F32)] * 5 + [jax.ShapeDtypeStruct((2, bsz, s, w), F32)] * 3,
        compiler_params=_params("parallel", "parallel"),
        name="rwkv_prep",
    )(rkv, rkv, rkv, lora, conv_w, vecs, w2cat, a2cat, g2, ones64)


def _rw_scan_kernel(r_ref, v_ref, kk_ref, lw_ref, key_ref, bv_ref, y_ref, s_ref):
    @pl.when(pl.program_id(2) == 0)
    def _():
        s_ref[...] = jnp.zeros_like(s_ref)

    C, n = RW_CHUNK, RW_HEAD_DIM
    sgn = 1 - 2 * pl.program_id(1)
    t2 = lax.broadcasted_iota(jnp.int32, (2 * C, 2 * C), 0) % C
    s2 = lax.broadcasted_iota(jnp.int32, (2 * C, 2 * C), 1) % C
    top = lax.broadcasted_iota(jnp.int32, (2 * C, 2 * C), 0) < C
    rel = (s2 - t2) * sgn
    keep = rel < jnp.where(top, 0, 1)
    tri = (rel[:C, :C] <= 0).astype(BF16)

    lw = lw_ref[0, 0]
    cum = _dot_split_rhs(tri, lw, 3)
    tot = jnp.sum(lw, axis=0, keepdims=True)
    r, v, kk, key, bv = r_ref[0], v_ref[0], kk_ref[0], key_ref[0, 0], bv_ref[0, 0]
    e_neg = jnp.exp(-cum)
    e_end = jnp.exp(tot - cum)
    alpha = (kk * jnp.exp(cum - lw)).astype(BF16)
    rho = (r * jnp.exp(cum)).astype(BF16)
    beta = (bv * e_neg).astype(BF16)
    kappa = (key * e_neg).astype(BF16)
    beta_end = (bv * e_end).astype(BF16)
    kappa_end = (key * e_end).astype(BF16)
    gamma = jnp.exp(tot)
    vb = v.astype(BF16)

    ys = []
    for h in range(RW_HEADS):
        sl = slice(h * n, (h + 1) * n)
        a_h, rho_h, v_h = alpha[:, sl], rho[:, sl], vb[:, sl]
        big = _dot_nt(jnp.concatenate([a_h, rho_h], axis=0), jnp.concatenate([beta[:, sl], kappa[:, sl]], axis=0))
        big = jnp.where(keep, big, 0.0)
        l_ab = big[:C, :C].astype(BF16)
        m_b = big[C:, :C].astype(BF16)
        lmv = _dot(big[:, C:].astype(BF16), v_h)
        x = -jnp.concatenate([a_h.astype(F32), lmv[:C]], axis=1)
        x = x - _dot(l_ab, x.astype(BF16))
        p = l_ab
        for _ in range(5):
            p = _dot(p, p).astype(BF16)
            x = x + _dot(p, x.astype(BF16))
        st = s_ref[h]
        ws = _dot_nt(jnp.concatenate([x[:, :n].astype(BF16), rho_h], axis=0), st.astype(BF16))
        u = ws[:C] + x[:, n:]
        ys.append(ws[C:] + _dot(m_b, u.astype(BF16)) + lmv[C:])
        upd = _dot_tn(jnp.concatenate([u.astype(BF16), v_h], axis=0),
                      jnp.concatenate([beta_end[:, sl], kappa_end[:, sl]], axis=0))
        s_ref[h] = st * gamma[:, sl] + upd
    y_ref[0, 0] = jnp.concatenate(ys, axis=1)


def _rw_scan(r, v, kk, lw, key, bv, t_lat):
    bsz, s, w = r.shape
    C = RW_CHUNK
    fwd, bwd = _chunk_order(t_lat // C, (s - t_lat) // C)
    order = lambda d, c: jnp.where(d == 0, fwd(c), bwd(c))
    tok = pl.BlockSpec((1, C, w), lambda b, d, c: (b, order(d, c), 0))
    dirtok = pl.BlockSpec((1, 1, C, w), lambda b, d, c: (d, b, order(d, c), 0))
    return pl.pallas_call(
        _rw_scan_kernel,
        grid=(bsz, 2, s // C),
        in_specs=[tok, tok, tok, dirtok, dirtok, dirtok],
        out_specs=dirtok,
        out_shape=jax.ShapeDtypeStruct((2, bsz, s, w), F32),
        scratch_shapes=[pltpu.VMEM((RW_HEADS, RW_HEAD_DIM, RW_HEAD_DIM), F32)],
        compiler_params=_params("parallel", "parallel", "arbitrary"),
        name="rwkv_scan",
    )(r, v, kk, lw, key, bv)


def _residual_mlp(mix, x, mod, g_ffn, wo_ref, w1_ref, w2_ref, out_ref):
    d = x.shape[-1]
    x1 = x + mod[:, 2 * d:3 * d] * _dot(mix, wo_ref[...])
    h2 = _norm_mod(x1, g_ffn, mod[:, 3 * d:4 * d], mod[:, 4 * d:5 * d]).astype(BF16)
    acc = jnp.zeros_like(x1)
    for k in range(w1_ref.shape[1] // d):
        hid = jnp.maximum(_dot(h2, w1_ref[:, k * d:(k + 1) * d]), 0.0)
        acc = acc + _dot((hid * hid).astype(BF16), w2_ref[k * d:(k + 1) * d, :])
    out_ref[0] = x1 + mod[:, 5 * d:6 * d] * acc


def _rec_post_kernel(hf_ref, hb_ref, o_ref, yf_ref, yb_ref, bonus_ref, gate_ref, x_ref, mod_ref, gf_ref,
                     mlg_ref, rwg_ref, avg_ml_ref, avg_rw_ref, wo_ref, w1_ref, w2_ref, out_ref):
    ml = _group_layernorm(hf_ref[0] + hb_ref[0], avg_ml_ref[...], mlg_ref[...]) * jax.nn.sigmoid(o_ref[0])
    rw = (_group_layernorm(yf_ref[0, 0] + yb_ref[0, 0], avg_rw_ref[...], rwg_ref[...]) + bonus_ref[0]) * gate_ref[0]
    mix = jnp.concatenate([ml, rw], axis=1).astype(BF16)
    _residual_mlp(mix, x_ref[0], mod_ref[0], gf_ref[...], wo_ref, w1_ref, w2_ref, out_ref)


def _att_post_kernel(ow_ref, og_ref, x_ref, mod_ref, gf_ref, wo_ref, w1_ref, w2_ref, out_ref):
    mix = jnp.concatenate([ow_ref[0], og_ref[0]], axis=1)
    _residual_mlp(mix, x_ref[0], mod_ref[0], gf_ref[...], wo_ref, w1_ref, w2_ref, out_ref)


def _post_call(kernel, name, token_inputs, xs, mods, consts, n_rows, n_lat_tiles):
    bsz, _, d = xs.shape
    tm = ROW_TILE
    mod_row = lambda b, i: (jnp.where(i >= n_lat_tiles, bsz, b), 0, 0)
    in_specs, args = [], []
    for arr, lead in token_inputs:
        if lead is None:
            in_specs.append(pl.BlockSpec((1, tm, arr.shape[-1]), lambda b, i: (b, i, 0)))
        else:
            in_specs.append(pl.BlockSpec((1, 1, tm, arr.shape[-1]), lambda b, i, lead=lead: (lead, b, i, 0)))
        args.append(arr)
    in_specs += [pl.BlockSpec((1, tm, d), lambda b, i: (b, i, 0)),
                 pl.BlockSpec((1, 1, mods.shape[-1]), mod_row)]
    in_specs += [_resident(c.shape) for c in consts]
    return pl.pallas_call(
        kernel,
        grid=(bsz, n_rows // tm),
        in_specs=in_specs,
        out_specs=pl.BlockSpec((1, tm, d), lambda b, i: (b, i, 0)),
        out_shape=jax.ShapeDtypeStruct((bsz, n_rows, d), F32),
        compiler_params=_params("parallel", "parallel"),
        name=name,
    )(*args, xs, mods, *consts)


ATT_QW, ATT_QG, ATT_KW, ATT_KG, ATT_VW, ATT_VG, ATT_END = 0, 512, 1024, 1152, 1280, 1408, 1536
ATT_QK_END = ATT_VW


def _att_inproj_kernel(x_ref, mod_ref, g_ref, w_ref, qkg_ref, cos_ref, sin_ref, sq_ref,
                       qw_ref, qg_ref, kw_ref, kg_ref, vw_ref, vg_ref):
    d = x_ref.shape[-1]
    mod = mod_ref[0]
    h = _norm_mod(x_ref[0], g_ref[...], mod[:, 0:d], mod[:, d:2 * d]).astype(BF16)
    half = ATT_HEAD_DIM // 2
    cos2, sin2 = cos_ref[0], sin_ref[0]

    def norm_rope(lo, hi):
        z = _dot(h, w_ref[:, lo:hi])
        n = hi - lo
        sq = sq_ref[...]
        z = z * lax.rsqrt(_dot_split_lhs(z * z, sq[:n, :n], 2) + NORM_EPS) * qkg_ref[:, lo:hi]
        reps = n // LANES
        cos = jnp.concatenate([cos2] * reps, axis=1)
        sin = jnp.concatenate([sin2] * reps, axis=1)
        lane = lax.broadcasted_iota(jnp.int32, z.shape, 1)
        partner = jnp.where(lane % ATT_HEAD_DIM < half, pltpu.roll(z, n - half, 1), pltpu.roll(z, half, 1))
        return (z * cos + partner * sin).astype(BF16)

    qw_ref[0] = norm_rope(ATT_QW, ATT_QG)
    qg_ref[0] = norm_rope(ATT_QG, ATT_KW)
    kk = norm_rope(ATT_KW, ATT_VW)
    kw_ref[0] = kk[:, :KV_WIDTH]
    kg_ref[0] = kk[:, KV_WIDTH:]
    vv = _dot(h, w_ref[:, ATT_VW:ATT_END]).astype(BF16)
    vw_ref[0] = vv[:, :KV_WIDTH]
    vg_ref[0] = vv[:, KV_WIDTH:]


def _att_inproj(xs, mods, gain, w, qk_gain, cos, sin, sq_avg, n_lat_tiles):
    bsz, s, d = xs.shape
    tm = ROW_TILE
    tok = lambda c: pl.BlockSpec((1, tm, c), lambda b, i: (b, i, 0))
    mod_row = lambda b, i: (jnp.where(i >= n_lat_tiles, bsz, b), 0, 0)
    rope = pl.BlockSpec((1, tm, LANES), lambda b, i: (0, i, 0))
    qw = Q_HEADS * ATT_HEAD_DIM
    return pl.pallas_call(
        _att_inproj_kernel,
        grid=(bsz, s // tm),
        in_specs=[tok(d), pl.BlockSpec((1, 1, mods.shape[-1]), mod_row), _resident((1, d)), _resident(w.shape),
                  _resident(qk_gain.shape), rope, rope, _resident(sq_avg.shape)],
        out_specs=[tok(qw), tok(qw), tok(KV_WIDTH), tok(KV_WIDTH), tok(KV_WIDTH), tok(KV_WIDTH)],
        out_shape=[jax.ShapeDtypeStruct((bsz, s, c), BF16) for c in (qw, qw, KV_WIDTH, KV_WIDTH, KV_WIDTH, KV_WIDTH)],
        compiler_params=_params("parallel", "parallel"),
        name="att_inproj",
    )(xs, mods, gain, w, qk_gain, cos, sin, sq_avg)


def _stack_queries(q):
    lane = lax.broadcasted_iota(jnp.int32, (q.shape[0], LANES), 1)
    zero = jnp.zeros((q.shape[0], LANES), q.dtype)
    slabs = [q[:, g * LANES:(g + 1) * LANES] for g in range(KV_GROUP)]
    lo = [jnp.where(lane < ATT_HEAD_DIM, sl, zero) for sl in slabs]
    hi = [jnp.where(lane >= ATT_HEAD_DIM, sl, zero) for sl in slabs]
    return jnp.concatenate(lo + hi, axis=0)


def _unstack_outputs(o, rows):
    lane = lax.broadcasted_iota(jnp.int32, (rows, LANES), 1)
    slabs = [jnp.where(lane < ATT_HEAD_DIM, o[g * rows:(g + 1) * rows], o[(KV_GROUP + g) * rows:(KV_GROUP + g + 1) * rows])
             for g in range(KV_GROUP)]
    return jnp.concatenate(slabs, axis=1)


def _win_attn_kernel(q_ref, kp_ref, ko_ref, kn_ref, kc_ref, vp_ref, vo_ref, vn_ref, vc_ref, sink_ref, o_ref):
    L = Q_BLOCK
    i, nb = pl.program_id(1), pl.num_programs(1)
    lc = kc_ref.shape[1]
    qs = _stack_queries(q_ref[0])
    keys = jnp.concatenate([kp_ref[0], ko_ref[0], kn_ref[0], kc_ref[0]], axis=0)
    vals = jnp.concatenate([vp_ref[0], vo_ref[0], vn_ref[0], vc_ref[0]], axis=0)
    nk = 3 * L + lc
    row = lax.broadcasted_iota(jnp.int32, (L, nk), 0)
    col = lax.broadcasted_iota(jnp.int32, (L, nk), 1)
    off = col - L - row
    valid = jnp.logical_and(off <= WINDOW, off >= -WINDOW)
    valid = jnp.logical_and(valid, col >= jnp.where(i > 0, 0, L))
    valid = jnp.logical_and(valid, col < jnp.where(i < nb - 1, 3 * L, 2 * L))
    valid = jnp.logical_or(valid, col >= 3 * L)
    valid = jnp.concatenate([valid] * (2 * KV_GROUP), axis=0)
    s = jnp.where(valid, _dot_nt(qs, keys), MASK_VALUE)
    sink = sink_ref[...]
    m = jnp.maximum(jnp.max(s, axis=1, keepdims=True), sink)
    p = jnp.exp(s - m)
    den = jnp.sum(p, axis=1, keepdims=True) + jnp.exp(sink - m)
    o = _dot(p.astype(BF16), vals) / den
    o_ref[0] = _unstack_outputs(o, L).astype(BF16)


def _win_attn(q, k, v, sink_col, t_lat):
    bsz, s, qw = q.shape
    L = Q_BLOCK
    nb = t_lat // L
    lc = s - t_lat
    kv = lambda f: pl.BlockSpec((1, L, KV_WIDTH), lambda b, i: (b, f(i), 0))
    ctx = pl.BlockSpec((1, lc, KV_WIDTH), lambda b, i: (b, t_lat // lc, 0))
    nbr = [kv(lambda i: jnp.maximum(i - 1, 0)), kv(lambda i: i), kv(lambda i: jnp.minimum(i + 1, nb - 1)), ctx]
    return pl.pallas_call(
        _win_attn_kernel,
        grid=(bsz, nb),
        in_specs=[pl.BlockSpec((1, L, qw), lambda b, i: (b, i, 0))] + nbr + nbr + [_resident(sink_col.shape)],
        out_specs=pl.BlockSpec((1, L, qw), lambda b, i: (b, i, 0)),
        out_shape=jax.ShapeDtypeStruct((bsz, t_lat, qw), BF16),
        compiler_params=_params("parallel", "parallel"),
        name="window_attention",
    )(q, k, k, k, k, v, v, v, v, sink_col)


def _glb_attn_kernel(q_ref, k_ref, v_ref, o_ref, *, tk):
    L = q_ref.shape[1]
    qs = _stack_queries(q_ref[0])
    rows = qs.shape[0]

    def body(j, carry):
        m, l, acc = carry
        start = pl.multiple_of(j * tk, tk)
        s = _dot_nt(qs, k_ref[0, pl.ds(start, tk), :])
        m_new = jnp.maximum(m, jnp.max(s, axis=1, keepdims=True))
        corr = jnp.exp(m - m_new)
        p = jnp.exp(s - m_new)
        l = corr * l + jnp.sum(p, axis=1, keepdims=True)
        acc = corr * acc + _dot(p.astype(BF16), v_ref[0, pl.ds(start, tk), :])
        return m_new, l, acc

    init = (jnp.full((rows, 1), MASK_VALUE, F32), jnp.zeros((rows, 1), F32), jnp.zeros((rows, LANES), F32))
    _, l, acc = lax.fori_loop(0, k_ref.shape[1] // tk, body, init)
    o_ref[0] = _unstack_outputs(acc / l, L).astype(BF16)


def _glb_attn(q, k, v, t_lat):
    bsz, s, qw = q.shape
    L = Q_BLOCK
    tk = next(t for t in (768, 512, 384, 256, 128) if s % t == 0)
    whole = pl.BlockSpec((1, s, KV_WIDTH), lambda b, i: (b, 0, 0))
    return pl.pallas_call(
        functools.partial(_glb_attn_kernel, tk=tk),
        grid=(bsz, t_lat // L),
        in_specs=[pl.BlockSpec((1, L, qw), lambda b, i: (b, i, 0)), whole, whole],
        out_specs=pl.BlockSpec((1, L, qw), lambda b, i: (b, i, 0)),
        out_shape=jax.ShapeDtypeStruct((bsz, t_lat, qw), BF16),
        compiler_params=_params("parallel", "parallel"),
        name="global_attention",
    )(q, k, v)


def _block_diag_const(width, group, value):
    idx = np.arange(width) // group
    return jnp.asarray((idx[:, None] == idx[None, :]) * value, dtype=BF16)


def _head_perm():
    return np.concatenate([np.arange(0, ATT_HEAD_DIM, 2), np.arange(1, ATT_HEAD_DIM, 2)])


def _q_head_order():
    return [kv * KV_GROUP + g for g in range(KV_GROUP) for kv in range(KV_HEADS)]


def _rec_weights(w, gate_b, conv_w, w0, w2, a0, a2, k_k, k_a, r_k):
    o = np.cumsum((0,) + (ML_WIDTH,) * 4 + (ML_HEADS,) * 4 + (RW_WIDTH,) * 3
                  + (RW_DECAY_RANK,) * 2 + (RW_ICLR_RANK,) * 2 + (RW_GATE_RANK,))
    q, k, v, og = (w[:, o[i]:o[i + 1]] for i in range(4))
    gates = w[:, o[4]:o[8]]
    rkv = w[:, o[8]:o[11]]
    lora = w[:, o[11]:o[16]]
    gates_pad = jnp.pad(gates, ((0, 0), (0, LANES - N_GATES)))
    w_main = jnp.concatenate([q, v, og, rkv, lora, gates_pad], axis=1).astype(BF16)
    gb = gate_b.reshape(-1)
    gb_row = jnp.pad(gb, (0, LANES - N_GATES)).reshape(1, LANES)
    zeros = jnp.zeros((RW_DECAY_RANK, RW_WIDTH), F32)
    w2cat = jnp.concatenate([jnp.concatenate([w2[0], zeros], 1), jnp.concatenate([zeros, w2[1]], 1)], 0).astype(BF16)
    a2cat = jnp.concatenate([jnp.concatenate([a2[0], zeros], 1), jnp.concatenate([zeros, a2[1]], 1)], 0).astype(BF16)
    vecs = jnp.stack([k_k, k_a, r_k, w0[0], w0[1], a0[0], a0[1], jnp.zeros_like(k_k)])
    return dict(w_main=w_main, wkt=k.T.astype(BF16), wgt=gates.T.astype(BF16), gb_row=gb_row,
                gb_col=gb.reshape(N_GATES, 1), conv=conv_w, vecs=vecs, w2cat=w2cat, a2cat=a2cat)


def _att_weights(w, win_q_norm, win_k_norm, glb_q_norm, glb_k_norm, w_out):
    hd, perm = ATT_HEAD_DIM, _head_perm()
    qn, kn = Q_HEADS * hd, KV_HEADS * hd
    o = np.cumsum((0, qn, kn, kn, qn, kn, kn))
    qw, kw, vw, qg, kg, vg = (w[:, o[i]:o[i + 1]] for i in range(6))
    q_cols = np.concatenate([h * hd + perm for h in _q_head_order()])
    k_cols = np.concatenate([h * hd + perm for h in range(KV_HEADS)])
    w_all = jnp.concatenate([qw[:, q_cols], qg[:, q_cols], kw[:, k_cols], kg[:, k_cols], vw, vg], axis=1).astype(BF16)
    scale = hd ** -0.5
    gains = jnp.concatenate([jnp.tile(win_q_norm[perm], Q_HEADS) * scale, jnp.tile(glb_q_norm[perm], Q_HEADS) * scale,
                             jnp.tile(win_k_norm[perm], KV_HEADS), jnp.tile(glb_k_norm[perm], KV_HEADS)]).reshape(1, -1)
    out_rows = np.concatenate([h * hd + np.arange(hd) for h in _q_head_order()])
    wo = jnp.concatenate([w_out[:qn][out_rows], w_out[qn:][out_rows]], axis=0).astype(BF16)
    return w_all, gains, wo


def _rope_tables(t_lat, s):
    n_freq = ATT_HEAD_DIM // 4
    pos = jnp.arange(t_lat)
    inv_freq = ROPE_THETA ** (-jnp.arange(n_freq, dtype=F32) / n_freq)
    ang = jnp.concatenate([(pos // GRID_W).astype(F32)[:, None] * inv_freq,
                           (pos % GRID_W).astype(F32)[:, None] * inv_freq], axis=-1)
    cos, sin = jnp.cos(ang), jnp.sin(ang)
    cos_h = jnp.concatenate([cos, cos], axis=-1)
    sin_h = jnp.concatenate([-sin, sin], axis=-1)
    pad = ((0, s - t_lat), (0, 0))
    cos_h = jnp.pad(cos_h, pad, constant_values=1.0)
    sin_h = jnp.pad(sin_h, pad)
    return jnp.tile(cos_h, (1, 2))[None], jnp.tile(sin_h, (1, 2))[None]


def kernel(x, c, ctx, c_ctx, norm_mix, norm_ffn, mod_w, mod_b, out_w, ffn_w1, ffn_w2, rec_in_w, ml_gate_b, ml_norm_g,
           rw_conv, rw_w0, rw_w2, rw_a0, rw_a2, rw_g2, rw_kk, rw_ka, rw_rk, rw_norm_g, att_in_w, win_q_norm,
           win_k_norm, win_sink, glb_q_norm, glb_k_norm):
    bsz, t_lat, d = x.shape
    lc = ctx.shape[1]
    s = t_lat + lc
    depth = mod_w.shape[0]
    assert depth == 2 and bsz + 1 <= SUBLANES
    assert t_lat % ROW_TILE == 0 and lc % ROW_TILE == 0 and t_lat % lc == 0 and t_lat % GRID_W == 0
    n_lat_tiles = t_lat // ROW_TILE

    cvec = jnp.zeros((SUBLANES, d), F32).at[:bsz].set(c).at[bsz].set(c_ctx)
    mods = _modulation(cvec, mod_w, mod_b)[:, :bsz + 1].reshape(depth, bsz + 1, 1, 6 * d)
    xs = jnp.concatenate([x, ctx], axis=1)
    w1 = ffn_w1.astype(BF16)
    w2 = ffn_w2.astype(BF16)

    rw = _rec_weights(rec_in_w[0], ml_gate_b[0], rw_conv[0], rw_w0[0], rw_w2[0], rw_a0[0], rw_a2[0],
                      rw_kk[0], rw_ka[0], rw_rk[0])
    q, v, o, rkv, lora, gates, kt, gates_t = _rec_inproj(
        xs, mods[0], norm_mix[0].reshape(1, d), rw["w_main"], rw["wkt"], rw["wgt"], rw["gb_row"], rw["gb_col"],
        n_lat_tiles)
    h_f, h_b = _mlstm(q, v, kt, gates, gates_t, t_lat)
    ones64 = _block_diag_const(RW_WIDTH, RW_HEAD_DIM, 1.0)
    r, vr, kk, bonus, gate, lw, key, bv = _rw_prep(rkv, lora, rw["conv"], rw["vecs"], rw["w2cat"], rw["a2cat"],
                                                   rw_g2[0].astype(BF16), ones64, t_lat)
    y = _rw_scan(r, vr, kk, lw, key, bv, t_lat)
    consts = [norm_ffn[0].reshape(1, d), ml_norm_g[0].reshape(1, -1), rw_norm_g[0].reshape(1, -1),
              _block_diag_const(ML_WIDTH, ML_HEAD_DIM, 1.0 / ML_HEAD_DIM),
              _block_diag_const(RW_WIDTH, RW_HEAD_DIM, 1.0 / RW_HEAD_DIM),
              out_w[0].astype(BF16), w1[0], w2[0]]
    tokens = [(h_f, None), (h_b, None), (o, None), (y, 0), (y, 1), (bonus, None), (gate, None)]
    xs = _post_call(_rec_post_kernel, "rec_post", tokens, xs, mods[0], consts, s, n_lat_tiles)

    w_att, qk_gain, wo = _att_weights(att_in_w[0], win_q_norm[0], win_k_norm[0], glb_q_norm[0], glb_k_norm[0], out_w[1])
    cos, sin = _rope_tables(t_lat, s)
    sq_avg = _block_diag_const(Q_HEADS * ATT_HEAD_DIM, ATT_HEAD_DIM, 1.0 / ATT_HEAD_DIM)
    qw, qg, kw, kg, vw, vg = _att_inproj(xs, mods[1], norm_mix[1].reshape(1, d), w_att, qk_gain, cos, sin, sq_avg,
                                         n_lat_tiles)
    sink_col = jnp.repeat(win_sink[0][np.array([kv * KV_GROUP + g for kv in range(KV_HEADS) for g in range(KV_GROUP)])],
                          Q_BLOCK).reshape(-1, 1)
    o_win = _win_attn(qw, kw, vw, sink_col, t_lat)
    o_glb = _glb_attn(qg, kg, vg, t_lat)
    consts = [norm_ffn[1].reshape(1, d), wo, w1[1], w2[1]]
    return _post_call(_att_post_kernel, "att_post", [(o_win, None), (o_glb, None)], xs, mods[1], consts, t_lat,
                      n_lat_tiles)
```

```python
import functools

import numpy as np
import jax
import jax.numpy as jnp
from jax import lax
from jax.experimental import pallas as pl
from jax.experimental.pallas import tpu as pltpu

F32 = jnp.float32
BF16 = jnp.bfloat16

GRID_W = 64
NORM_EPS = 1e-6
GN_EPS = 64e-5
ROPE_THETA = 10000.0
ML_HEAD_DIM = 128
ML_HEADS = 4
ML_WIDTH = ML_HEADS * ML_HEAD_DIM
ML_CHUNK = 128
RW_HEAD_DIM = 64
RW_HEADS = 8
RW_WIDTH = RW_HEADS * RW_HEAD_DIM
RW_CHUNK = 64
RW_DECAY_RANK = 64
RW_ICLR_RANK = 64
RW_GATE_RANK = 128
ATT_HEAD_DIM = 64
Q_HEADS = 8
KV_HEADS = 2
KV_GROUP = Q_HEADS // KV_HEADS
KV_WIDTH = KV_HEADS * ATT_HEAD_DIM
WINDOW = 128
Q_BLOCK = 128

LANES = 128
SUBLANES = 8
VMEM_LIMIT_BYTES = 56 * 1024 * 1024
ROW_TILE = 256
MASK_VALUE = -1e30


def _params(*semantics):
    return pltpu.CompilerParams(dimension_semantics=semantics, vmem_limit_bytes=VMEM_LIMIT_BYTES)


def _resident(shape):
    zeros = (0,) * len(shape)
    return pl.BlockSpec(shape, lambda *_: zeros, pipeline_mode=pl.Buffered(1))


def _dot(a, b):
    return jnp.dot(a, b, preferred_element_type=F32)


def _dot_nt(a, b):
    return lax.dot_general(a, b, (((1,), (1,)), ((), ())), preferred_element_type=F32)


def _dot_tn(a, b):
    return lax.dot_general(a, b, (((0,), (0,)), ((), ())), preferred_element_type=F32)


def _split(x, n):
    parts, rest = [], x
    for _ in range(n):
        p = rest.astype(BF16)
        parts.append(p)
        rest = rest - p.astype(F32)
    return parts


def _dot_split_lhs(x, m, n):
    return sum(_dot(p, m) for p in _split(x, n))


def _dot_split_rhs(m, x, n):
    return sum(_dot(m, p) for p in _split(x, n))


def _dot_f32(a, b):
    ah, al = _split(a, 2)
    bh, bl = _split(b, 2)
    return _dot(ah, bh) + _dot(ah, bl) + _dot(al, bh)


def _log_sigmoid(x):
    return jnp.minimum(x, 0.0) - jnp.log1p(jnp.exp(-jnp.abs(x)))


def _norm_mod(x, gain, shift, scale):
    y = x * lax.rsqrt(jnp.mean(x * x, axis=-1, keepdims=True) + NORM_EPS)
    return (y * gain) * (1.0 + scale) + shift


def _group_layernorm(y, avg, gain):
    yc = y - _dot_split_lhs(y, avg, 2)
    var = _dot_split_lhs(yc * yc, avg, 2)
    return yc * lax.rsqrt(var + GN_EPS) * gain


def _mod_kernel(cv_ref, w_ref, b_ref, o_ref):
    cv = cv_ref[...]
    o_ref[0] = _dot_f32(cv * jax.nn.sigmoid(cv), w_ref[0]) + b_ref[0]


def _modulation(cvec, mod_w, mod_b):
    depth, d, n = mod_w.shape
    tn = n // 4
    return pl.pallas_call(
        _mod_kernel,
        grid=(depth, n // tn),
        in_specs=[pl.BlockSpec((SUBLANES, d), lambda l, j: (0, 0)),
                  pl.BlockSpec((1, d, tn), lambda l, j: (l, 0, j)),
                  pl.BlockSpec((1, 1, tn), lambda l, j: (l, 0, j))],
        out_specs=pl.BlockSpec((1, SUBLANES, tn), lambda l, j: (l, 0, j)),
        out_shape=jax.ShapeDtypeStruct((depth, SUBLANES, n), F32),
        compiler_params=_params("parallel", "parallel"),
        name="adaln_modulation",
    )(cvec, mod_w, mod_b.reshape(depth, 1, n))


REC_Q, REC_V, REC_O, REC_RKV, REC_LORA, REC_GATE, REC_END = 0, 512, 1024, 1536, 3072, 3456, 3584
N_GATES = 4 * ML_HEADS


def _rec_inproj_kernel(x_ref, mod_ref, g_ref, w_ref, wkt_ref, wgt_ref, gb_ref, gbt_ref,
                       q_ref, v_ref, o_ref, rkv_ref, lora_ref, gate_ref, kt_ref, gatet_ref):
    d = x_ref.shape[-1]
    mod = mod_ref[0]
    h = _norm_mod(x_ref[0], g_ref[...], mod[:, 0:d], mod[:, d:2 * d]).astype(BF16)
    q_ref[0] = _dot(h, w_ref[:, REC_Q:REC_V]).astype(BF16)
    v_ref[0] = _dot(h, w_ref[:, REC_V:REC_O]).astype(BF16)
    o_ref[0] = _dot(h, w_ref[:, REC_O:REC_RKV])
    rkv_ref[0] = _dot(h, w_ref[:, REC_RKV:REC_LORA])
    lora_ref[0] = _dot(h, w_ref[:, REC_LORA:REC_GATE])
    gate_ref[0] = _dot(h, w_ref[:, REC_GATE:REC_END]) + gb_ref[...]
    kt_ref[0] = _dot_nt(wkt_ref[...], h).astype(BF16)
    gatet_ref[0] = _dot_nt(wgt_ref[...], h) + gbt_ref[...]


def _rec_inproj(xs, mods, gain, w_main, wkt, wgt, gb_row, gb_col, n_lat_tiles):
    bsz, s, d = xs.shape
    tm = ROW_TILE
    tok = lambda c: pl.BlockSpec((1, tm, c), lambda b, i: (b, i, 0))
    mod_row = lambda b, i: (jnp.where(i >= n_lat_tiles, bsz, b), 0, 0)
    return pl.pallas_call(
        _rec_inproj_kernel,
        grid=(bsz, s // tm),
        in_specs=[tok(d),
                  pl.BlockSpec((1, 1, mods.shape[-1]), mod_row),
                  _resident((1, d)), _resident(w_main.shape), _resident(wkt.shape), _resident(wgt.shape),
                  _resident(gb_row.shape), _resident(gb_col.shape)],
        out_specs=[tok(ML_WIDTH), tok(ML_WIDTH), tok(ML_WIDTH), tok(3 * RW_WIDTH), tok(REC_GATE - REC_LORA),
                   tok(LANES),
                   pl.BlockSpec((1, ML_WIDTH, tm), lambda b, i: (b, 0, i)),
                   pl.BlockSpec((1, N_GATES, tm), lambda b, i: (b, 0, i))],
        out_shape=[jax.ShapeDtypeStruct((bsz, s, ML_WIDTH), BF16),
                   jax.ShapeDtypeStruct((bsz, s, ML_WIDTH), BF16),
                   jax.ShapeDtypeStruct((bsz, s, ML_WIDTH), F32),
                   jax.ShapeDtypeStruct((bsz, s, 3 * RW_WIDTH), F32),
                   jax.ShapeDtypeStruct((bsz, s, REC_GATE - REC_LORA), F32),
                   jax.ShapeDtypeStruct((bsz, s, LANES), F32),
                   jax.ShapeDtypeStruct((bsz, ML_WIDTH, s), BF16),
                   jax.ShapeDtypeStruct((bsz, N_GATES, s), F32)],
        compiler_params=_params("parallel", "parallel"),
        name="rec_inproj",
    )(xs, mods, gain, w_main, wkt, wgt, gb_row, gb_col)


def _mlstm_kernel(qf_ref, vf_ref, ktf_ref, gf_ref, gtf_ref, qb_ref, vb_ref, ktb_ref, gb_ref, gtb_ref,
                  hf_ref, hb_ref, c_ref, m_ref):
    @pl.when(pl.program_id(1) == 0)
    def _():
        c_ref[...] = jnp.zeros_like(c_ref)
        m_ref[...] = jnp.zeros_like(m_ref)

    L, dh = ML_CHUNK, ML_HEAD_DIM
    t_idx = lax.broadcasted_iota(jnp.int32, (L, L), 0)
    s_idx = lax.broadcasted_iota(jnp.int32, (L, L), 1)
    ones_col = (lax.broadcasted_iota(jnp.int32, (L, LANES), 1) == 0).astype(BF16)
    scale = dh ** -0.5
    streams = ((qf_ref, vf_ref, ktf_ref, gf_ref, gtf_ref, hf_ref),
               (qb_ref, vb_ref, ktb_ref, gb_ref, gtb_ref, hb_ref))
    chains = []
    for d, (q_ref, v_ref, kt_ref, g_ref, gt_ref, h_ref) in enumerate(streams):
        earlier = (s_idx <= t_idx) if d == 0 else (s_idx >= t_idx)
        tri = earlier.astype(BF16)
        tri_t = ((t_idx <= s_idx) if d == 0 else (t_idx >= s_idx)).astype(BF16)
        gt = gt_ref[0]
        lf_rows = _log_sigmoid(gt)
        b_cols = _dot_split_rhs(tri, _log_sigmoid(g_ref[0]), 3)
        b_rows = _dot_split_lhs(lf_rows, tri_t, 3)
        tot = jnp.sum(lf_rows, axis=1, keepdims=True)
        q, v, kt = q_ref[0], v_ref[0], kt_ref[0]
        for hh in range(ML_HEADS):
            ic, fc = 2 * ML_HEADS * d + hh, 2 * ML_HEADS * d + ML_HEADS + hh
            chains.append(dict(j=d * ML_HEADS + hh, hh=hh, h_ref=h_ref, earlier=earlier,
                               b_col=b_cols[:, fc:fc + 1], b_row=b_rows[fc:fc + 1, :], i_row=gt[ic:ic + 1, :],
                               b_end=tot[fc:fc + 1, :], q=q[:, hh * dh:(hh + 1) * dh], kt=kt[hh * dh:(hh + 1) * dh, :],
                               v_aug=jnp.concatenate([v[:, hh * dh:(hh + 1) * dh], ones_col], axis=1)))
    for ch in chains:
        ch["cst"] = c_ref[ch["j"]]
        ch["qk"] = _dot(ch["q"], ch["kt"])
        ch["qc"] = _dot(ch["q"], ch["cst"].astype(BF16))
    for ch in chains:
        m_prev = m_ref[ch["j"]:ch["j"] + 1, 0:1]
        d_in = jnp.where(ch["earlier"], ch["b_col"] - ch["b_row"] + ch["i_row"], MASK_VALUE)
        d_prev = ch["b_col"] + m_prev
        m_t = jnp.maximum(d_prev, jnp.max(d_in, axis=1, keepdims=True))
        ch["s"] = (ch["qk"] * (scale * jnp.exp(d_in - m_t))).astype(BF16)
        ch["w_prev"], ch["floor"] = jnp.exp(d_prev - m_t), jnp.exp(-m_t)
        d_end = ch["b_end"] - ch["b_row"] + ch["i_row"]
        m_new = jnp.maximum(ch["b_end"] + m_prev, jnp.max(d_end, axis=1, keepdims=True))
        ch["kw"] = (ch["kt"].astype(F32) * (jnp.exp(d_end - m_new) * scale)).astype(BF16)
        ch["decay"] = jnp.exp(ch["b_end"] + m_prev - m_new)
        m_ref[ch["j"]:ch["j"] + 1, :] = jnp.broadcast_to(m_new, (1, LANES))
    for ch in chains:
        acc = _dot(ch["s"], ch["v_aug"]) + ch["w_prev"] * ch["qc"]
        den = jnp.maximum(jnp.abs(acc[:, dh:dh + 1]), ch["floor"])
        ch["h_ref"][0, :, ch["hh"] * dh:(ch["hh"] + 1) * dh] = acc[:, :dh] / den
    for ch in chains:
        c_ref[ch["j"]] = ch["decay"] * ch["cst"] + _dot(ch["kw"], ch["v_aug"])


def _chunk_order(n_lat, n_ctx):
    n = n_lat + n_ctx
    fwd = lambda c: (c + n_lat) % n
    bwd = lambda c: n - 1 - c
    return fwd, bwd


def _mlstm(q, v, kt, gates, gates_t, t_lat):
    bsz, s, _ = q.shape
    L = ML_CHUNK
    fwd, bwd = _chunk_order(t_lat // L, (s - t_lat) // L)
    specs = []
    for order in (fwd, bwd):
        tok = lambda c, order=order: pl.BlockSpec((1, L, c), lambda b, i: (b, order(i), 0))
        specs += [tok(ML_WIDTH), tok(ML_WIDTH),
                  pl.BlockSpec((1, ML_WIDTH, L), lambda b, i, order=order: (b, 0, order(i))),
                  tok(LANES),
                  pl.BlockSpec((1, N_GATES, L), lambda b, i, order=order: (b, 0, order(i)))]
    n_chain = 2 * ML_HEADS
    return pl.pallas_call(
        _mlstm_kernel,
        grid=(bsz, s // L),
        in_specs=specs,
        out_specs=[pl.BlockSpec((1, L, ML_WIDTH), lambda b, i: (b, fwd(i), 0)),
                   pl.BlockSpec((1, L, ML_WIDTH), lambda b, i: (b, bwd(i), 0))],
        out_shape=[jax.ShapeDtypeStruct((bsz, s, ML_WIDTH), F32)] * 2,
        scratch_shapes=[pltpu.VMEM((n_chain, ML_HEAD_DIM, ML_HEAD_DIM + LANES), F32),
                        pltpu.VMEM((n_chain, LANES), F32)],
        compiler_params=_params("parallel", "arbitrary"),
        name="mlstm_bidir",
    )(q, v, kt, gates, gates_t, q, v, kt, gates, gates_t)


def _rw_prep_kernel(z_ref, zp_ref, zn_ref, lora_ref, conv_ref, vec_ref, w2_ref, a2_ref, g2_ref, ones_ref,
                    r_ref, v_ref, kk_ref, bonus_ref, gate_ref, lw_ref, key_ref, bv_ref, *, t_lat, s_tot):
    tm, w = z_ref.shape[1], RW_WIDTH
    row0 = pl.program_id(1) * tm
    has_prev = jnp.where(jnp.logical_and(row0 != 0, row0 != t_lat), 1.0, 0.0)
    has_next = jnp.where(jnp.logical_and(row0 + tm != t_lat, row0 + tm != s_tot), 1.0, 0.0)
    z = z_ref[0]
    prev_row = zp_ref[0][SUBLANES - 1:SUBLANES, :] * has_prev
    next_row = zn_ref[0][0:1, :] * has_next
    ridx = lax.broadcasted_iota(jnp.int32, z.shape, 0)
    z_prev = jnp.where(ridx == 0, prev_row, pltpu.roll(z, 1, 0))
    z_next = jnp.where(ridx == tm - 1, next_row, pltpu.roll(z, tm - 1, 0))
    cw = conv_ref[...]
    zc = cw[0:1] * z_prev + cw[1:2] * z + cw[2:3] * z_next
    r, kr, vr = zc[:, :w], zc[:, w:2 * w], zc[:, 2 * w:]
    vec = vec_ref[...]
    k_k, k_a, r_k = vec[0:1], vec[1:2], vec[2:3]
    ones = ones_ref[...]
    kkr = kr * k_k
    kk = kkr * lax.rsqrt(_dot_split_lhs(kkr * kkr, ones, 2) + NORM_EPS)
    r_ref[0] = r
    v_ref[0] = vr
    kk_ref[0] = kk
    bonus_ref[0] = _dot_split_lhs(r * kr * r_k, ones, 2) * vr
    lora = lora_ref[0]
    dec = _dot(jnp.tanh(lora[:, 0:LANES]).astype(BF16), w2_ref[...])
    icl = _dot(lora[:, LANES:2 * LANES].astype(BF16), a2_ref[...])
    gate_ref[0] = _dot(jax.nn.sigmoid(lora[:, 2 * LANES:3 * LANES]).astype(BF16), g2_ref[...])
    for d in range(2):
        w_log = _log_sigmoid(vec[3 + d:4 + d] + dec[:, d * w:(d + 1) * w]) - 0.5
        a = jax.nn.sigmoid(vec[5 + d:6 + d] + icl[:, d * w:(d + 1) * w])
        lw_ref[d, 0] = -jnp.exp(w_log)
        key_ref[d, 0] = kr * (1.0 + (a - 1.0) * k_a)
        bv_ref[d, 0] = kk * a


def _rw_prep(rkv, lora, conv_w, vecs, w2cat, a2cat, g2, ones64, t_lat):
    bsz, s, _ = rkv.shape
    tm, w = ROW_TILE, RW_WIDTH
    nsub = tm // SUBLANES
    tok = lambda c: pl.BlockSpec((1, tm, c), lambda b, i: (b, i, 0))
    dirtok = pl.BlockSpec((2, 1, tm, w), lambda b, i: (0, b, i, 0))
    last = s // SUBLANES - 1
    return pl.pallas_call(
        functools.partial(_rw_prep_kernel, t_lat=t_lat, s_tot=s),
        grid=(bsz, s // tm),
        in_specs=[tok(3 * w),
                  pl.BlockSpec((1, SUBLANES, 3 * w), lambda b, i: (b, jnp.maximum(i * nsub - 1, 0), 0)),
                  pl.BlockSpec((1, SUBLANES, 3 * w), lambda b, i: (b, jnp.minimum((i + 1) * nsub, last), 0)),
                  tok(lora.shape[-1]),
                  _resident(conv_w.shape), _resident(vecs.shape), _resident(w2cat.shape), _resident(a2cat.shape),
                  _resident(g2.shape), _resident(ones64.shape)],
        out_specs=[tok(w)] * 5 + [dirtok] * 3,
        out_shape=[jax.ShapeDtypeStruct((bsz, s, w), F32)] * 5 + [jax.ShapeDtypeStruct((2, bsz, s, w), F32)] * 3,
        compiler_params=_params("parallel", "parallel"),
        name="rwkv_prep",
    )(rkv, rkv, rkv, lora, conv_w, vecs, w2cat, a2cat, g2, ones64)


def _rw_scan_kernel(rf_ref, vf_ref, kkf_ref, lwf_ref, keyf_ref, bvf_ref, rb_ref, vb_ref, kkb_ref, lwb_ref, keyb_ref,
                    bvb_ref, yf_ref, yb_ref, s_ref):
    @pl.when(pl.program_id(1) == 0)
    def _():
        s_ref[...] = jnp.zeros_like(s_ref)

    C, n = RW_CHUNK, RW_HEAD_DIM
    t2 = lax.broadcasted_iota(jnp.int32, (2 * C, 2 * C), 0) % C
    s2 = lax.broadcasted_iota(jnp.int32, (2 * C, 2 * C), 1) % C
    lower = lax.broadcasted_iota(jnp.int32, (2 * C, 2 * C), 0) >= C

    chains = []
    for d, (r_ref, v_ref, kk_ref, lw_ref, key_ref, bv_ref) in enumerate(
            ((rf_ref, vf_ref, kkf_ref, lwf_ref, keyf_ref, bvf_ref), (rb_ref, vb_ref, kkb_ref, lwb_ref, keyb_ref, bvb_ref))):
        rel = (s2 - t2) if d == 0 else (t2 - s2)
        keep = rel < jnp.where(lower, 1, 0)
        tri = (rel[:C, :C] <= 0).astype(BF16)
        lw = lw_ref[0, 0]
        cum = _dot_split_rhs(tri, lw, 3)
        tot = jnp.sum(lw, axis=0, keepdims=True)
        r, v, kk, key, bv = r_ref[0], v_ref[0], kk_ref[0], key_ref[0, 0], bv_ref[0, 0]
        e_neg = jnp.exp(-cum)
        e_end = jnp.exp(tot - cum)
        alpha = (kk * jnp.exp(cum - lw)).astype(BF16)
        rho = (r * jnp.exp(cum)).astype(BF16)
        beta = (bv * e_neg).astype(BF16)
        kappa = (key * e_neg).astype(BF16)
        beta_end = (bv * e_end).astype(BF16)
        kappa_end = (key * e_end).astype(BF16)
        gamma = jnp.exp(tot)
        vb = v.astype(BF16)
        for h in range(RW_HEADS):
            sl = slice(h * n, (h + 1) * n)
            chains.append(dict(j=d * RW_HEADS + h, keep=keep, a=alpha[:, sl], rho=rho[:, sl], v=vb[:, sl],
                               lhs=jnp.concatenate([alpha[:, sl], rho[:, sl]], axis=0),
                               rhs=jnp.concatenate([beta[:, sl], kappa[:, sl]], axis=0),
                               end=jnp.concatenate([beta_end[:, sl], kappa_end[:, sl]], axis=0),
                               gamma=gamma[:, sl]))

    for ch in chains:
        ch["big"] = jnp.where(ch["keep"], _dot_nt(ch["lhs"], ch["rhs"]), 0.0).astype(BF16)
    for ch in chains:
        ch["lmv"] = _dot(ch["big"][:, C:], ch["v"])
    tt = lax.broadcasted_iota(jnp.int32, (C, C), 0)
    ss = lax.broadcasted_iota(jnp.int32, (C, C), 1)
    eye = (tt == ss).astype(F32)
    for ch in chains:
        ch["lab"] = ch["big"][:C, :C]
        ch["inv"] = eye - jnp.where((tt >> 1) == (ss >> 1), ch["lab"], 0.0).astype(F32)
    for k in range(1, 6):
        couples = jnp.logical_and((tt >> (k + 1)) == (ss >> (k + 1)), (tt >> k) != (ss >> k))
        for ch in chains:
            ch["ed"] = _dot(jnp.where(couples, ch["lab"], 0.0).astype(BF16), ch["inv"].astype(BF16)).astype(BF16)
        for ch in chains:
            ch["inv"] = ch["inv"] - _dot(ch["inv"].astype(BF16), ch["ed"])
    for ch in chains:
        x = -jnp.concatenate([ch["a"].astype(F32), ch["lmv"][:C]], axis=1)
        ch["x"] = _dot(ch["inv"].astype(BF16), x.astype(BF16))
    for ch in chains:
        ch["st"] = s_ref[ch["j"]]
        ch["ws"] = _dot_nt(jnp.concatenate([ch["x"][:, :n].astype(BF16), ch["rho"]], axis=0), ch["st"].astype(BF16))
    for ch in chains:
        ch["u"] = (ch["ws"][:C] + ch["x"][:, n:]).astype(BF16)
    for ch in chains:
        ch["y"] = ch["ws"][C:] + _dot(ch["big"][C:, :C], ch["u"]) + ch["lmv"][C:]
    for ch in chains:
        upd = _dot_tn(jnp.concatenate([ch["u"], ch["v"]], axis=0), ch["end"])
        s_ref[ch["j"]] = ch["st"] * ch["gamma"] + upd
    yf_ref[0] = jnp.concatenate([ch["y"] for ch in chains[:RW_HEADS]], axis=1)
    yb_ref[0] = jnp.concatenate([ch["y"] for ch in chains[RW_HEADS:]], axis=1)


def _rw_scan(r, v, kk, lw, key, bv, t_lat):
    bsz, s, w = r.shape
    C = RW_CHUNK
    orders = _chunk_order(t_lat // C, (s - t_lat) // C)
    specs = []
    for d, order in enumerate(orders):
        tok = pl.BlockSpec((1, C, w), lambda b, c, order=order: (b, order(c), 0))
        dirtok = pl.BlockSpec((1, 1, C, w), lambda b, c, order=order, d=d: (d, b, order(c), 0))
        specs += [tok, tok, tok, dirtok, dirtok, dirtok]
    return pl.pallas_call(
        _rw_scan_kernel,
        grid=(bsz, s // C),
        in_specs=specs,
        out_specs=[specs[0], specs[6]],
        out_shape=[jax.ShapeDtypeStruct((bsz, s, w), F32)] * 2,
        scratch_shapes=[pltpu.VMEM((2 * RW_HEADS, RW_HEAD_DIM, RW_HEAD_DIM), F32)],
        compiler_params=_params("parallel", "arbitrary"),
        name="rwkv_scan",
    )(r, v, kk, lw, key, bv, r, v, kk, lw, key, bv)


def _residual_mlp(mix, x, mod, g_ffn, wo_ref, w1_ref, w2_ref, out_ref):
    d = x.shape[-1]
    x1 = x + mod[:, 2 * d:3 * d] * _dot(mix, wo_ref[...])
    h2 = _norm_mod(x1, g_ffn, mod[:, 3 * d:4 * d], mod[:, 4 * d:5 * d]).astype(BF16)
    acc = jnp.zeros_like(x1)
    for k in range(w1_ref.shape[1] // d):
        hid = jnp.maximum(_dot(h2, w1_ref[:, k * d:(k + 1) * d]), 0.0)
        acc = acc + _dot((hid * hid).astype(BF16), w2_ref[k * d:(k + 1) * d, :])
    out_ref[0] = x1 + mod[:, 5 * d:6 * d] * acc


def _rec_post_kernel(hf_ref, hb_ref, o_ref, yf_ref, yb_ref, bonus_ref, gate_ref, x_ref, mod_ref, gf_ref,
                     mlg_ref, rwg_ref, avg_ml_ref, avg_rw_ref, wo_ref, w1_ref, w2_ref, out_ref):
    ml = _group_layernorm(hf_ref[0] + hb_ref[0], avg_ml_ref[...], mlg_ref[...]) * jax.nn.sigmoid(o_ref[0])
    rw = (_group_layernorm(yf_ref[0] + yb_ref[0], avg_rw_ref[...], rwg_ref[...]) + bonus_ref[0]) * gate_ref[0]
    mix = jnp.concatenate([ml, rw], axis=1).astype(BF16)
    _residual_mlp(mix, x_ref[0], mod_ref[0], gf_ref[...], wo_ref, w1_ref, w2_ref, out_ref)


def _att_post_kernel(ow_ref, og_ref, x_ref, mod_ref, gf_ref, wo_ref, w1_ref, w2_ref, out_ref):
    mix = jnp.concatenate([ow_ref[0], og_ref[0]], axis=1)
    _residual_mlp(mix, x_ref[0], mod_ref[0], gf_ref[...], wo_ref, w1_ref, w2_ref, out_ref)


def _post_call(kernel, name, token_inputs, xs, mods, consts, n_rows, n_lat_tiles):
    bsz, _, d = xs.shape
    tm = ROW_TILE
    mod_row = lambda b, i: (jnp.where(i >= n_lat_tiles, bsz, b), 0, 0)
    tok = lambda c: pl.BlockSpec((1, tm, c), lambda b, i: (b, i, 0))
    in_specs = [tok(arr.shape[-1]) for arr in token_inputs]
    in_specs += [tok(d), pl.BlockSpec((1, 1, mods.shape[-1]), mod_row)]
    in_specs += [_resident(c.shape) for c in consts]
    args = list(token_inputs)
    return pl.pallas_call(
        kernel,
        grid=(bsz, n_rows // tm),
        in_specs=in_specs,
        out_specs=pl.BlockSpec((1, tm, d), lambda b, i: (b, i, 0)),
        out_shape=jax.ShapeDtypeStruct((bsz, n_rows, d), F32),
        compiler_params=_params("parallel", "parallel"),
        name=name,
    )(*args, xs, mods, *consts)


ATT_QW, ATT_QG, ATT_KW, ATT_KG, ATT_VW, ATT_VG, ATT_END = 0, 512, 1024, 1152, 1280, 1408, 1536
ATT_QK_END = ATT_VW


def _att_inproj_kernel(x_ref, mod_ref, g_ref, w_ref, qkg_ref, cos_ref, sin_ref, sq_ref,
                       qw_ref, qg_ref, kw_ref, kg_ref, vw_ref, vg0_ref, vg1_ref):
    d = x_ref.shape[-1]
    mod = mod_ref[0]
    h = _norm_mod(x_ref[0], g_ref[...], mod[:, 0:d], mod[:, d:2 * d]).astype(BF16)
    half = ATT_HEAD_DIM // 2
    cos2, sin2 = cos_ref[0], sin_ref[0]

    def norm_rope(lo, hi):
        z = _dot(h, w_ref[:, lo:hi])
        n = hi - lo
        sq = sq_ref[...]
        z = z * lax.rsqrt(_dot_split_lhs(z * z, sq[:n, :n], 2) + NORM_EPS) * qkg_ref[:, lo:hi]
        reps = n // LANES
        cos = jnp.concatenate([cos2] * reps, axis=1)
        sin = jnp.concatenate([sin2] * reps, axis=1)
        lane = lax.broadcasted_iota(jnp.int32, z.shape, 1)
        partner = jnp.where(lane % ATT_HEAD_DIM < half, pltpu.roll(z, n - half, 1), pltpu.roll(z, half, 1))
        return (z * cos + partner * sin).astype(BF16)

    qw_ref[0] = norm_rope(ATT_QW, ATT_QG)
    qg_ref[0] = norm_rope(ATT_QG, ATT_KW)
    kk = norm_rope(ATT_KW, ATT_VW)
    kw_ref[0] = kk[:, :KV_WIDTH]
    kg_ref[0] = kk[:, KV_WIDTH:]
    vv = _dot(h, w_ref[:, ATT_VW:ATT_END])
    vw_ref[0] = vv[:, :KV_WIDTH].astype(BF16)
    vg = vv[:, KV_WIDTH:]
    lane = lax.broadcasted_iota(jnp.int32, vg.shape, 1)
    vg0_ref[0] = jnp.where(lane < ATT_HEAD_DIM, vg, jnp.where(lane == ATT_HEAD_DIM, 1.0, 0.0)).astype(BF16)
    vg1_ref[0] = jnp.where(lane >= ATT_HEAD_DIM, vg, jnp.where(lane == 0, 1.0, 0.0)).astype(BF16)


def _att_inproj(xs, mods, gain, w, qk_gain, cos, sin, sq_avg, n_lat_tiles):
    bsz, s, d = xs.shape
    tm = ROW_TILE
    tok = lambda c: pl.BlockSpec((1, tm, c), lambda b, i: (b, i, 0))
    mod_row = lambda b, i: (jnp.where(i >= n_lat_tiles, bsz, b), 0, 0)
    rope = pl.BlockSpec((1, tm, LANES), lambda b, i: (0, i, 0))
    qw = Q_HEADS * ATT_HEAD_DIM
    return pl.pallas_call(
        _att_inproj_kernel,
        grid=(bsz, s // tm),
        in_specs=[tok(d), pl.BlockSpec((1, 1, mods.shape[-1]), mod_row), _resident((1, d)), _resident(w.shape),
                  _resident(qk_gain.shape), rope, rope, _resident(sq_avg.shape)],
        out_specs=[tok(qw), tok(qw)] + [tok(KV_WIDTH)] * 5,
        out_shape=[jax.ShapeDtypeStruct((bsz, s, c), BF16) for c in (qw, qw) + (KV_WIDTH,) * 5],
        compiler_params=_params("parallel", "parallel"),
        name="att_inproj",
    )(xs, mods, gain, w, qk_gain, cos, sin, sq_avg)


def _stack_queries(q):
    lane = lax.broadcasted_iota(jnp.int32, (q.shape[0], LANES), 1)
    zero = jnp.zeros((q.shape[0], LANES), q.dtype)
    slabs = [q[:, g * LANES:(g + 1) * LANES] for g in range(KV_GROUP)]
    lo = [jnp.where(lane < ATT_HEAD_DIM, sl, zero) for sl in slabs]
    hi = [jnp.where(lane >= ATT_HEAD_DIM, sl, zero) for sl in slabs]
    return jnp.concatenate(lo + hi, axis=0)


def _unstack_outputs(o, rows):
    lane = lax.broadcasted_iota(jnp.int32, (rows, LANES), 1)
    slabs = [jnp.where(lane < ATT_HEAD_DIM, o[g * rows:(g + 1) * rows], o[(KV_GROUP + g) * rows:(KV_GROUP + g + 1) * rows])
             for g in range(KV_GROUP)]
    return jnp.concatenate(slabs, axis=1)


def _win_attn_kernel(q_ref, kp_ref, ko_ref, kn_ref, kc_ref, vp_ref, vo_ref, vn_ref, vc_ref, sink_ref, o_ref):
    L = Q_BLOCK
    i, nb = pl.program_id(1), pl.num_programs(1)
    lc = kc_ref.shape[1]
    qs = _stack_queries(q_ref[0])
    keys = jnp.concatenate([kp_ref[0], ko_ref[0], kn_ref[0], kc_ref[0]], axis=0)
    vals = jnp.concatenate([vp_ref[0], vo_ref[0], vn_ref[0], vc_ref[0]], axis=0)
    nk = 3 * L + lc
    row = lax.broadcasted_iota(jnp.int32, (L, nk), 0)
    col = lax.broadcasted_iota(jnp.int32, (L, nk), 1)
    off = col - L - row
    valid = jnp.logical_and(off <= WINDOW, off >= -WINDOW)
    valid = jnp.logical_and(valid, col >= jnp.where(i > 0, 0, L))
    valid = jnp.logical_and(valid, col < jnp.where(i < nb - 1, 3 * L, 2 * L))
    valid = jnp.logical_or(valid, col >= 3 * L)
    valid = jnp.concatenate([valid] * (2 * KV_GROUP), axis=0)
    s = jnp.where(valid, _dot_nt(qs, keys), MASK_VALUE)
    sink = sink_ref[...]
    m = jnp.maximum(jnp.max(s, axis=1, keepdims=True), sink)
    p = jnp.exp(s - m)
    den = jnp.sum(p, axis=1, keepdims=True) + jnp.exp(sink - m)
    o = _dot(p.astype(BF16), vals) / den
    o_ref[0] = _unstack_outputs(o, L).astype(BF16)


def _win_attn(q, k, v, sink_col, t_lat):
    bsz, s, qw = q.shape
    L = Q_BLOCK
    nb = t_lat // L
    lc = s - t_lat
    kv = lambda f: pl.BlockSpec((1, L, KV_WIDTH), lambda b, i: (b, f(i), 0))
    ctx = pl.BlockSpec((1, lc, KV_WIDTH), lambda b, i: (b, t_lat // lc, 0))
    nbr = [kv(lambda i: jnp.maximum(i - 1, 0)), kv(lambda i: i), kv(lambda i: jnp.minimum(i + 1, nb - 1)), ctx]
    return pl.pallas_call(
        _win_attn_kernel,
        grid=(bsz, nb),
        in_specs=[pl.BlockSpec((1, L, qw), lambda b, i: (b, i, 0))] + nbr + nbr + [_resident(sink_col.shape)],
        out_specs=pl.BlockSpec((1, L, qw), lambda b, i: (b, i, 0)),
        out_shape=jax.ShapeDtypeStruct((bsz, t_lat, qw), BF16),
        compiler_params=_params("parallel", "parallel"),
        name="window_attention",
    )(q, k, k, k, k, v, v, v, v, sink_col)


def _glb_attn_kernel(q_ref, k_ref, v0_ref, v1_ref, o_ref, *, tk):
    L = q_ref.shape[1]
    qs = _stack_queries(q_ref[0])
    rows = qs.shape[0]
    half = rows // 2

    def body(j, carry):
        m, acc = carry
        start = pl.multiple_of(j * tk, tk)
        s = _dot_nt(qs, k_ref[0, pl.ds(start, tk), :])
        m_new = jnp.maximum(m, jnp.max(s, axis=1, keepdims=True))
        p = jnp.exp2(s - m_new).astype(BF16)
        pv = jnp.concatenate([_dot(p[:half], v0_ref[0, pl.ds(start, tk), :]),
                              _dot(p[half:], v1_ref[0, pl.ds(start, tk), :])], axis=0)
        return m_new, jnp.exp2(m - m_new) * acc + pv

    init = (jnp.full((rows, 1), MASK_VALUE, F32), jnp.zeros((rows, LANES), F32))
    _, acc = lax.fori_loop(0, k_ref.shape[1] // tk, body, init)
    den = jnp.concatenate([acc[:half, ATT_HEAD_DIM:ATT_HEAD_DIM + 1], acc[half:, 0:1]], axis=0)
    o_ref[0] = _unstack_outputs(acc / den, L).astype(BF16)


GLB_KEY_TILES = (1408, 768, 512, 384, 256, 128)


def _glb_attn(q, k, v0, v1, t_lat):
    bsz, s, qw = q.shape
    L = Q_BLOCK
    tk = next(t for t in GLB_KEY_TILES if s % t == 0)
    whole = pl.BlockSpec((1, s, KV_WIDTH), lambda b, i: (b, 0, 0))
    return pl.pallas_call(
        functools.partial(_glb_attn_kernel, tk=tk),
        grid=(bsz, t_lat // L),
        in_specs=[pl.BlockSpec((1, L, qw), lambda b, i: (b, i, 0)), whole, whole, whole],
        out_specs=pl.BlockSpec((1, L, qw), lambda b, i: (b, i, 0)),
        out_shape=jax.ShapeDtypeStruct((bsz, t_lat, qw), BF16),
        compiler_params=_params("parallel", "parallel"),
        name="global_attention",
    )(q, k, v0, v1)


def _block_diag_const(width, group, value):
    idx = np.arange(width) // group
    return jnp.asarray((idx[:, None] == idx[None, :]) * value, dtype=BF16)


def _head_perm():
    return np.concatenate([np.arange(0, ATT_HEAD_DIM, 2), np.arange(1, ATT_HEAD_DIM, 2)])


def _q_head_order():
    return [kv * KV_GROUP + g for g in range(KV_GROUP) for kv in range(KV_HEADS)]


def _rec_weights(w, gate_b, conv_w, w0, w2, a0, a2, k_k, k_a, r_k):
    o = np.cumsum((0,) + (ML_WIDTH,) * 4 + (ML_HEADS,) * 4 + (RW_WIDTH,) * 3
                  + (RW_DECAY_RANK,) * 2 + (RW_ICLR_RANK,) * 2 + (RW_GATE_RANK,))
    q, k, v, og = (w[:, o[i]:o[i + 1]] for i in range(4))
    gates = w[:, o[4]:o[8]]
    rkv = w[:, o[8]:o[11]]
    lora = w[:, o[11]:o[16]]
    gates_pad = jnp.pad(gates, ((0, 0), (0, LANES - N_GATES)))
    w_main = jnp.concatenate([q, v, og, rkv, lora, gates_pad], axis=1).astype(BF16)
    gb = gate_b.reshape(-1)
    gb_row = jnp.pad(gb, (0, LANES - N_GATES)).reshape(1, LANES)
    zeros = jnp.zeros((RW_DECAY_RANK, RW_WIDTH), F32)
    w2cat = jnp.concatenate([jnp.concatenate([w2[0], zeros], 1), jnp.concatenate([zeros, w2[1]], 1)], 0).astype(BF16)
    a2cat = jnp.concatenate([jnp.concatenate([a2[0], zeros], 1), jnp.concatenate([zeros, a2[1]], 1)], 0).astype(BF16)
    vecs = jnp.stack([k_k, k_a, r_k, w0[0], w0[1], a0[0], a0[1], jnp.zeros_like(k_k)])
    return dict(w_main=w_main, wkt=k.T.astype(BF16), wgt=gates.T.astype(BF16), gb_row=gb_row,
                gb_col=gb.reshape(N_GATES, 1), conv=conv_w, vecs=vecs, w2cat=w2cat, a2cat=a2cat)


def _att_weights(w, win_q_norm, win_k_norm, glb_q_norm, glb_k_norm, w_out):
    hd, perm = ATT_HEAD_DIM, _head_perm()
    qn, kn = Q_HEADS * hd, KV_HEADS * hd
    o = np.cumsum((0, qn, kn, kn, qn, kn, kn))
    qw, kw, vw, qg, kg, vg = (w[:, o[i]:o[i + 1]] for i in range(6))
    q_cols = np.concatenate([h * hd + perm for h in _q_head_order()])
    k_cols = np.concatenate([h * hd + perm for h in range(KV_HEADS)])
    w_all = jnp.concatenate([qw[:, q_cols], qg[:, q_cols], kw[:, k_cols], kg[:, k_cols], vw, vg], axis=1).astype(BF16)
    scale = hd ** -0.5
    log2e = float(np.log2(np.e))
    gains = jnp.concatenate([jnp.tile(win_q_norm[perm], Q_HEADS) * scale,
                             jnp.tile(glb_q_norm[perm], Q_HEADS) * (scale * log2e),
                             jnp.tile(win_k_norm[perm], KV_HEADS), jnp.tile(glb_k_norm[perm], KV_HEADS)]).reshape(1, -1)
    out_rows = np.concatenate([h * hd + np.arange(hd) for h in _q_head_order()])
    wo = jnp.concatenate([w_out[:qn][out_rows], w_out[qn:][out_rows]], axis=0).astype(BF16)
    return w_all, gains, wo


def _rope_tables(t_lat, s):
    n_freq = ATT_HEAD_DIM // 4
    pos = jnp.arange(t_lat)
    inv_freq = ROPE_THETA ** (-jnp.arange(n_freq, dtype=F32) / n_freq)
    ang = jnp.concatenate([(pos // GRID_W).astype(F32)[:, None] * inv_freq,
                           (pos % GRID_W).astype(F32)[:, None] * inv_freq], axis=-1)
    cos, sin = jnp.cos(ang), jnp.sin(ang)
    cos_h = jnp.concatenate([cos, cos], axis=-1)
    sin_h = jnp.concatenate([-sin, sin], axis=-1)
    pad = ((0, s - t_lat), (0, 0))
    cos_h = jnp.pad(cos_h, pad, constant_values=1.0)
    sin_h = jnp.pad(sin_h, pad)
    return jnp.tile(cos_h, (1, 2))[None], jnp.tile(sin_h, (1, 2))[None]


def kernel(x, c, ctx, c_ctx, norm_mix, norm_ffn, mod_w, mod_b, out_w, ffn_w1, ffn_w2, rec_in_w, ml_gate_b, ml_norm_g,
           rw_conv, rw_w0, rw_w2, rw_a0, rw_a2, rw_g2, rw_kk, rw_ka, rw_rk, rw_norm_g, att_in_w, win_q_norm,
           win_k_norm, win_sink, glb_q_norm, glb_k_norm):
    bsz, t_lat, d = x.shape
    lc = ctx.shape[1]
    s = t_lat + lc
    depth = mod_w.shape[0]
    assert depth == 2 and bsz + 1 <= SUBLANES
    assert t_lat % ROW_TILE == 0 and lc % ROW_TILE == 0 and t_lat % lc == 0 and t_lat % GRID_W == 0
    n_lat_tiles = t_lat // ROW_TILE

    cvec = jnp.zeros((SUBLANES, d), F32).at[:bsz].set(c).at[bsz].set(c_ctx)
    mods = _modulation(cvec, mod_w, mod_b)[:, :bsz + 1].reshape(depth, bsz + 1, 1, 6 * d)
    xs = jnp.concatenate([x, ctx], axis=1)
    w1 = ffn_w1.astype(BF16)
    w2 = ffn_w2.astype(BF16)

    rw = _rec_weights(rec_in_w[0], ml_gate_b[0], rw_conv[0], rw_w0[0], rw_w2[0], rw_a0[0], rw_a2[0],
                      rw_kk[0], rw_ka[0], rw_rk[0])
    q, v, o, rkv, lora, gates, kt, gates_t = _rec_inproj(
        xs, mods[0], norm_mix[0].reshape(1, d), rw["w_main"], rw["wkt"], rw["wgt"], rw["gb_row"], rw["gb_col"],
        n_lat_tiles)
    h_f, h_b = _mlstm(q, v, kt, gates, gates_t, t_lat)
    ones64 = _block_diag_const(RW_WIDTH, RW_HEAD_DIM, 1.0)
    r, vr, kk, bonus, gate, lw, key, bv = _rw_prep(rkv, lora, rw["conv"], rw["vecs"], rw["w2cat"], rw["a2cat"],
                                                   rw_g2[0].astype(BF16), ones64, t_lat)
    y_f, y_b = _rw_scan(r, vr, kk, lw, key, bv, t_lat)
    consts = [norm_ffn[0].reshape(1, d), ml_norm_g[0].reshape(1, -1), rw_norm_g[0].reshape(1, -1),
              _block_diag_const(ML_WIDTH, ML_HEAD_DIM, 1.0 / ML_HEAD_DIM),
              _block_diag_const(RW_WIDTH, RW_HEAD_DIM, 1.0 / RW_HEAD_DIM),
              out_w[0].astype(BF16), w1[0], w2[0]]
    tokens = [h_f, h_b, o, y_f, y_b, bonus, gate]
    xs = _post_call(_rec_post_kernel, "rec_post", tokens, xs, mods[0], consts, s, n_lat_tiles)

    w_att, qk_gain, wo = _att_weights(att_in_w[0], win_q_norm[0], win_k_norm[0], glb_q_norm[0], glb_k_norm[0], out_w[1])
    cos, sin = _rope_tables(t_lat, s)
    sq_avg = _block_diag_const(Q_HEADS * ATT_HEAD_DIM, ATT_HEAD_DIM, 1.0 / ATT_HEAD_DIM)
    qw, qg, kw, kg, vw, vg0, vg1 = _att_inproj(xs, mods[1], norm_mix[1].reshape(1, d), w_att, qk_gain, cos, sin,
                                               sq_avg, n_lat_tiles)
    sink_col = jnp.repeat(win_sink[0][np.array([kv * KV_GROUP + g for kv in range(KV_HEADS) for g in range(KV_GROUP)])],
                          Q_BLOCK).reshape(-1, 1)
    o_win = _win_attn(qw, kw, vw, sink_col, t_lat)
    o_glb = _glb_attn(qg, kg, vg0, vg1, t_lat)
    consts = [norm_ffn[1].reshape(1, d), wo, w1[1], w2[1]]
    return _post_call(_att_post_kernel, "att_post", [o_win, o_glb], xs, mods[1], consts, t_lat,
                      n_lat_tiles)
```

```python
import functools

import numpy as np
import jax
import jax.numpy as jnp
from jax import lax
from jax.experimental import pallas as pl
from jax.experimental.pallas import tpu as pltpu

F32 = jnp.float32
BF16 = jnp.bfloat16

GRID_W = 64
NORM_EPS = 1e-6
GN_EPS = 64e-5
ROPE_THETA = 10000.0
ML_HEAD_DIM = 128
ML_HEADS = 4
ML_WIDTH = ML_HEADS * ML_HEAD_DIM
ML_CHUNK = 128
RW_HEAD_DIM = 64
RW_HEADS = 8
RW_WIDTH = RW_HEADS * RW_HEAD_DIM
RW_CHUNK = 64
RW_DECAY_RANK = 64
RW_ICLR_RANK = 64
RW_GATE_RANK = 128
ATT_HEAD_DIM = 64
Q_HEADS = 8
KV_HEADS = 2
KV_GROUP = Q_HEADS // KV_HEADS
KV_WIDTH = KV_HEADS * ATT_HEAD_DIM
WINDOW = 128
Q_BLOCK = 128

LANES = 128
SUBLANES = 8
VMEM_LIMIT_BYTES = 56 * 1024 * 1024
ROW_TILE = 256
MASK_VALUE = -1e30


def _params(*semantics):
    return pltpu.CompilerParams(dimension_semantics=semantics, vmem_limit_bytes=VMEM_LIMIT_BYTES)


def _resident(shape):
    zeros = (0,) * len(shape)
    return pl.BlockSpec(shape, lambda *_: zeros, pipeline_mode=pl.Buffered(1))


def _dot(a, b):
    return jnp.dot(a, b, preferred_element_type=F32)


def _dot_nt(a, b):
    return lax.dot_general(a, b, (((1,), (1,)), ((), ())), preferred_element_type=F32)


def _dot_tn(a, b):
    return lax.dot_general(a, b, (((0,), (0,)), ((), ())), preferred_element_type=F32)


def _split(x, n):
    parts, rest = [], x
    for _ in range(n):
        p = rest.astype(BF16)
        parts.append(p)
        rest = rest - p.astype(F32)
    return parts


def _dot_split_lhs(x, m, n):
    return sum(_dot(p, m) for p in _split(x, n))


def _dot_split_rhs(m, x, n):
    return sum(_dot(m, p) for p in _split(x, n))


def _dot_f32(a, b):
    ah, al = _split(a, 2)
    bh, bl = _split(b, 2)
    return _dot(ah, bh) + _dot(ah, bl) + _dot(al, bh)


def _log_sigmoid(x):
    return jnp.minimum(x, 0.0) - jnp.log1p(jnp.exp(-jnp.abs(x)))


def _norm_mod(x, gain, shift, scale):
    y = x * lax.rsqrt(jnp.mean(x * x, axis=-1, keepdims=True) + NORM_EPS)
    return (y * gain) * (1.0 + scale) + shift


def _group_layernorm(y, avg, gain):
    yc = y - _dot_split_lhs(y, avg, 2)
    var = _dot_split_lhs(yc * yc, avg, 2)
    return yc * lax.rsqrt(var + GN_EPS) * gain


def _mod_kernel(cv_ref, w_ref, b_ref, o_ref):
    cv = cv_ref[...]
    o_ref[0] = _dot_f32(cv * jax.nn.sigmoid(cv), w_ref[0]) + b_ref[0]


def _modulation(cvec, mod_w, mod_b):
    depth, d, n = mod_w.shape
    tn = n // 4
    return pl.pallas_call(
        _mod_kernel,
        grid=(depth, n // tn),
        in_specs=[pl.BlockSpec((SUBLANES, d), lambda l, j: (0, 0)),
                  pl.BlockSpec((1, d, tn), lambda l, j: (l, 0, j)),
                  pl.BlockSpec((1, 1, tn), lambda l, j: (l, 0, j))],
        out_specs=pl.BlockSpec((1, SUBLANES, tn), lambda l, j: (l, 0, j)),
        out_shape=jax.ShapeDtypeStruct((depth, SUBLANES, n), F32),
        compiler_params=_params("parallel", "parallel"),
        name="adaln_modulation",
    )(cvec, mod_w, mod_b.reshape(depth, 1, n))


REC_Q, REC_V, REC_O, REC_RKV, REC_LORA, REC_GATE, REC_END = 0, 512, 1024, 1536, 3072, 3456, 3584
N_GATES = 4 * ML_HEADS


def _rec_inproj_kernel(x_ref, mod_ref, g_ref, w_ref, wkt_ref, wgt_ref, gb_ref, gbt_ref,
                       q_ref, v_ref, o_ref, rkv_ref, lora_ref, gate_ref, kt_ref, gatet_ref):
    d = x_ref.shape[-1]
    mod = mod_ref[0]
    h = _norm_mod(x_ref[0], g_ref[...], mod[:, 0:d], mod[:, d:2 * d]).astype(BF16)
    q_ref[0] = _dot(h, w_ref[:, REC_Q:REC_V]).astype(BF16)
    v_ref[0] = _dot(h, w_ref[:, REC_V:REC_O]).astype(BF16)
    o_ref[0] = _dot(h, w_ref[:, REC_O:REC_RKV]).astype(BF16)
    rkv_ref[0] = _dot(h, w_ref[:, REC_RKV:REC_LORA])
    lora_ref[0] = _dot(h, w_ref[:, REC_LORA:REC_GATE])
    gate_ref[0] = _dot(h, w_ref[:, REC_GATE:REC_END]) + gb_ref[...]
    kt_ref[0] = _dot_nt(wkt_ref[...], h).astype(BF16)
    gatet_ref[0] = _dot_nt(wgt_ref[...], h) + gbt_ref[...]


def _rec_inproj(xs, mods, gain, w_main, wkt, wgt, gb_row, gb_col, n_lat_tiles):
    bsz, s, d = xs.shape
    tm = ROW_TILE
    tok = lambda c: pl.BlockSpec((1, tm, c), lambda b, i: (b, i, 0))
    mod_row = lambda b, i: (jnp.where(i >= n_lat_tiles, bsz, b), 0, 0)
    return pl.pallas_call(
        _rec_inproj_kernel,
        grid=(bsz, s // tm),
        in_specs=[tok(d),
                  pl.BlockSpec((1, 1, mods.shape[-1]), mod_row),
                  _resident((1, d)), _resident(w_main.shape), _resident(wkt.shape), _resident(wgt.shape),
                  _resident(gb_row.shape), _resident(gb_col.shape)],
        out_specs=[tok(ML_WIDTH), tok(ML_WIDTH), tok(ML_WIDTH), tok(3 * RW_WIDTH), tok(REC_GATE - REC_LORA),
                   tok(LANES),
                   pl.BlockSpec((1, ML_WIDTH, tm), lambda b, i: (b, 0, i)),
                   pl.BlockSpec((1, N_GATES, tm), lambda b, i: (b, 0, i))],
        out_shape=[jax.ShapeDtypeStruct((bsz, s, ML_WIDTH), BF16),
                   jax.ShapeDtypeStruct((bsz, s, ML_WIDTH), BF16),
                   jax.ShapeDtypeStruct((bsz, s, ML_WIDTH), BF16),
                   jax.ShapeDtypeStruct((bsz, s, 3 * RW_WIDTH), F32),
                   jax.ShapeDtypeStruct((bsz, s, REC_GATE - REC_LORA), F32),
                   jax.ShapeDtypeStruct((bsz, s, LANES), F32),
                   jax.ShapeDtypeStruct((bsz, ML_WIDTH, s), BF16),
                   jax.ShapeDtypeStruct((bsz, N_GATES, s), F32)],
        compiler_params=_params("parallel", "parallel"),
        name="rec_inproj",
    )(xs, mods, gain, w_main, wkt, wgt, gb_row, gb_col)


def _mlstm_kernel(qf_ref, vf_ref, ktf_ref, gf_ref, gtf_ref, qb_ref, vb_ref, ktb_ref, gb_ref, gtb_ref,
                  hf_ref, hb_ref, c_ref, m_ref):
    @pl.when(pl.program_id(1) == 0)
    def _():
        c_ref[...] = jnp.zeros_like(c_ref)
        m_ref[...] = jnp.zeros_like(m_ref)

    L, dh = ML_CHUNK, ML_HEAD_DIM
    t_idx = lax.broadcasted_iota(jnp.int32, (L, L), 0)
    s_idx = lax.broadcasted_iota(jnp.int32, (L, L), 1)
    ones_col = (lax.broadcasted_iota(jnp.int32, (L, LANES), 1) == 0).astype(BF16)
    scale = dh ** -0.5
    streams = ((qf_ref, vf_ref, ktf_ref, gf_ref, gtf_ref, hf_ref),
               (qb_ref, vb_ref, ktb_ref, gb_ref, gtb_ref, hb_ref))
    chains = []
    for d, (q_ref, v_ref, kt_ref, g_ref, gt_ref, h_ref) in enumerate(streams):
        earlier = (s_idx <= t_idx) if d == 0 else (s_idx >= t_idx)
        tri = earlier.astype(BF16)
        tri_t = ((t_idx <= s_idx) if d == 0 else (t_idx >= s_idx)).astype(BF16)
        gt = gt_ref[0]
        lf_rows = _log_sigmoid(gt)
        b_cols = _dot_split_rhs(tri, _log_sigmoid(g_ref[0]), 3)
        b_rows = _dot_split_lhs(lf_rows, tri_t, 3)
        tot = jnp.sum(lf_rows, axis=1, keepdims=True)
        q, v, kt = q_ref[0], v_ref[0], kt_ref[0]
        for hh in range(ML_HEADS):
            ic, fc = 2 * ML_HEADS * d + hh, 2 * ML_HEADS * d + ML_HEADS + hh
            chains.append(dict(j=d * ML_HEADS + hh, hh=hh, h_ref=h_ref, earlier=earlier,
                               b_col=b_cols[:, fc:fc + 1], b_row=b_rows[fc:fc + 1, :], i_row=gt[ic:ic + 1, :],
                               b_end=tot[fc:fc + 1, :], q=q[:, hh * dh:(hh + 1) * dh], kt=kt[hh * dh:(hh + 1) * dh, :],
                               v_aug=jnp.concatenate([v[:, hh * dh:(hh + 1) * dh], ones_col], axis=1)))
    for ch in chains:
        ch["cst"] = c_ref[ch["j"]]
        ch["qk"] = _dot(ch["q"], ch["kt"])
        ch["qc"] = _dot(ch["q"], ch["cst"].astype(BF16))
    for ch in chains:
        m_prev = m_ref[ch["j"]:ch["j"] + 1, 0:1]
        d_in = jnp.where(ch["earlier"], ch["b_col"] - ch["b_row"] + ch["i_row"], MASK_VALUE)
        d_prev = ch["b_col"] + m_prev
        m_t = jnp.maximum(d_prev, jnp.max(d_in, axis=1, keepdims=True))
        ch["s"] = (ch["qk"] * (scale * jnp.exp(d_in - m_t))).astype(BF16)
        ch["w_prev"], ch["floor"] = jnp.exp(d_prev - m_t), jnp.exp(-m_t)
        d_end = ch["b_end"] - ch["b_row"] + ch["i_row"]
        m_new = jnp.maximum(ch["b_end"] + m_prev, jnp.max(d_end, axis=1, keepdims=True))
        ch["kw"] = (ch["kt"].astype(F32) * (jnp.exp(d_end - m_new) * scale)).astype(BF16)
        ch["decay"] = jnp.exp(ch["b_end"] + m_prev - m_new)
        m_ref[ch["j"]:ch["j"] + 1, :] = jnp.broadcast_to(m_new, (1, LANES))
    for ch in chains:
        acc = _dot(ch["s"], ch["v_aug"]) + ch["w_prev"] * ch["qc"]
        den = jnp.maximum(jnp.abs(acc[:, dh:dh + 1]), ch["floor"])
        ch["h_ref"][0, :, ch["hh"] * dh:(ch["hh"] + 1) * dh] = acc[:, :dh] / den
    for ch in chains:
        c_ref[ch["j"]] = ch["decay"] * ch["cst"] + _dot(ch["kw"], ch["v_aug"])


def _chunk_order(n_lat, n_ctx):
    n = n_lat + n_ctx
    fwd = lambda c: (c + n_lat) % n
    bwd = lambda c: n - 1 - c
    return fwd, bwd


def _mlstm(q, v, kt, gates, gates_t, t_lat):
    bsz, s, _ = q.shape
    L = ML_CHUNK
    fwd, bwd = _chunk_order(t_lat // L, (s - t_lat) // L)
    specs = []
    for order in (fwd, bwd):
        tok = lambda c, order=order: pl.BlockSpec((1, L, c), lambda b, i: (b, order(i), 0))
        specs += [tok(ML_WIDTH), tok(ML_WIDTH),
                  pl.BlockSpec((1, ML_WIDTH, L), lambda b, i, order=order: (b, 0, order(i))),
                  tok(LANES),
                  pl.BlockSpec((1, N_GATES, L), lambda b, i, order=order: (b, 0, order(i)))]
    n_chain = 2 * ML_HEADS
    return pl.pallas_call(
        _mlstm_kernel,
        grid=(bsz, s // L),
        in_specs=specs,
        out_specs=[pl.BlockSpec((1, L, ML_WIDTH), lambda b, i: (b, fwd(i), 0)),
                   pl.BlockSpec((1, L, ML_WIDTH), lambda b, i: (b, bwd(i), 0))],
        out_shape=[jax.ShapeDtypeStruct((bsz, s, ML_WIDTH), F32)] * 2,
        scratch_shapes=[pltpu.VMEM((n_chain, ML_HEAD_DIM, ML_HEAD_DIM + LANES), F32),
                        pltpu.VMEM((n_chain, LANES), F32)],
        compiler_params=_params("parallel", "arbitrary"),
        name="mlstm_bidir",
    )(q, v, kt, gates, gates_t, q, v, kt, gates, gates_t)


def _rw_prep_kernel(z_ref, zp_ref, zn_ref, lora_ref, conv_ref, vec_ref, w2_ref, a2_ref, g2_ref, ones_ref,
                    r_ref, v_ref, kk_ref, bonus_ref, gate_ref, lw_ref, key_ref, bv_ref, *, t_lat, s_tot):
    tm, w = z_ref.shape[1], RW_WIDTH
    row0 = pl.program_id(1) * tm
    has_prev = jnp.where(jnp.logical_and(row0 != 0, row0 != t_lat), 1.0, 0.0)
    has_next = jnp.where(jnp.logical_and(row0 + tm != t_lat, row0 + tm != s_tot), 1.0, 0.0)
    z = z_ref[0]
    prev_row = zp_ref[0][SUBLANES - 1:SUBLANES, :] * has_prev
    next_row = zn_ref[0][0:1, :] * has_next
    ridx = lax.broadcasted_iota(jnp.int32, z.shape, 0)
    z_prev = jnp.where(ridx == 0, prev_row, pltpu.roll(z, 1, 0))
    z_next = jnp.where(ridx == tm - 1, next_row, pltpu.roll(z, tm - 1, 0))
    cw = conv_ref[...]
    zc = cw[0:1] * z_prev + cw[1:2] * z + cw[2:3] * z_next
    r, kr, vr = zc[:, :w], zc[:, w:2 * w], zc[:, 2 * w:]
    vec = vec_ref[...]
    k_k, k_a, r_k = vec[0:1], vec[1:2], vec[2:3]
    ones = ones_ref[...]
    kkr = kr * k_k
    kk = kkr * lax.rsqrt(_dot_split_lhs(kkr * kkr, ones, 2) + NORM_EPS)
    r_ref[0] = r.astype(BF16)
    v_ref[0] = vr.astype(BF16)
    kk_ref[0] = kk.astype(BF16)
    bonus_ref[0] = (_dot_split_lhs(r * kr * r_k, ones, 2) * vr).astype(BF16)
    lora = lora_ref[0]
    dec = _dot(jnp.tanh(lora[:, 0:LANES]).astype(BF16), w2_ref[...])
    icl = _dot(lora[:, LANES:2 * LANES].astype(BF16), a2_ref[...])
    gate_ref[0] = _dot(jax.nn.sigmoid(lora[:, 2 * LANES:3 * LANES]).astype(BF16), g2_ref[...]).astype(BF16)
    for d in range(2):
        w_log = _log_sigmoid(vec[3 + d:4 + d] + dec[:, d * w:(d + 1) * w]) - 0.5
        a = jax.nn.sigmoid(vec[5 + d:6 + d] + icl[:, d * w:(d + 1) * w])
        lw_ref[d, 0] = -jnp.exp(w_log)
        key_ref[d, 0] = (kr * (1.0 + (a - 1.0) * k_a)).astype(BF16)
        bv_ref[d, 0] = (kk * a).astype(BF16)


def _rw_prep(rkv, lora, conv_w, vecs, w2cat, a2cat, g2, ones64, t_lat):
    bsz, s, _ = rkv.shape
    tm, w = ROW_TILE, RW_WIDTH
    nsub = tm // SUBLANES
    tok = lambda c: pl.BlockSpec((1, tm, c), lambda b, i: (b, i, 0))
    dirtok = pl.BlockSpec((2, 1, tm, w), lambda b, i: (0, b, i, 0))
    last = s // SUBLANES - 1
    return pl.pallas_call(
        functools.partial(_rw_prep_kernel, t_lat=t_lat, s_tot=s),
        grid=(bsz, s // tm),
        in_specs=[tok(3 * w),
                  pl.BlockSpec((1, SUBLANES, 3 * w), lambda b, i: (b, jnp.maximum(i * nsub - 1, 0), 0)),
                  pl.BlockSpec((1, SUBLANES, 3 * w), lambda b, i: (b, jnp.minimum((i + 1) * nsub, last), 0)),
                  tok(lora.shape[-1]),
                  _resident(conv_w.shape), _resident(vecs.shape), _resident(w2cat.shape), _resident(a2cat.shape),
                  _resident(g2.shape), _resident(ones64.shape)],
        out_specs=[tok(w)] * 5 + [dirtok] * 3,
        out_shape=[jax.ShapeDtypeStruct((bsz, s, w), BF16)] * 5
        + [jax.ShapeDtypeStruct((2, bsz, s, w), dt) for dt in (F32, BF16, BF16)],
        compiler_params=_params("parallel", "parallel"),
        name="rwkv_prep",
    )(rkv, rkv, rkv, lora, conv_w, vecs, w2cat, a2cat, g2, ones64)


def _rw_scan_kernel(rf_ref, vf_ref, kkf_ref, lwf_ref, keyf_ref, bvf_ref, rb_ref, vb_ref, kkb_ref, lwb_ref, keyb_ref,
                    bvb_ref, yf_ref, yb_ref, s_ref):
    @pl.when(pl.program_id(1) == 0)
    def _():
        s_ref[...] = jnp.zeros_like(s_ref)

    C, n = RW_CHUNK, RW_HEAD_DIM
    t2 = lax.broadcasted_iota(jnp.int32, (2 * C, 2 * C), 0) % C
    s2 = lax.broadcasted_iota(jnp.int32, (2 * C, 2 * C), 1) % C
    lower = lax.broadcasted_iota(jnp.int32, (2 * C, 2 * C), 0) >= C

    chains = []
    for d, (r_ref, v_ref, kk_ref, lw_ref, key_ref, bv_ref) in enumerate(
            ((rf_ref, vf_ref, kkf_ref, lwf_ref, keyf_ref, bvf_ref), (rb_ref, vb_ref, kkb_ref, lwb_ref, keyb_ref, bvb_ref))):
        rel = (s2 - t2) if d == 0 else (t2 - s2)
        keep = rel < jnp.where(lower, 1, 0)
        tri = (rel[:C, :C] <= 0).astype(BF16)
        lw = lw_ref[0, 0]
        cum = _dot_split_rhs(tri, lw, 3)
        tot = jnp.sum(lw, axis=0, keepdims=True)
        r, v, kk, key, bv = (a.astype(F32) for a in (r_ref[0], v_ref[0], kk_ref[0], key_ref[0, 0], bv_ref[0, 0]))
        e_neg = jnp.exp(-cum)
        e_end = jnp.exp(tot - cum)
        alpha = (kk * jnp.exp(cum - lw)).astype(BF16)
        rho = (r * jnp.exp(cum)).astype(BF16)
        beta = (bv * e_neg).astype(BF16)
        kappa = (key * e_neg).astype(BF16)
        beta_end = (bv * e_end).astype(BF16)
        kappa_end = (key * e_end).astype(BF16)
        gamma = jnp.exp(tot)
        vb = v.astype(BF16)
        for h in range(RW_HEADS):
            sl = slice(h * n, (h + 1) * n)
            chains.append(dict(j=d * RW_HEADS + h, keep=keep, a=alpha[:, sl], rho=rho[:, sl], v=vb[:, sl],
                               lhs=jnp.concatenate([alpha[:, sl], rho[:, sl]], axis=0),
                               rhs=jnp.concatenate([beta[:, sl], kappa[:, sl]], axis=0),
                               end=jnp.concatenate([beta_end[:, sl], kappa_end[:, sl]], axis=0),
                               gamma=gamma[:, sl]))

    for ch in chains:
        ch["big"] = jnp.where(ch["keep"], _dot_nt(ch["lhs"], ch["rhs"]), 0.0).astype(BF16)
    for ch in chains:
        ch["lmv"] = _dot(ch["big"][:, C:], ch["v"])
    tt = lax.broadcasted_iota(jnp.int32, (C, C), 0)
    ss = lax.broadcasted_iota(jnp.int32, (C, C), 1)
    eye = (tt == ss).astype(F32)
    for ch in chains:
        ch["lab"] = ch["big"][:C, :C]
        ch["inv"] = eye - jnp.where((tt >> 1) == (ss >> 1), ch["lab"], 0.0).astype(F32)
    for k in range(1, 6):
        couples = jnp.logical_and((tt >> (k + 1)) == (ss >> (k + 1)), (tt >> k) != (ss >> k))
        for ch in chains:
            ch["ed"] = _dot(jnp.where(couples, ch["lab"], 0.0).astype(BF16), ch["inv"].astype(BF16)).astype(BF16)
        for ch in chains:
            ch["inv"] = ch["inv"] - _dot(ch["inv"].astype(BF16), ch["ed"])
    for ch in chains:
        x = -jnp.concatenate([ch["a"].astype(F32), ch["lmv"][:C]], axis=1)
        ch["x"] = _dot(ch["inv"].astype(BF16), x.astype(BF16))
    for ch in chains:
        ch["st"] = s_ref[ch["j"]]
        ch["ws"] = _dot_nt(jnp.concatenate([ch["x"][:, :n].astype(BF16), ch["rho"]], axis=0), ch["st"].astype(BF16))
    for ch in chains:
        ch["u"] = (ch["ws"][:C] + ch["x"][:, n:]).astype(BF16)
    for ch in chains:
        ch["y"] = ch["ws"][C:] + _dot(ch["big"][C:, :C], ch["u"]) + ch["lmv"][C:]
    for ch in chains:
        upd = _dot_tn(jnp.concatenate([ch["u"], ch["v"]], axis=0), ch["end"])
        s_ref[ch["j"]] = ch["st"] * ch["gamma"] + upd
    yf_ref[0] = jnp.concatenate([ch["y"] for ch in chains[:RW_HEADS]], axis=1)
    yb_ref[0] = jnp.concatenate([ch["y"] for ch in chains[RW_HEADS:]], axis=1)


def _rw_scan(r, v, kk, lw, key, bv, t_lat):
    bsz, s, w = r.shape
    C = RW_CHUNK
    orders = _chunk_order(t_lat // C, (s - t_lat) // C)
    specs = []
    for d, order in enumerate(orders):
        tok = pl.BlockSpec((1, C, w), lambda b, c, order=order: (b, order(c), 0))
        dirtok = pl.BlockSpec((1, 1, C, w), lambda b, c, order=order, d=d: (d, b, order(c), 0))
        specs += [tok, tok, tok, dirtok, dirtok, dirtok]
    return pl.pallas_call(
        _rw_scan_kernel,
        grid=(bsz, s // C),
        in_specs=specs,
        out_specs=[specs[0], specs[6]],
        out_shape=[jax.ShapeDtypeStruct((bsz, s, w), F32)] * 2,
        scratch_shapes=[pltpu.VMEM((2 * RW_HEADS, RW_HEAD_DIM, RW_HEAD_DIM), F32)],
        compiler_params=_params("parallel", "arbitrary"),
        name="rwkv_scan",
    )(r, v, kk, lw, key, bv, r, v, kk, lw, key, bv)


def _residual_mlp(mix, x, mod, g_ffn, wo_ref, w1_ref, w2_ref, out_ref):
    d = x.shape[-1]
    x1 = x + mod[:, 2 * d:3 * d] * _dot(mix, wo_ref[...])
    h2 = _norm_mod(x1, g_ffn, mod[:, 3 * d:4 * d], mod[:, 4 * d:5 * d]).astype(BF16)
    acc = jnp.zeros_like(x1)
    for k in range(w1_ref.shape[1] // d):
        hid = jnp.maximum(_dot(h2, w1_ref[:, k * d:(k + 1) * d]), 0.0)
        acc = acc + _dot((hid * hid).astype(BF16), w2_ref[k * d:(k + 1) * d, :])
    out_ref[0] = x1 + mod[:, 5 * d:6 * d] * acc


def _rec_post_kernel(hf_ref, hb_ref, o_ref, yf_ref, yb_ref, bonus_ref, gate_ref, x_ref, mod_ref, gf_ref,
                     mlg_ref, rwg_ref, avg_ml_ref, avg_rw_ref, wo_ref, w1_ref, w2_ref, out_ref):
    ml = _group_layernorm(hf_ref[0] + hb_ref[0], avg_ml_ref[...], mlg_ref[...]) * jax.nn.sigmoid(o_ref[0].astype(F32))
    rw = (_group_layernorm(yf_ref[0] + yb_ref[0], avg_rw_ref[...], rwg_ref[...]) + bonus_ref[0]) * gate_ref[0]
    mix = jnp.concatenate([ml, rw], axis=1).astype(BF16)
    _residual_mlp(mix, x_ref[0], mod_ref[0], gf_ref[...], wo_ref, w1_ref, w2_ref, out_ref)


def _att_post_kernel(ow_ref, og_ref, x_ref, mod_ref, gf_ref, wo_ref, w1_ref, w2_ref, out_ref):
    mix = jnp.concatenate([ow_ref[0], og_ref[0]], axis=1)
    _residual_mlp(mix, x_ref[0], mod_ref[0], gf_ref[...], wo_ref, w1_ref, w2_ref, out_ref)


def _post_call(kernel, name, token_inputs, xs, mods, consts, n_rows, n_lat_tiles):
    bsz, _, d = xs.shape
    tm = ROW_TILE
    mod_row = lambda b, i: (jnp.where(i >= n_lat_tiles, bsz, b), 0, 0)
    tok = lambda c: pl.BlockSpec((1, tm, c), lambda b, i: (b, i, 0))
    in_specs = [tok(arr.shape[-1]) for arr in token_inputs]
    in_specs += [tok(d), pl.BlockSpec((1, 1, mods.shape[-1]), mod_row)]
    in_specs += [_resident(c.shape) for c in consts]
    args = list(token_inputs)
    return pl.pallas_call(
        kernel,
        grid=(bsz, n_rows // tm),
        in_specs=in_specs,
        out_specs=pl.BlockSpec((1, tm, d), lambda b, i: (b, i, 0)),
        out_shape=jax.ShapeDtypeStruct((bsz, n_rows, d), F32),
        compiler_params=_params("parallel", "parallel"),
        name=name,
    )(*args, xs, mods, *consts)


ATT_QW, ATT_QG, ATT_KW, ATT_KG, ATT_VW, ATT_VG, ATT_END = 0, 512, 1024, 1152, 1280, 1408, 1536
ATT_QK_END = ATT_VW


def _att_inproj_kernel(x_ref, mod_ref, g_ref, w_ref, qkg_ref, cos_ref, sin_ref, sq_ref,
                       qw_ref, qg_ref, kw_ref, kg_ref, vw_ref, vg0_ref, vg1_ref):
    d = x_ref.shape[-1]
    mod = mod_ref[0]
    h = _norm_mod(x_ref[0], g_ref[...], mod[:, 0:d], mod[:, d:2 * d]).astype(BF16)
    half = ATT_HEAD_DIM // 2
    cos2, sin2 = cos_ref[0], sin_ref[0]

    def norm_rope(lo, hi):
        z = _dot(h, w_ref[:, lo:hi])
        n = hi - lo
        sq = sq_ref[...]
        z = z * lax.rsqrt(_dot((z * z).astype(BF16), sq[:n, :n]) + NORM_EPS) * qkg_ref[:, lo:hi]
        reps = n // LANES
        cos = jnp.concatenate([cos2] * reps, axis=1)
        sin = jnp.concatenate([sin2] * reps, axis=1)
        lane = lax.broadcasted_iota(jnp.int32, z.shape, 1)
        partner = jnp.where(lane % ATT_HEAD_DIM < half, pltpu.roll(z, n - half, 1), pltpu.roll(z, half, 1))
        return (z * cos + partner * sin).astype(BF16)

    qw_ref[0] = norm_rope(ATT_QW, ATT_QG)
    qg_ref[0] = norm_rope(ATT_QG, ATT_KW)
    kk = norm_rope(ATT_KW, ATT_VW)
    kw_ref[0] = kk[:, :KV_WIDTH]
    kg_ref[0] = kk[:, KV_WIDTH:]
    vv = _dot(h, w_ref[:, ATT_VW:ATT_END])
    vw_ref[0] = vv[:, :KV_WIDTH].astype(BF16)
    vg = vv[:, KV_WIDTH:]
    lane = lax.broadcasted_iota(jnp.int32, vg.shape, 1)
    vg0_ref[0] = jnp.where(lane < ATT_HEAD_DIM, vg, jnp.where(lane == ATT_HEAD_DIM, 1.0, 0.0)).astype(BF16)
    vg1_ref[0] = jnp.where(lane >= ATT_HEAD_DIM, vg, jnp.where(lane == 0, 1.0, 0.0)).astype(BF16)


def _att_inproj(xs, mods, gain, w, qk_gain, cos, sin, sq_avg, n_lat_tiles):
    bsz, s, d = xs.shape
    tm = ROW_TILE
    tok = lambda c: pl.BlockSpec((1, tm, c), lambda b, i: (b, i, 0))
    mod_row = lambda b, i: (jnp.where(i >= n_lat_tiles, bsz, b), 0, 0)
    rope = pl.BlockSpec((1, tm, LANES), lambda b, i: (0, i, 0))
    qw = Q_HEADS * ATT_HEAD_DIM
    return pl.pallas_call(
        _att_inproj_kernel,
        grid=(bsz, s // tm),
        in_specs=[tok(d), pl.BlockSpec((1, 1, mods.shape[-1]), mod_row), _resident((1, d)), _resident(w.shape),
                  _resident(qk_gain.shape), rope, rope, _resident(sq_avg.shape)],
        out_specs=[tok(qw), tok(qw)] + [tok(KV_WIDTH)] * 5,
        out_shape=[jax.ShapeDtypeStruct((bsz, s, c), BF16) for c in (qw, qw) + (KV_WIDTH,) * 5],
        compiler_params=_params("parallel", "parallel"),
        name="att_inproj",
    )(xs, mods, gain, w, qk_gain, cos, sin, sq_avg)


def _stack_queries(q):
    lane = lax.broadcasted_iota(jnp.int32, (q.shape[0], LANES), 1)
    zero = jnp.zeros((q.shape[0], LANES), q.dtype)
    slabs = [q[:, g * LANES:(g + 1) * LANES] for g in range(KV_GROUP)]
    lo = [jnp.where(lane < ATT_HEAD_DIM, sl, zero) for sl in slabs]
    hi = [jnp.where(lane >= ATT_HEAD_DIM, sl, zero) for sl in slabs]
    return jnp.concatenate(lo + hi, axis=0)


def _unstack_outputs(o, rows):
    lane = lax.broadcasted_iota(jnp.int32, (rows, LANES), 1)
    slabs = [jnp.where(lane < ATT_HEAD_DIM, o[g * rows:(g + 1) * rows], o[(KV_GROUP + g) * rows:(KV_GROUP + g + 1) * rows])
             for g in range(KV_GROUP)]
    return jnp.concatenate(slabs, axis=1)


def _attn_kernel(qw_ref, kp_ref, ko_ref, kn_ref, kc_ref, vp_ref, vo_ref, vn_ref, vc_ref, sink_ref,
                 qg_ref, k_ref, v0_ref, v1_ref, ow_ref, og_ref, *, tk):
    L = Q_BLOCK
    i, nb = pl.program_id(1), pl.num_programs(1)
    lc = kc_ref.shape[1]
    qs_w = _stack_queries(qw_ref[0])
    keys = jnp.concatenate([kp_ref[0], ko_ref[0], kn_ref[0], kc_ref[0]], axis=0)
    vals = jnp.concatenate([vp_ref[0], vo_ref[0], vn_ref[0], vc_ref[0]], axis=0)
    s_w = _dot_nt(qs_w, keys)

    qs = _stack_queries(qg_ref[0])
    rows = qs.shape[0]
    half = rows // 2
    n_tiles = k_ref.shape[1] // tk
    scores = lambda j: _dot_nt(qs, k_ref[0, j * tk:(j + 1) * tk, :])
    s_next = scores(0)

    nk = 3 * L + lc
    row = lax.broadcasted_iota(jnp.int32, (L, nk), 0)
    col = lax.broadcasted_iota(jnp.int32, (L, nk), 1)
    off = col - L - row
    valid = jnp.logical_and(off <= WINDOW, off >= -WINDOW)
    valid = jnp.logical_and(valid, col >= jnp.where(i > 0, 0, L))
    valid = jnp.logical_and(valid, col < jnp.where(i < nb - 1, 3 * L, 2 * L))
    valid = jnp.logical_or(valid, col >= 3 * L)
    valid = jnp.concatenate([valid] * (2 * KV_GROUP), axis=0)
    s_w = jnp.where(valid, s_w, MASK_VALUE)
    sink = sink_ref[...]
    m_w = jnp.maximum(jnp.max(s_w, axis=1, keepdims=True), sink)
    p_w = jnp.exp(s_w - m_w)
    den_w = jnp.sum(p_w, axis=1, keepdims=True) + jnp.exp(sink - m_w)
    ow_ref[0] = _unstack_outputs(_dot(p_w.astype(BF16), vals) / den_w, L).astype(BF16)

    m = jnp.full((rows, 1), MASK_VALUE, F32)
    acc = jnp.zeros((rows, LANES), F32)
    for j in range(n_tiles):
        s = s_next
        if j + 1 < n_tiles:
            s_next = scores(j + 1)
        m_new = jnp.maximum(m, jnp.max(s, axis=1, keepdims=True))
        p = jnp.exp2(s - m_new).astype(BF16)
        pv = jnp.concatenate([_dot(p[:half], v0_ref[0, j * tk:(j + 1) * tk, :]),
                              _dot(p[half:], v1_ref[0, j * tk:(j + 1) * tk, :])], axis=0)
        acc = jnp.exp2(m - m_new) * acc + pv
        m = m_new
    den = jnp.concatenate([acc[:half, ATT_HEAD_DIM:ATT_HEAD_DIM + 1], acc[half:, 0:1]], axis=0)
    og_ref[0] = _unstack_outputs(acc / den, L).astype(BF16)


GLB_KEY_TILES = (1408, 768, 512, 384, 256, 128)


def _attention(qw, kw, vw, sink_col, qg, kg, vg0, vg1, t_lat):
    bsz, s, width = qw.shape
    L = Q_BLOCK
    nb = t_lat // L
    lc = s - t_lat
    tk = next(t for t in GLB_KEY_TILES if s % t == 0)
    qblk = pl.BlockSpec((1, L, width), lambda b, i: (b, i, 0))
    kv = lambda f: pl.BlockSpec((1, L, KV_WIDTH), lambda b, i: (b, f(i), 0))
    ctx = pl.BlockSpec((1, lc, KV_WIDTH), lambda b, i: (b, t_lat // lc, 0))
    nbr = [kv(lambda i: jnp.maximum(i - 1, 0)), kv(lambda i: i), kv(lambda i: jnp.minimum(i + 1, nb - 1)), ctx]
    whole = pl.BlockSpec((1, s, KV_WIDTH), lambda b, i: (b, 0, 0))
    return pl.pallas_call(
        functools.partial(_attn_kernel, tk=tk),
        grid=(bsz, nb),
        in_specs=[qblk] + nbr + nbr + [_resident(sink_col.shape), qblk, whole, whole, whole],
        out_specs=[qblk, qblk],
        out_shape=[jax.ShapeDtypeStruct((bsz, t_lat, width), BF16)] * 2,
        compiler_params=_params("parallel", "parallel"),
        name="attention",
    )(qw, kw, kw, kw, kw, vw, vw, vw, vw, sink_col, qg, kg, vg0, vg1)


def _block_diag_const(width, group, value):
    idx = np.arange(width) // group
    return jnp.asarray((idx[:, None] == idx[None, :]) * value, dtype=BF16)


def _head_perm():
    return np.concatenate([np.arange(0, ATT_HEAD_DIM, 2), np.arange(1, ATT_HEAD_DIM, 2)])


def _q_head_order():
    return [kv * KV_GROUP + g for g in range(KV_GROUP) for kv in range(KV_HEADS)]


def _rec_weights(w, gate_b, conv_w, w0, w2, a0, a2, k_k, k_a, r_k):
    o = np.cumsum((0,) + (ML_WIDTH,) * 4 + (ML_HEADS,) * 4 + (RW_WIDTH,) * 3
                  + (RW_DECAY_RANK,) * 2 + (RW_ICLR_RANK,) * 2 + (RW_GATE_RANK,))
    q, k, v, og = (w[:, o[i]:o[i + 1]] for i in range(4))
    gates = w[:, o[4]:o[8]]
    rkv = w[:, o[8]:o[11]]
    lora = w[:, o[11]:o[16]]
    gates_pad = jnp.pad(gates, ((0, 0), (0, LANES - N_GATES)))
    w_main = jnp.concatenate([q, v, og, rkv, lora, gates_pad], axis=1).astype(BF16)
    gb = gate_b.reshape(-1)
    gb_row = jnp.pad(gb, (0, LANES - N_GATES)).reshape(1, LANES)
    zeros = jnp.zeros((RW_DECAY_RANK, RW_WIDTH), F32)
    w2cat = jnp.concatenate([jnp.concatenate([w2[0], zeros], 1), jnp.concatenate([zeros, w2[1]], 1)], 0).astype(BF16)
    a2cat = jnp.concatenate([jnp.concatenate([a2[0], zeros], 1), jnp.concatenate([zeros, a2[1]], 1)], 0).astype(BF16)
    vecs = jnp.stack([k_k, k_a, r_k, w0[0], w0[1], a0[0], a0[1], jnp.zeros_like(k_k)])
    return dict(w_main=w_main, wkt=k.T.astype(BF16), wgt=gates.T.astype(BF16), gb_row=gb_row,
                gb_col=gb.reshape(N_GATES, 1), conv=conv_w, vecs=vecs, w2cat=w2cat, a2cat=a2cat)


def _att_weights(w, win_q_norm, win_k_norm, glb_q_norm, glb_k_norm, w_out):
    hd, perm = ATT_HEAD_DIM, _head_perm()
    qn, kn = Q_HEADS * hd, KV_HEADS * hd
    o = np.cumsum((0, qn, kn, kn, qn, kn, kn))
    qw, kw, vw, qg, kg, vg = (w[:, o[i]:o[i + 1]] for i in range(6))
    q_cols = np.concatenate([h * hd + perm for h in _q_head_order()])
    k_cols = np.concatenate([h * hd + perm for h in range(KV_HEADS)])
    w_all = jnp.concatenate([qw[:, q_cols], qg[:, q_cols], kw[:, k_cols], kg[:, k_cols], vw, vg], axis=1).astype(BF16)
    scale = hd ** -0.5
    log2e = float(np.log2(np.e))
    gains = jnp.concatenate([jnp.tile(win_q_norm[perm], Q_HEADS) * scale,
                             jnp.tile(glb_q_norm[perm], Q_HEADS) * (scale * log2e),
                             jnp.tile(win_k_norm[perm], KV_HEADS), jnp.tile(glb_k_norm[perm], KV_HEADS)]).reshape(1, -1)
    out_rows = np.concatenate([h * hd + np.arange(hd) for h in _q_head_order()])
    wo = jnp.concatenate([w_out[:qn][out_rows], w_out[qn:][out_rows]], axis=0).astype(BF16)
    return w_all, gains, wo


def _rope_tables(t_lat, s):
    n_freq = ATT_HEAD_DIM // 4
    pos = jnp.arange(t_lat)
    inv_freq = ROPE_THETA ** (-jnp.arange(n_freq, dtype=F32) / n_freq)
    ang = jnp.concatenate([(pos // GRID_W).astype(F32)[:, None] * inv_freq,
                           (pos % GRID_W).astype(F32)[:, None] * inv_freq], axis=-1)
    cos, sin = jnp.cos(ang), jnp.sin(ang)
    cos_h = jnp.concatenate([cos, cos], axis=-1)
    sin_h = jnp.concatenate([-sin, sin], axis=-1)
    pad = ((0, s - t_lat), (0, 0))
    cos_h = jnp.pad(cos_h, pad, constant_values=1.0)
    sin_h = jnp.pad(sin_h, pad)
    return jnp.tile(cos_h, (1, 2))[None], jnp.tile(sin_h, (1, 2))[None]


def kernel(x, c, ctx, c_ctx, norm_mix, norm_ffn, mod_w, mod_b, out_w, ffn_w1, ffn_w2, rec_in_w, ml_gate_b, ml_norm_g,
           rw_conv, rw_w0, rw_w2, rw_a0, rw_a2, rw_g2, rw_kk, rw_ka, rw_rk, rw_norm_g, att_in_w, win_q_norm,
           win_k_norm, win_sink, glb_q_norm, glb_k_norm):
    bsz, t_lat, d = x.shape
    lc = ctx.shape[1]
    s = t_lat + lc
    depth = mod_w.shape[0]
    assert depth == 2 and bsz + 1 <= SUBLANES
    assert t_lat % ROW_TILE == 0 and lc % ROW_TILE == 0 and t_lat % lc == 0 and t_lat % GRID_W == 0
    n_lat_tiles = t_lat // ROW_TILE

    cvec = jnp.zeros((SUBLANES, d), F32).at[:bsz].set(c).at[bsz].set(c_ctx)
    mods = _modulation(cvec, mod_w, mod_b)[:, :bsz + 1].reshape(depth, bsz + 1, 1, 6 * d)
    xs = jnp.concatenate([x, ctx], axis=1)
    w1 = ffn_w1.astype(BF16)
    w2 = ffn_w2.astype(BF16)

    rw = _rec_weights(rec_in_w[0], ml_gate_b[0], rw_conv[0], rw_w0[0], rw_w2[0], rw_a0[0], rw_a2[0],
                      rw_kk[0], rw_ka[0], rw_rk[0])
    q, v, o, rkv, lora, gates, kt, gates_t = _rec_inproj(
        xs, mods[0], norm_mix[0].reshape(1, d), rw["w_main"], rw["wkt"], rw["wgt"], rw["gb_row"], rw["gb_col"],
        n_lat_tiles)
    h_f, h_b = _mlstm(q, v, kt, gates, gates_t, t_lat)
    ones64 = _block_diag_const(RW_WIDTH, RW_HEAD_DIM, 1.0)
    r, vr, kk, bonus, gate, lw, key, bv = _rw_prep(rkv, lora, rw["conv"], rw["vecs"], rw["w2cat"], rw["a2cat"],
                                                   rw_g2[0].astype(BF16), ones64, t_lat)
    y_f, y_b = _rw_scan(r, vr, kk, lw, key, bv, t_lat)
    consts = [norm_ffn[0].reshape(1, d), ml_norm_g[0].reshape(1, -1), rw_norm_g[0].reshape(1, -1),
              _block_diag_const(ML_WIDTH, ML_HEAD_DIM, 1.0 / ML_HEAD_DIM),
              _block_diag_const(RW_WIDTH, RW_HEAD_DIM, 1.0 / RW_HEAD_DIM),
              out_w[0].astype(BF16), w1[0], w2[0]]
    tokens = [h_f, h_b, o, y_f, y_b, bonus, gate]
    xs = _post_call(_rec_post_kernel, "rec_post", tokens, xs, mods[0], consts, s, n_lat_tiles)

    w_att, qk_gain, wo = _att_weights(att_in_w[0], win_q_norm[0], win_k_norm[0], glb_q_norm[0], glb_k_norm[0], out_w[1])
    cos, sin = _rope_tables(t_lat, s)
    sq_avg = _block_diag_const(Q_HEADS * ATT_HEAD_DIM, ATT_HEAD_DIM, 1.0 / ATT_HEAD_DIM)
    qw, qg, kw, kg, vw, vg0, vg1 = _att_inproj(xs, mods[1], norm_mix[1].reshape(1, d), w_att, qk_gain, cos, sin,
                                               sq_avg, n_lat_tiles)
    sink_col = jnp.repeat(win_sink[0][np.array([kv * KV_GROUP + g for kv in range(KV_HEADS) for g in range(KV_GROUP)])],
                          Q_BLOCK).reshape(-1, 1)
    o_win, o_glb = _attention(qw, kw, vw, sink_col, qg, kg, vg0, vg1, t_lat)
    consts = [norm_ffn[1].reshape(1, d), wo, w1[1], w2[1]]
    return _post_call(_att_post_kernel, "att_post", [o_win, o_glb], xs, mods[1], consts, t_lat,
                      n_lat_tiles)
```

```python
import functools

import numpy as np
import jax
import jax.numpy as jnp
from jax import lax
from jax.experimental import pallas as pl
from jax.experimental.pallas import tpu as pltpu

F32 = jnp.float32
BF16 = jnp.bfloat16

GRID_W = 64
NORM_EPS = 1e-6
GN_EPS = 64e-5
ROPE_THETA = 10000.0
ML_HEAD_DIM = 128
ML_HEADS = 4
ML_WIDTH = ML_HEADS * ML_HEAD_DIM
ML_CHUNK = 128
RW_HEAD_DIM = 64
RW_HEADS = 8
RW_WIDTH = RW_HEADS * RW_HEAD_DIM
RW_CHUNK = 64
RW_DECAY_RANK = 64
RW_ICLR_RANK = 64
RW_GATE_RANK = 128
ATT_HEAD_DIM = 64
Q_HEADS = 8
KV_HEADS = 2
KV_GROUP = Q_HEADS // KV_HEADS
KV_WIDTH = KV_HEADS * ATT_HEAD_DIM
WINDOW = 128
Q_BLOCK = 128

LANES = 128
SUBLANES = 8
VMEM_LIMIT_BYTES = 56 * 1024 * 1024
ROW_TILE = 256
MASK_VALUE = -1e30


def _params(*semantics):
    return pltpu.CompilerParams(dimension_semantics=semantics, vmem_limit_bytes=VMEM_LIMIT_BYTES)


def _resident(shape):
    zeros = (0,) * len(shape)
    return pl.BlockSpec(shape, lambda *_: zeros, pipeline_mode=pl.Buffered(1))


def _dot(a, b):
    return jnp.dot(a, b, preferred_element_type=F32)


def _dot_nt(a, b):
    return lax.dot_general(a, b, (((1,), (1,)), ((), ())), preferred_element_type=F32)


def _dot_tn(a, b):
    return lax.dot_general(a, b, (((0,), (0,)), ((), ())), preferred_element_type=F32)


def _split(x, n):
    parts, rest = [], x
    for _ in range(n):
        p = rest.astype(BF16)
        parts.append(p)
        rest = rest - p.astype(F32)
    return parts


def _dot_split_lhs(x, m, n):
    return sum(_dot(p, m) for p in _split(x, n))


def _dot_split_rhs(m, x, n):
    return sum(_dot(m, p) for p in _split(x, n))


def _dot_f32(a, b):
    ah, al = _split(a, 2)
    bh, bl = _split(b, 2)
    return _dot(ah, bh) + _dot(ah, bl) + _dot(al, bh)


def _log_sigmoid(x):
    return jnp.minimum(x, 0.0) - jnp.log1p(jnp.exp(-jnp.abs(x)))


def _norm_mod(x, gain, shift, scale):
    y = x * lax.rsqrt(jnp.mean(x * x, axis=-1, keepdims=True) + NORM_EPS)
    return (y * gain) * (1.0 + scale) + shift


def _group_layernorm(y, avg, gain):
    yc = y - _dot_split_lhs(y, avg, 2)
    var = _dot_split_lhs(yc * yc, avg, 2)
    return yc * lax.rsqrt(var + GN_EPS) * gain


def _mod_kernel(cv_ref, w_ref, b_ref, o_ref):
    cv = cv_ref[...]
    o_ref[0] = _dot_f32(cv * jax.nn.sigmoid(cv), w_ref[0]) + b_ref[0]


def _modulation(cvec, mod_w, mod_b):
    depth, d, n = mod_w.shape
    tn = n // 4
    return pl.pallas_call(
        _mod_kernel,
        grid=(depth, n // tn),
        in_specs=[pl.BlockSpec((SUBLANES, d), lambda l, j: (0, 0)),
                  pl.BlockSpec((1, d, tn), lambda l, j: (l, 0, j)),
                  pl.BlockSpec((1, 1, tn), lambda l, j: (l, 0, j))],
        out_specs=pl.BlockSpec((1, SUBLANES, tn), lambda l, j: (l, 0, j)),
        out_shape=jax.ShapeDtypeStruct((depth, SUBLANES, n), F32),
        compiler_params=_params("parallel", "parallel"),
        name="adaln_modulation",
    )(cvec, mod_w, mod_b.reshape(depth, 1, n))


REC_Q, REC_V, REC_O, REC_RKV, REC_LORA, REC_GATE, REC_END = 0, 512, 1024, 1536, 3072, 3456, 3584
N_GATES = 4 * ML_HEADS


def _rec_inproj_kernel(x_ref, mod_ref, g_ref, w_ref, wkt_ref, wgt_ref, gb_ref, gbt_ref,
                       q_ref, v_ref, o_ref, rkv_ref, lora_ref, gate_ref, kt_ref, gatet_ref):
    d = x_ref.shape[-1]
    mod = mod_ref[0]
    h = _norm_mod(x_ref[0], g_ref[...], mod[:, 0:d], mod[:, d:2 * d]).astype(BF16)
    q_ref[0] = _dot(h, w_ref[:, REC_Q:REC_V]).astype(BF16)
    v_ref[0] = _dot(h, w_ref[:, REC_V:REC_O]).astype(BF16)
    o_ref[0] = _dot(h, w_ref[:, REC_O:REC_RKV]).astype(BF16)
    rkv_ref[0] = _dot(h, w_ref[:, REC_RKV:REC_LORA])
    lora_ref[0] = _dot(h, w_ref[:, REC_LORA:REC_GATE])
    gate_ref[0] = _dot(h, w_ref[:, REC_GATE:REC_END]) + gb_ref[...]
    kt_ref[0] = _dot_nt(wkt_ref[...], h).astype(BF16)
    gatet_ref[0] = _dot_nt(wgt_ref[...], h) + gbt_ref[...]


def _rec_inproj(xs, mods, gain, w_main, wkt, wgt, gb_row, gb_col, n_lat_tiles):
    bsz, s, d = xs.shape
    tm = ROW_TILE
    tok = lambda c: pl.BlockSpec((1, tm, c), lambda b, i: (b, i, 0))
    mod_row = lambda b, i: (jnp.where(i >= n_lat_tiles, bsz, b), 0, 0)
    return pl.pallas_call(
        _rec_inproj_kernel,
        grid=(bsz, s // tm),
        in_specs=[tok(d),
                  pl.BlockSpec((1, 1, mods.shape[-1]), mod_row),
                  _resident((1, d)), _resident(w_main.shape), _resident(wkt.shape), _resident(wgt.shape),
                  _resident(gb_row.shape), _resident(gb_col.shape)],
        out_specs=[tok(ML_WIDTH), tok(ML_WIDTH), tok(ML_WIDTH), tok(3 * RW_WIDTH), tok(REC_GATE - REC_LORA),
                   tok(LANES),
                   pl.BlockSpec((1, ML_WIDTH, tm), lambda b, i: (b, 0, i)),
                   pl.BlockSpec((1, N_GATES, tm), lambda b, i: (b, 0, i))],
        out_shape=[jax.ShapeDtypeStruct((bsz, s, ML_WIDTH), BF16),
                   jax.ShapeDtypeStruct((bsz, s, ML_WIDTH), BF16),
                   jax.ShapeDtypeStruct((bsz, s, ML_WIDTH), BF16),
                   jax.ShapeDtypeStruct((bsz, s, 3 * RW_WIDTH), F32),
                   jax.ShapeDtypeStruct((bsz, s, REC_GATE - REC_LORA), F32),
                   jax.ShapeDtypeStruct((bsz, s, LANES), F32),
                   jax.ShapeDtypeStruct((bsz, ML_WIDTH, s), BF16),
                   jax.ShapeDtypeStruct((bsz, N_GATES, s), F32)],
        compiler_params=_params("parallel", "parallel"),
        name="rec_inproj",
    )(xs, mods, gain, w_main, wkt, wgt, gb_row, gb_col)


def _mlstm_stages(qf_ref, vf_ref, ktf_ref, gf_ref, gtf_ref, qb_ref, vb_ref, ktb_ref, gb_ref, gtb_ref,
                  hf_ref, hb_ref, c_ref, m_ref):
    L, dh = ML_CHUNK, ML_HEAD_DIM
    t_idx = lax.broadcasted_iota(jnp.int32, (L, L), 0)
    s_idx = lax.broadcasted_iota(jnp.int32, (L, L), 1)
    ones_col = (lax.broadcasted_iota(jnp.int32, (L, LANES), 1) == 0).astype(BF16)
    scale = dh ** -0.5
    streams = ((qf_ref, vf_ref, ktf_ref, gf_ref, gtf_ref, hf_ref),
               (qb_ref, vb_ref, ktb_ref, gb_ref, gtb_ref, hb_ref))
    chains = []
    for d, (q_ref, v_ref, kt_ref, g_ref, gt_ref, h_ref) in enumerate(streams):
        earlier = (s_idx <= t_idx) if d == 0 else (s_idx >= t_idx)
        tri = earlier.astype(BF16)
        tri_t = ((t_idx <= s_idx) if d == 0 else (t_idx >= s_idx)).astype(BF16)
        gt = gt_ref[0]
        lf_rows = _log_sigmoid(gt)
        b_cols = _dot_split_rhs(tri, _log_sigmoid(g_ref[0]), 3)
        b_rows = _dot_split_lhs(lf_rows, tri_t, 3)
        tot = jnp.sum(lf_rows, axis=1, keepdims=True)
        q, v, kt = q_ref[0], v_ref[0], kt_ref[0]
        for hh in range(ML_HEADS):
            ic, fc = 2 * ML_HEADS * d + hh, 2 * ML_HEADS * d + ML_HEADS + hh
            chains.append(dict(j=d * ML_HEADS + hh, hh=hh, h_ref=h_ref, earlier=earlier,
                               b_col=b_cols[:, fc:fc + 1], b_row=b_rows[fc:fc + 1, :], i_row=gt[ic:ic + 1, :],
                               b_end=tot[fc:fc + 1, :], q=q[:, hh * dh:(hh + 1) * dh], kt=kt[hh * dh:(hh + 1) * dh, :],
                               v_aug=jnp.concatenate([v[:, hh * dh:(hh + 1) * dh], ones_col], axis=1)))
    yield
    for ch in chains:
        ch["cst"] = c_ref[ch["j"]]
        ch["qk"] = _dot(ch["q"], ch["kt"])
        ch["qc"] = _dot(ch["q"], ch["cst"].astype(BF16))
    yield
    for ch in chains:
        m_prev = m_ref[ch["j"]:ch["j"] + 1, 0:1]
        d_in = jnp.where(ch["earlier"], ch["b_col"] - ch["b_row"] + ch["i_row"], MASK_VALUE)
        d_prev = ch["b_col"] + m_prev
        m_t = jnp.maximum(d_prev, jnp.max(d_in, axis=1, keepdims=True))
        ch["s"] = (ch["qk"] * (scale * jnp.exp(d_in - m_t))).astype(BF16)
        ch["w_prev"], ch["floor"] = jnp.exp(d_prev - m_t), jnp.exp(-m_t)
        d_end = ch["b_end"] - ch["b_row"] + ch["i_row"]
        m_new = jnp.maximum(ch["b_end"] + m_prev, jnp.max(d_end, axis=1, keepdims=True))
        ch["kw"] = (ch["kt"].astype(F32) * (jnp.exp(d_end - m_new) * scale)).astype(BF16)
        ch["decay"] = jnp.exp(ch["b_end"] + m_prev - m_new)
        m_ref[ch["j"]:ch["j"] + 1, :] = jnp.broadcast_to(m_new, (1, LANES))
    yield
    for ch in chains:
        acc = _dot(ch["s"], ch["v_aug"]) + ch["w_prev"] * ch["qc"]
        den = jnp.maximum(jnp.abs(acc[:, dh:dh + 1]), ch["floor"])
        ch["h_ref"][0, :, ch["hh"] * dh:(ch["hh"] + 1) * dh] = acc[:, :dh] / den
    yield
    for ch in chains:
        c_ref[ch["j"]] = ch["decay"] * ch["cst"] + _dot(ch["kw"], ch["v_aug"])


def _chunk_order(n_lat, n_ctx):
    n = n_lat + n_ctx
    fwd = lambda c: (c + n_lat) % n
    bwd = lambda c: n - 1 - c
    return fwd, bwd


def _rw_prep_kernel(z_ref, zp_ref, zn_ref, lora_ref, conv_ref, vec_ref, w2_ref, a2_ref, g2_ref, ones_ref,
                    r_ref, v_ref, kk_ref, bonus_ref, gate_ref, lw_ref, key_ref, bv_ref, *, t_lat, s_tot):
    tm, w = z_ref.shape[1], RW_WIDTH
    row0 = pl.program_id(1) * tm
    has_prev = jnp.where(jnp.logical_and(row0 != 0, row0 != t_lat), 1.0, 0.0)
    has_next = jnp.where(jnp.logical_and(row0 + tm != t_lat, row0 + tm != s_tot), 1.0, 0.0)
    z = z_ref[0]
    prev_row = zp_ref[0][SUBLANES - 1:SUBLANES, :] * has_prev
    next_row = zn_ref[0][0:1, :] * has_next
    ridx = lax.broadcasted_iota(jnp.int32, z.shape, 0)
    z_prev = jnp.where(ridx == 0, prev_row, pltpu.roll(z, 1, 0))
    z_next = jnp.where(ridx == tm - 1, next_row, pltpu.roll(z, tm - 1, 0))
    cw = conv_ref[...]
    zc = cw[0:1] * z_prev + cw[1:2] * z + cw[2:3] * z_next
    r, kr, vr = zc[:, :w], zc[:, w:2 * w], zc[:, 2 * w:]
    vec = vec_ref[...]
    k_k, k_a, r_k = vec[0:1], vec[1:2], vec[2:3]
    ones = ones_ref[...]
    kkr = kr * k_k
    kk = kkr * lax.rsqrt(_dot_split_lhs(kkr * kkr, ones, 2) + NORM_EPS)
    r_ref[0] = r.astype(BF16)
    v_ref[0] = vr.astype(BF16)
    kk_ref[0] = kk.astype(BF16)
    bonus_ref[0] = (_dot_split_lhs(r * kr * r_k, ones, 2) * vr).astype(BF16)
    lora = lora_ref[0]
    dec = _dot(jnp.tanh(lora[:, 0:LANES]).astype(BF16), w2_ref[...])
    icl = _dot(lora[:, LANES:2 * LANES].astype(BF16), a2_ref[...])
    gate_ref[0] = _dot(jax.nn.sigmoid(lora[:, 2 * LANES:3 * LANES]).astype(BF16), g2_ref[...]).astype(BF16)
    for d in range(2):
        w_log = _log_sigmoid(vec[3 + d:4 + d] + dec[:, d * w:(d + 1) * w]) - 0.5
        a = jax.nn.sigmoid(vec[5 + d:6 + d] + icl[:, d * w:(d + 1) * w])
        lw_ref[d, 0] = -jnp.exp(w_log)
        key_ref[d, 0] = (kr * (1.0 + (a - 1.0) * k_a)).astype(BF16)
        bv_ref[d, 0] = (kk * a).astype(BF16)


def _rw_prep(rkv, lora, conv_w, vecs, w2cat, a2cat, g2, ones64, t_lat):
    bsz, s, _ = rkv.shape
    tm, w = ROW_TILE, RW_WIDTH
    nsub = tm // SUBLANES
    tok = lambda c: pl.BlockSpec((1, tm, c), lambda b, i: (b, i, 0))
    dirtok = pl.BlockSpec((2, 1, tm, w), lambda b, i: (0, b, i, 0))
    last = s // SUBLANES - 1
    return pl.pallas_call(
        functools.partial(_rw_prep_kernel, t_lat=t_lat, s_tot=s),
        grid=(bsz, s // tm),
        in_specs=[tok(3 * w),
                  pl.BlockSpec((1, SUBLANES, 3 * w), lambda b, i: (b, jnp.maximum(i * nsub - 1, 0), 0)),
                  pl.BlockSpec((1, SUBLANES, 3 * w), lambda b, i: (b, jnp.minimum((i + 1) * nsub, last), 0)),
                  tok(lora.shape[-1]),
                  _resident(conv_w.shape), _resident(vecs.shape), _resident(w2cat.shape), _resident(a2cat.shape),
                  _resident(g2.shape), _resident(ones64.shape)],
        out_specs=[tok(w)] * 5 + [dirtok] * 3,
        out_shape=[jax.ShapeDtypeStruct((bsz, s, w), BF16)] * 5
        + [jax.ShapeDtypeStruct((2, bsz, s, w), dt) for dt in (F32, BF16, BF16)],
        compiler_params=_params("parallel", "parallel"),
        name="rwkv_prep",
    )(rkv, rkv, rkv, lora, conv_w, vecs, w2cat, a2cat, g2, ones64)


def _rw_scan_stages(rf_ref, vf_ref, kkf_ref, lwf_ref, keyf_ref, bvf_ref, rb_ref, vb_ref, kkb_ref, lwb_ref, keyb_ref,
                    bvb_ref, yf_ref, yb_ref, s_ref, step):
    C, n = RW_CHUNK, RW_HEAD_DIM
    t2 = lax.broadcasted_iota(jnp.int32, (2 * C, 2 * C), 0) % C
    s2 = lax.broadcasted_iota(jnp.int32, (2 * C, 2 * C), 1) % C
    lower = lax.broadcasted_iota(jnp.int32, (2 * C, 2 * C), 0) >= C

    chains = []
    rows_of = (slice(step * C, (step + 1) * C), slice((1 - step) * C, (2 - step) * C))
    for d, (r_ref, v_ref, kk_ref, lw_ref, key_ref, bv_ref) in enumerate(
            ((rf_ref, vf_ref, kkf_ref, lwf_ref, keyf_ref, bvf_ref), (rb_ref, vb_ref, kkb_ref, lwb_ref, keyb_ref, bvb_ref))):
        rows = rows_of[d]
        rel = (s2 - t2) if d == 0 else (t2 - s2)
        keep = rel < jnp.where(lower, 1, 0)
        tri = (rel[:C, :C] <= 0).astype(BF16)
        lw = lw_ref[0, 0, rows, :]
        cum = _dot_split_rhs(tri, lw, 3)
        tot = jnp.sum(lw, axis=0, keepdims=True)
        r, v, kk, key, bv = (a.astype(F32) for a in (r_ref[0, rows, :], v_ref[0, rows, :], kk_ref[0, rows, :],
                                                     key_ref[0, 0, rows, :], bv_ref[0, 0, rows, :]))
        e_neg = jnp.exp(-cum)
        e_end = jnp.exp(tot - cum)
        alpha = (kk * jnp.exp(cum - lw)).astype(BF16)
        rho = (r * jnp.exp(cum)).astype(BF16)
        beta = (bv * e_neg).astype(BF16)
        kappa = (key * e_neg).astype(BF16)
        beta_end = (bv * e_end).astype(BF16)
        kappa_end = (key * e_end).astype(BF16)
        gamma = jnp.exp(tot)
        vb = v.astype(BF16)
        for h in range(RW_HEADS):
            sl = slice(h * n, (h + 1) * n)
            chains.append(dict(j=d * RW_HEADS + h, keep=keep, a=alpha[:, sl], rho=rho[:, sl], v=vb[:, sl],
                               lhs=jnp.concatenate([alpha[:, sl], rho[:, sl]], axis=0),
                               rhs=jnp.concatenate([beta[:, sl], kappa[:, sl]], axis=0),
                               end=jnp.concatenate([beta_end[:, sl], kappa_end[:, sl]], axis=0),
                               gamma=gamma[:, sl]))
    yield
    for ch in chains:
        ch["big"] = jnp.where(ch["keep"], _dot_nt(ch["lhs"], ch["rhs"]), 0.0).astype(BF16)
    yield
    for ch in chains:
        ch["lmv"] = _dot(ch["big"][:, C:], ch["v"])
    yield
    tt = lax.broadcasted_iota(jnp.int32, (C, C), 0)
    ss = lax.broadcasted_iota(jnp.int32, (C, C), 1)
    eye = (tt == ss).astype(F32)
    for ch in chains:
        ch["lab"] = ch["big"][:C, :C]
        ch["inv"] = eye - jnp.where((tt >> 1) == (ss >> 1), ch["lab"], 0.0).astype(F32)
    for k in range(1, 6):
        couples = jnp.logical_and((tt >> (k + 1)) == (ss >> (k + 1)), (tt >> k) != (ss >> k))
        for ch in chains:
            ch["ed"] = _dot(jnp.where(couples, ch["lab"], 0.0).astype(BF16), ch["inv"].astype(BF16)).astype(BF16)
        yield
        for ch in chains:
            ch["inv"] = ch["inv"] - _dot(ch["inv"].astype(BF16), ch["ed"])
        yield
    for ch in chains:
        x = -jnp.concatenate([ch["a"].astype(F32), ch["lmv"][:C]], axis=1)
        ch["x"] = _dot(ch["inv"].astype(BF16), x.astype(BF16))
    yield
    for ch in chains:
        ch["st"] = s_ref[ch["j"]]
        ch["ws"] = _dot_nt(jnp.concatenate([ch["x"][:, :n].astype(BF16), ch["rho"]], axis=0), ch["st"].astype(BF16))
    yield
    for ch in chains:
        ch["u"] = (ch["ws"][:C] + ch["x"][:, n:]).astype(BF16)
    for ch in chains:
        ch["y"] = ch["ws"][C:] + _dot(ch["big"][C:, :C], ch["u"]) + ch["lmv"][C:]
    yield
    for ch in chains:
        upd = _dot_tn(jnp.concatenate([ch["u"], ch["v"]], axis=0), ch["end"])
        s_ref[ch["j"]] = ch["st"] * ch["gamma"] + upd
    yf_ref[0, rows_of[0], :] = jnp.concatenate([ch["y"] for ch in chains[:RW_HEADS]], axis=1)
    yb_ref[0, rows_of[1], :] = jnp.concatenate([ch["y"] for ch in chains[RW_HEADS:]], axis=1)


RW_STAGES_PER_ML_STAGE = 4


def _rec_scan_kernel(*refs):
    rw_in, ml_in = refs[:12], refs[12:22]
    yf_ref, yb_ref, hf_ref, hb_ref, s_ref, c_ref, m_ref = refs[22:]

    @pl.when(pl.program_id(1) == 0)
    def _():
        s_ref[...] = jnp.zeros_like(s_ref)
        c_ref[...] = jnp.zeros_like(c_ref)
        m_ref[...] = jnp.zeros_like(m_ref)

    ml = _mlstm_stages(*ml_in, hf_ref, hb_ref, c_ref, m_ref)
    n_stage = 0
    for step in range(ML_CHUNK // RW_CHUNK):
        for _ in _rw_scan_stages(*rw_in, yf_ref, yb_ref, s_ref, step):
            n_stage += 1
            if n_stage % RW_STAGES_PER_ML_STAGE == 0:
                next(ml, None)
    for _ in ml:
        pass


def _rec_scan(r, v, kk, lw, key, bv, q, vm, kt, gates, gates_t, t_lat):
    bsz, s, w = r.shape
    L = ML_CHUNK
    orders = _chunk_order(t_lat // L, (s - t_lat) // L)
    rw_specs, ml_specs, out_rw, out_ml = [], [], [], []
    for d, order in enumerate(orders):
        tok = lambda c, order=order: pl.BlockSpec((1, L, c), lambda b, i: (b, order(i), 0))
        dirtok = pl.BlockSpec((1, 1, L, w), lambda b, i, order=order, d=d: (d, b, order(i), 0))
        timelast = lambda c, order=order: pl.BlockSpec((1, c, L), lambda b, i: (b, 0, order(i)))
        rw_specs += [tok(w), tok(w), tok(w), dirtok, dirtok, dirtok]
        ml_specs += [tok(ML_WIDTH), tok(ML_WIDTH), timelast(ML_WIDTH), tok(LANES), timelast(N_GATES)]
        out_rw.append(tok(w))
        out_ml.append(tok(ML_WIDTH))
    return pl.pallas_call(
        _rec_scan_kernel,
        grid=(bsz, s // L),
        in_specs=rw_specs + ml_specs,
        out_specs=out_rw + out_ml,
        out_shape=[jax.ShapeDtypeStruct((bsz, s, w), F32)] * 2 + [jax.ShapeDtypeStruct((bsz, s, ML_WIDTH), F32)] * 2,
        scratch_shapes=[pltpu.VMEM((2 * RW_HEADS, RW_HEAD_DIM, RW_HEAD_DIM), F32),
                        pltpu.VMEM((2 * ML_HEADS, ML_HEAD_DIM, ML_HEAD_DIM + LANES), F32),
                        pltpu.VMEM((2 * ML_HEADS, LANES), F32)],
        compiler_params=_params("parallel", "arbitrary"),
        name="recurrent_scan",
    )(r, v, kk, lw, key, bv, r, v, kk, lw, key, bv, q, vm, kt, gates, gates_t, q, vm, kt, gates, gates_t)


def _residual_mlp(mix, x, mod, g_ffn, wo_ref, w1_ref, w2_ref, out_ref):
    d = x.shape[-1]
    x1 = x + mod[:, 2 * d:3 * d] * _dot(mix, wo_ref[...])
    h2 = _norm_mod(x1, g_ffn, mod[:, 3 * d:4 * d], mod[:, 4 * d:5 * d]).astype(BF16)
    acc = jnp.zeros_like(x1)
    for k in range(w1_ref.shape[1] // d):
        hid = jnp.maximum(_dot(h2, w1_ref[:, k * d:(k + 1) * d]), 0.0)
        acc = acc + _dot((hid * hid).astype(BF16), w2_ref[k * d:(k + 1) * d, :])
    out_ref[0] = x1 + mod[:, 5 * d:6 * d] * acc


def _rec_post_kernel(hf_ref, hb_ref, o_ref, yf_ref, yb_ref, bonus_ref, gate_ref, x_ref, mod_ref, gf_ref,
                     mlg_ref, rwg_ref, avg_ml_ref, avg_rw_ref, wo_ref, w1_ref, w2_ref, out_ref):
    ml = _group_layernorm(hf_ref[0] + hb_ref[0], avg_ml_ref[...], mlg_ref[...]) * jax.nn.sigmoid(o_ref[0].astype(F32))
    rw = (_group_layernorm(yf_ref[0] + yb_ref[0], avg_rw_ref[...], rwg_ref[...]) + bonus_ref[0]) * gate_ref[0]
    mix = jnp.concatenate([ml, rw], axis=1).astype(BF16)
    _residual_mlp(mix, x_ref[0], mod_ref[0], gf_ref[...], wo_ref, w1_ref, w2_ref, out_ref)


def _att_post_kernel(ow_ref, og_ref, x_ref, mod_ref, gf_ref, wo_ref, w1_ref, w2_ref, out_ref):
    mix = jnp.concatenate([ow_ref[0], og_ref[0]], axis=1)
    _residual_mlp(mix, x_ref[0], mod_ref[0], gf_ref[...], wo_ref, w1_ref, w2_ref, out_ref)


def _post_call(kernel, name, token_inputs, xs, mods, consts, n_rows, n_lat_tiles):
    bsz, _, d = xs.shape
    tm = ROW_TILE
    mod_row = lambda b, i: (jnp.where(i >= n_lat_tiles, bsz, b), 0, 0)
    tok = lambda c: pl.BlockSpec((1, tm, c), lambda b, i: (b, i, 0))
    in_specs = [tok(arr.shape[-1]) for arr in token_inputs]
    in_specs += [tok(d), pl.BlockSpec((1, 1, mods.shape[-1]), mod_row)]
    in_specs += [_resident(c.shape) for c in consts]
    args = list(token_inputs)
    return pl.pallas_call(
        kernel,
        grid=(bsz, n_rows // tm),
        in_specs=in_specs,
        out_specs=pl.BlockSpec((1, tm, d), lambda b, i: (b, i, 0)),
        out_shape=jax.ShapeDtypeStruct((bsz, n_rows, d), F32),
        compiler_params=_params("parallel", "parallel"),
        name=name,
    )(*args, xs, mods, *consts)


ATT_QW, ATT_QG, ATT_KW, ATT_KG, ATT_VW, ATT_VG, ATT_END = 0, 512, 1024, 1152, 1280, 1408, 1536
ATT_QK_END = ATT_VW


def _att_inproj_kernel(x_ref, mod_ref, g_ref, w_ref, qkg_ref, cos_ref, sin_ref, sq_ref,
                       qw_ref, qg_ref, kw_ref, kg_ref, vw_ref, vg0_ref, vg1_ref):
    d = x_ref.shape[-1]
    mod = mod_ref[0]
    h = _norm_mod(x_ref[0], g_ref[...], mod[:, 0:d], mod[:, d:2 * d]).astype(BF16)
    half = ATT_HEAD_DIM // 2
    cos2, sin2 = cos_ref[0], sin_ref[0]

    def norm_rope(lo, hi):
        z = _dot(h, w_ref[:, lo:hi])
        n = hi - lo
        sq = sq_ref[...]
        z = z * lax.rsqrt(_dot((z * z).astype(BF16), sq[:n, :n]) + NORM_EPS) * qkg_ref[:, lo:hi]
        reps = n // LANES
        cos = jnp.concatenate([cos2] * reps, axis=1)
        sin = jnp.concatenate([sin2] * reps, axis=1)
        lane = lax.broadcasted_iota(jnp.int32, z.shape, 1)
        partner = jnp.where(lane % ATT_HEAD_DIM < half, pltpu.roll(z, n - half, 1), pltpu.roll(z, half, 1))
        return (z * cos + partner * sin).astype(BF16)

    qw_ref[0] = norm_rope(ATT_QW, ATT_QG)
    qg_ref[0] = norm_rope(ATT_QG, ATT_KW)
    kk = norm_rope(ATT_KW, ATT_VW)
    kw_ref[0] = kk[:, :KV_WIDTH]
    kg_ref[0] = kk[:, KV_WIDTH:]
    vv = _dot(h, w_ref[:, ATT_VW:ATT_END])
    vw_ref[0] = vv[:, :KV_WIDTH].astype(BF16)
    vg = vv[:, KV_WIDTH:]
    lane = lax.broadcasted_iota(jnp.int32, vg.shape, 1)
    vg0_ref[0] = jnp.where(lane < ATT_HEAD_DIM, vg, jnp.where(lane == ATT_HEAD_DIM, 1.0, 0.0)).astype(BF16)
    vg1_ref[0] = jnp.where(lane >= ATT_HEAD_DIM, vg, jnp.where(lane == 0, 1.0, 0.0)).astype(BF16)


def _att_inproj(xs, mods, gain, w, qk_gain, cos, sin, sq_avg, n_lat_tiles):
    bsz, s, d = xs.shape
    tm = ROW_TILE
    tok = lambda c: pl.BlockSpec((1, tm, c), lambda b, i: (b, i, 0))
    mod_row = lambda b, i: (jnp.where(i >= n_lat_tiles, bsz, b), 0, 0)
    rope = pl.BlockSpec((1, tm, LANES), lambda b, i: (0, i, 0))
    qw = Q_HEADS * ATT_HEAD_DIM
    return pl.pallas_call(
        _att_inproj_kernel,
        grid=(bsz, s // tm),
        in_specs=[tok(d), pl.BlockSpec((1, 1, mods.shape[-1]), mod_row), _resident((1, d)), _resident(w.shape),
                  _resident(qk_gain.shape), rope, rope, _resident(sq_avg.shape)],
        out_specs=[tok(qw), tok(qw)] + [tok(KV_WIDTH)] * 5,
        out_shape=[jax.ShapeDtypeStruct((bsz, s, c), BF16) for c in (qw, qw) + (KV_WIDTH,) * 5],
        compiler_params=_params("parallel", "parallel"),
        name="att_inproj",
    )(xs, mods, gain, w, qk_gain, cos, sin, sq_avg)


def _stack_queries(q):
    lane = lax.broadcasted_iota(jnp.int32, (q.shape[0], LANES), 1)
    zero = jnp.zeros((q.shape[0], LANES), q.dtype)
    slabs = [q[:, g * LANES:(g + 1) * LANES] for g in range(KV_GROUP)]
    lo = [jnp.where(lane < ATT_HEAD_DIM, sl, zero) for sl in slabs]
    hi = [jnp.where(lane >= ATT_HEAD_DIM, sl, zero) for sl in slabs]
    return jnp.concatenate(lo + hi, axis=0)


def _unstack_outputs(o, rows):
    lane = lax.broadcasted_iota(jnp.int32, (rows, LANES), 1)
    slabs = [jnp.where(lane < ATT_HEAD_DIM, o[g * rows:(g + 1) * rows], o[(KV_GROUP + g) * rows:(KV_GROUP + g + 1) * rows])
             for g in range(KV_GROUP)]
    return jnp.concatenate(slabs, axis=1)


def _attn_kernel(qw_ref, kp_ref, ko_ref, kn_ref, kc_ref, vp_ref, vo_ref, vn_ref, vc_ref, sink_ref,
                 qg_ref, k_ref, v0_ref, v1_ref, ow_ref, og_ref, *, tk):
    L = Q_BLOCK
    i, nb = pl.program_id(1), pl.num_programs(1)
    lc = kc_ref.shape[1]
    qs_w = _stack_queries(qw_ref[0])
    keys = jnp.concatenate([kp_ref[0], ko_ref[0], kn_ref[0], kc_ref[0]], axis=0)
    vals = jnp.concatenate([vp_ref[0], vo_ref[0], vn_ref[0], vc_ref[0]], axis=0)
    s_w = _dot_nt(qs_w, keys)

    qs = _stack_queries(qg_ref[0])
    rows = qs.shape[0]
    half = rows // 2
    n_tiles = k_ref.shape[1] // tk
    scores = lambda j: _dot_nt(qs, k_ref[0, j * tk:(j + 1) * tk, :])
    s_next = scores(0)

    nk = 3 * L + lc
    row = lax.broadcasted_iota(jnp.int32, (L, nk), 0)
    col = lax.broadcasted_iota(jnp.int32, (L, nk), 1)
    off = col - L - row
    valid = jnp.logical_and(off <= WINDOW, off >= -WINDOW)
    valid = jnp.logical_and(valid, col >= jnp.where(i > 0, 0, L))
    valid = jnp.logical_and(valid, col < jnp.where(i < nb - 1, 3 * L, 2 * L))
    valid = jnp.logical_or(valid, col >= 3 * L)
    valid = jnp.concatenate([valid] * (2 * KV_GROUP), axis=0)
    s_w = jnp.where(valid, s_w, MASK_VALUE)
    sink = sink_ref[...]
    m_w = jnp.maximum(jnp.max(s_w, axis=1, keepdims=True), sink)
    p_w = jnp.exp(s_w - m_w)
    den_w = jnp.sum(p_w, axis=1, keepdims=True) + jnp.exp(sink - m_w)
    ow_ref[0] = _unstack_outputs(_dot(p_w.astype(BF16), vals) / den_w, L).astype(BF16)

    m = jnp.full((rows, 1), MASK_VALUE, F32)
    acc = jnp.zeros((rows, LANES), F32)
    for j in range(n_tiles):
        s = s_next
        if j + 1 < n_tiles:
            s_next = scores(j + 1)
        m_new = jnp.maximum(m, jnp.max(s, axis=1, keepdims=True))
        p = jnp.exp2(s - m_new).astype(BF16)
        pv = jnp.concatenate([_dot(p[:half], v0_ref[0, j * tk:(j + 1) * tk, :]),
                              _dot(p[half:], v1_ref[0, j * tk:(j + 1) * tk, :])], axis=0)
        acc = jnp.exp2(m - m_new) * acc + pv
        m = m_new
    den = jnp.concatenate([acc[:half, ATT_HEAD_DIM:ATT_HEAD_DIM + 1], acc[half:, 0:1]], axis=0)
    og_ref[0] = _unstack_outputs(acc / den, L).astype(BF16)


GLB_KEY_TILES = (1408, 768, 512, 384, 256, 128)


def _attention(qw, kw, vw, sink_col, qg, kg, vg0, vg1, t_lat):
    bsz, s, width = qw.shape
    L = Q_BLOCK
    nb = t_lat // L
    lc = s - t_lat
    tk = next(t for t in GLB_KEY_TILES if s % t == 0)
    qblk = pl.BlockSpec((1, L, width), lambda b, i: (b, i, 0))
    kv = lambda f: pl.BlockSpec((1, L, KV_WIDTH), lambda b, i: (b, f(i), 0))
    ctx = pl.BlockSpec((1, lc, KV_WIDTH), lambda b, i: (b, t_lat // lc, 0))
    nbr = [kv(lambda i: jnp.maximum(i - 1, 0)), kv(lambda i: i), kv(lambda i: jnp.minimum(i + 1, nb - 1)), ctx]
    whole = pl.BlockSpec((1, s, KV_WIDTH), lambda b, i: (b, 0, 0))
    return pl.pallas_call(
        functools.partial(_attn_kernel, tk=tk),
        grid=(bsz, nb),
        in_specs=[qblk] + nbr + nbr + [_resident(sink_col.shape), qblk, whole, whole, whole],
        out_specs=[qblk, qblk],
        out_shape=[jax.ShapeDtypeStruct((bsz, t_lat, width), BF16)] * 2,
        compiler_params=_params("parallel", "parallel"),
        name="attention",
    )(qw, kw, kw, kw, kw, vw, vw, vw, vw, sink_col, qg, kg, vg0, vg1)


def _block_diag_const(width, group, value):
    idx = np.arange(width) // group
    return jnp.asarray((idx[:, None] == idx[None, :]) * value, dtype=BF16)


def _head_perm():
    return np.concatenate([np.arange(0, ATT_HEAD_DIM, 2), np.arange(1, ATT_HEAD_DIM, 2)])


def _q_head_order():
    return [kv * KV_GROUP + g for g in range(KV_GROUP) for kv in range(KV_HEADS)]


def _rec_weights(w, gate_b, conv_w, w0, w2, a0, a2, k_k, k_a, r_k):
    o = np.cumsum((0,) + (ML_WIDTH,) * 4 + (ML_HEADS,) * 4 + (RW_WIDTH,) * 3
                  + (RW_DECAY_RANK,) * 2 + (RW_ICLR_RANK,) * 2 + (RW_GATE_RANK,))
    q, k, v, og = (w[:, o[i]:o[i + 1]] for i in range(4))
    gates = w[:, o[4]:o[8]]
    rkv = w[:, o[8]:o[11]]
    lora = w[:, o[11]:o[16]]
    gates_pad = jnp.pad(gates, ((0, 0), (0, LANES - N_GATES)))
    w_main = jnp.concatenate([q, v, og, rkv, lora, gates_pad], axis=1).astype(BF16)
    gb = gate_b.reshape(-1)
    gb_row = jnp.pad(gb, (0, LANES - N_GATES)).reshape(1, LANES)
    zeros = jnp.zeros((RW_DECAY_RANK, RW_WIDTH), F32)
    w2cat = jnp.concatenate([jnp.concatenate([w2[0], zeros], 1), jnp.concatenate([zeros, w2[1]], 1)], 0).astype(BF16)
    a2cat = jnp.concatenate([jnp.concatenate([a2[0], zeros], 1), jnp.concatenate([zeros, a2[1]], 1)], 0).astype(BF16)
    vecs = jnp.stack([k_k, k_a, r_k, w0[0], w0[1], a0[0], a0[1], jnp.zeros_like(k_k)])
    return dict(w_main=w_main, wkt=k.T.astype(BF16), wgt=gates.T.astype(BF16), gb_row=gb_row,
                gb_col=gb.reshape(N_GATES, 1), conv=conv_w, vecs=vecs, w2cat=w2cat, a2cat=a2cat)


def _att_weights(w, win_q_norm, win_k_norm, glb_q_norm, glb_k_norm, w_out):
    hd, perm = ATT_HEAD_DIM, _head_perm()
    qn, kn = Q_HEADS * hd, KV_HEADS * hd
    o = np.cumsum((0, qn, kn, kn, qn, kn, kn))
    qw, kw, vw, qg, kg, vg = (w[:, o[i]:o[i + 1]] for i in range(6))
    q_cols = np.concatenate([h * hd + perm for h in _q_head_order()])
    k_cols = np.concatenate([h * hd + perm for h in range(KV_HEADS)])
    w_all = jnp.concatenate([qw[:, q_cols], qg[:, q_cols], kw[:, k_cols], kg[:, k_cols], vw, vg], axis=1).astype(BF16)
    scale = hd ** -0.5
    log2e = float(np.log2(np.e))
    gains = jnp.concatenate([jnp.tile(win_q_norm[perm], Q_HEADS) * scale,
                             jnp.tile(glb_q_norm[perm], Q_HEADS) * (scale * log2e),
                             jnp.tile(win_k_norm[perm], KV_HEADS), jnp.tile(glb_k_norm[perm], KV_HEADS)]).reshape(1, -1)
    out_rows = np.concatenate([h * hd + np.arange(hd) for h in _q_head_order()])
    wo = jnp.concatenate([w_out[:qn][out_rows], w_out[qn:][out_rows]], axis=0).astype(BF16)
    return w_all, gains, wo


def _rope_tables(t_lat, s):
    n_freq = ATT_HEAD_DIM // 4
    pos = jnp.arange(t_lat)
    inv_freq = ROPE_THETA ** (-jnp.arange(n_freq, dtype=F32) / n_freq)
    ang = jnp.concatenate([(pos // GRID_W).astype(F32)[:, None] * inv_freq,
                           (pos % GRID_W).astype(F32)[:, None] * inv_freq], axis=-1)
    cos, sin = jnp.cos(ang), jnp.sin(ang)
    cos_h = jnp.concatenate([cos, cos], axis=-1)
    sin_h = jnp.concatenate([-sin, sin], axis=-1)
    pad = ((0, s - t_lat), (0, 0))
    cos_h = jnp.pad(cos_h, pad, constant_values=1.0)
    sin_h = jnp.pad(sin_h, pad)
    return jnp.tile(cos_h, (1, 2))[None], jnp.tile(sin_h, (1, 2))[None]


def kernel(x, c, ctx, c_ctx, norm_mix, norm_ffn, mod_w, mod_b, out_w, ffn_w1, ffn_w2, rec_in_w, ml_gate_b, ml_norm_g,
           rw_conv, rw_w0, rw_w2, rw_a0, rw_a2, rw_g2, rw_kk, rw_ka, rw_rk, rw_norm_g, att_in_w, win_q_norm,
           win_k_norm, win_sink, glb_q_norm, glb_k_norm):
    bsz, t_lat, d = x.shape
    lc = ctx.shape[1]
    s = t_lat + lc
    depth = mod_w.shape[0]
    assert depth == 2 and bsz + 1 <= SUBLANES
    assert t_lat % ROW_TILE == 0 and lc % ROW_TILE == 0 and t_lat % lc == 0 and t_lat % GRID_W == 0
    n_lat_tiles = t_lat // ROW_TILE

    cvec = jnp.zeros((SUBLANES, d), F32).at[:bsz].set(c).at[bsz].set(c_ctx)
    mods = _modulation(cvec, mod_w, mod_b)[:, :bsz + 1].reshape(depth, bsz + 1, 1, 6 * d)
    xs = jnp.concatenate([x, ctx], axis=1)
    w1 = ffn_w1.astype(BF16)
    w2 = ffn_w2.astype(BF16)

    rw = _rec_weights(rec_in_w[0], ml_gate_b[0], rw_conv[0], rw_w0[0], rw_w2[0], rw_a0[0], rw_a2[0],
                      rw_kk[0], rw_ka[0], rw_rk[0])
    q, v, o, rkv, lora, gates, kt, gates_t = _rec_inproj(
        xs, mods[0], norm_mix[0].reshape(1, d), rw["w_main"], rw["wkt"], rw["wgt"], rw["gb_row"], rw["gb_col"],
        n_lat_tiles)
    ones64 = _block_diag_const(RW_WIDTH, RW_HEAD_DIM, 1.0)
    r, vr, kk, bonus, gate, lw, key, bv = _rw_prep(rkv, lora, rw["conv"], rw["vecs"], rw["w2cat"], rw["a2cat"],
                                                   rw_g2[0].astype(BF16), ones64, t_lat)
    y_f, y_b, h_f, h_b = _rec_scan(r, vr, kk, lw, key, bv, q, v, kt, gates, gates_t, t_lat)
    consts = [norm_ffn[0].reshape(1, d), ml_norm_g[0].reshape(1, -1), rw_norm_g[0].reshape(1, -1),
              _block_diag_const(ML_WIDTH, ML_HEAD_DIM, 1.0 / ML_HEAD_DIM),
              _block_diag_const(RW_WIDTH, RW_HEAD_DIM, 1.0 / RW_HEAD_DIM),
              out_w[0].astype(BF16), w1[0], w2[0]]
    tokens = [h_f, h_b, o, y_f, y_b, bonus, gate]
    xs = _post_call(_rec_post_kernel, "rec_post", tokens, xs, mods[0], consts, s, n_lat_tiles)

    w_att, qk_gain, wo = _att_weights(att_in_w[0], win_q_norm[0], win_k_norm[0], glb_q_norm[0], glb_k_norm[0], out_w[1])
    cos, sin = _rope_tables(t_lat, s)
    sq_avg = _block_diag_const(Q_HEADS * ATT_HEAD_DIM, ATT_HEAD_DIM, 1.0 / ATT_HEAD_DIM)
    qw, qg, kw, kg, vw, vg0, vg1 = _att_inproj(xs, mods[1], norm_mix[1].reshape(1, d), w_att, qk_gain, cos, sin,
                                               sq_avg, n_lat_tiles)
    sink_col = jnp.repeat(win_sink[0][np.array([kv * KV_GROUP + g for kv in range(KV_HEADS) for g in range(KV_GROUP)])],
                          Q_BLOCK).reshape(-1, 1)
    o_win, o_glb = _attention(qw, kw, vw, sink_col, qg, kg, vg0, vg1, t_lat)
    consts = [norm_ffn[1].reshape(1, d), wo, w1[1], w2[1]]
    return _post_call(_att_post_kernel, "att_post", [o_win, o_glb], xs, mods[1], consts, t_lat,
                      n_lat_tiles)
```

```python
import functools

import numpy as np
import jax
import jax.numpy as jnp
from jax import lax
from jax.experimental import pallas as pl
from jax.experimental.pallas import tpu as pltpu

F32 = jnp.float32
BF16 = jnp.bfloat16

GRID_W = 64
NORM_EPS = 1e-6
GN_EPS = 64e-5
ROPE_THETA = 10000.0
ML_HEAD_DIM = 128
ML_HEADS = 4
ML_WIDTH = ML_HEADS * ML_HEAD_DIM
ML_CHUNK = 128
RW_HEAD_DIM = 64
RW_HEADS = 8
RW_WIDTH = RW_HEADS * RW_HEAD_DIM
RW_CHUNK = 64
RW_DECAY_RANK = 64
RW_ICLR_RANK = 64
RW_GATE_RANK = 128
ATT_HEAD_DIM = 64
Q_HEADS = 8
KV_HEADS = 2
KV_GROUP = Q_HEADS // KV_HEADS
KV_WIDTH = KV_HEADS * ATT_HEAD_DIM
WINDOW = 128
Q_BLOCK = 128

LANES = 128
SUBLANES = 8
VMEM_LIMIT_BYTES = 56 * 1024 * 1024
ROW_TILE = 256
MASK_VALUE = -1e30


def _params(*semantics):
    return pltpu.CompilerParams(dimension_semantics=semantics, vmem_limit_bytes=VMEM_LIMIT_BYTES)


def _resident(shape):
    zeros = (0,) * len(shape)
    return pl.BlockSpec(shape, lambda *_: zeros, pipeline_mode=pl.Buffered(1))


def _dot(a, b):
    return jnp.dot(a, b, preferred_element_type=F32)


def _dot_nt(a, b):
    return lax.dot_general(a, b, (((1,), (1,)), ((), ())), preferred_element_type=F32)


def _dot_tn(a, b):
    return lax.dot_general(a, b, (((0,), (0,)), ((), ())), preferred_element_type=F32)


def _split(x, n):
    parts, rest = [], x
    for _ in range(n):
        p = rest.astype(BF16)
        parts.append(p)
        rest = rest - p.astype(F32)
    return parts


def _dot_split_lhs(x, m, n):
    return sum(_dot(p, m) for p in _split(x, n))


def _dot_split_rhs(m, x, n):
    return sum(_dot(m, p) for p in _split(x, n))


def _dot_f32(a, b):
    ah, al = _split(a, 2)
    bh, bl = _split(b, 2)
    return _dot(ah, bh) + _dot(ah, bl) + _dot(al, bh)


def _log_sigmoid(x):
    return jnp.minimum(x, 0.0) - jnp.log1p(jnp.exp(-jnp.abs(x)))


def _norm_mod(x, gain, shift, scale):
    y = x * lax.rsqrt(jnp.mean(x * x, axis=-1, keepdims=True) + NORM_EPS)
    return (y * gain) * (1.0 + scale) + shift


def _group_layernorm(y, avg, gain):
    yc = y - _dot(y.astype(BF16), avg)
    var = _dot((yc * yc).astype(BF16), avg)
    return yc * lax.rsqrt(var + GN_EPS) * gain


def _mod_kernel(cv_ref, w_ref, b_ref, o_ref):
    cv = cv_ref[...]
    o_ref[0] = _dot_f32(cv * jax.nn.sigmoid(cv), w_ref[0]) + b_ref[0]


def _modulation(cvec, mod_w, mod_b):
    depth, d, n = mod_w.shape
    tn = n // 4
    return pl.pallas_call(
        _mod_kernel,
        grid=(depth, n // tn),
        in_specs=[pl.BlockSpec((SUBLANES, d), lambda l, j: (0, 0)),
                  pl.BlockSpec((1, d, tn), lambda l, j: (l, 0, j)),
                  pl.BlockSpec((1, 1, tn), lambda l, j: (l, 0, j))],
        out_specs=pl.BlockSpec((1, SUBLANES, tn), lambda l, j: (l, 0, j)),
        out_shape=jax.ShapeDtypeStruct((depth, SUBLANES, n), F32),
        compiler_params=_params("parallel", "parallel"),
        name="adaln_modulation",
    )(cvec, mod_w, mod_b.reshape(depth, 1, n))


REC_Q, REC_V, REC_O, REC_RKV, REC_LORA, REC_GATE, REC_END = 0, 512, 1024, 1536, 3072, 3456, 3584
N_GATES = 4 * ML_HEADS


def _token_rows(x_ref, ctx_ref, n_lat_tiles):
    return jnp.where(pl.program_id(1) >= n_lat_tiles, ctx_ref[0], x_ref[0])


def _token_specs(tm, d, n_lat_tiles):
    return [pl.BlockSpec((1, tm, d), lambda b, i: (b, jnp.minimum(i, n_lat_tiles - 1), 0)),
            pl.BlockSpec((1, tm, d), lambda b, i: (b, jnp.maximum(i - n_lat_tiles, 0), 0))]


def _rec_inproj_kernel(x_ref, ctx_ref, mod_ref, g_ref, w_ref, wkt_ref, wgt_ref, gb_ref, gbt_ref,
                       q_ref, v_ref, o_ref, rkv_ref, lora_ref, gate_ref, kt_ref, gatet_ref, *, n_lat_tiles):
    d = x_ref.shape[-1]
    mod = mod_ref[0]
    h = _norm_mod(_token_rows(x_ref, ctx_ref, n_lat_tiles), g_ref[...], mod[:, 0:d], mod[:, d:2 * d]).astype(BF16)
    q_ref[0] = _dot(h, w_ref[:, REC_Q:REC_V]).astype(BF16)
    v_ref[0] = _dot(h, w_ref[:, REC_V:REC_O]).astype(BF16)
    o_ref[0] = _dot(h, w_ref[:, REC_O:REC_RKV]).astype(BF16)
    rkv_ref[0] = _dot(h, w_ref[:, REC_RKV:REC_LORA])
    lora_ref[0] = _dot(h, w_ref[:, REC_LORA:REC_GATE])
    gate_ref[0] = _dot(h, w_ref[:, REC_GATE:REC_END]) + gb_ref[...]
    kt_ref[0] = _dot_nt(wkt_ref[...], h).astype(BF16)
    gatet_ref[0] = _dot_nt(wgt_ref[...], h) + gbt_ref[...]


def _rec_inproj(x, ctx, mods, gain, w_main, wkt, wgt, gb_row, gb_col, n_lat_tiles):
    bsz, _, d = x.shape
    s = x.shape[1] + ctx.shape[1]
    tm = ROW_TILE
    tok = lambda c: pl.BlockSpec((1, tm, c), lambda b, i: (b, i, 0))
    mod_row = lambda b, i: (jnp.where(i >= n_lat_tiles, bsz, b), 0, 0)
    return pl.pallas_call(
        functools.partial(_rec_inproj_kernel, n_lat_tiles=n_lat_tiles),
        grid=(bsz, s // tm),
        in_specs=_token_specs(tm, d, n_lat_tiles) + [
                  pl.BlockSpec((1, 1, mods.shape[-1]), mod_row),
                  _resident((1, d)), _resident(w_main.shape), _resident(wkt.shape), _resident(wgt.shape),
                  _resident(gb_row.shape), _resident(gb_col.shape)],
        out_specs=[tok(ML_WIDTH), tok(ML_WIDTH), tok(ML_WIDTH), tok(3 * RW_WIDTH), tok(REC_GATE - REC_LORA),
                   tok(LANES),
                   pl.BlockSpec((1, ML_WIDTH, tm), lambda b, i: (b, 0, i)),
                   pl.BlockSpec((1, N_GATES, tm), lambda b, i: (b, 0, i))],
        out_shape=[jax.ShapeDtypeStruct((bsz, s, ML_WIDTH), BF16),
                   jax.ShapeDtypeStruct((bsz, s, ML_WIDTH), BF16),
                   jax.ShapeDtypeStruct((bsz, s, ML_WIDTH), BF16),
                   jax.ShapeDtypeStruct((bsz, s, 3 * RW_WIDTH), F32),
                   jax.ShapeDtypeStruct((bsz, s, REC_GATE - REC_LORA), F32),
                   jax.ShapeDtypeStruct((bsz, s, LANES), F32),
                   jax.ShapeDtypeStruct((bsz, ML_WIDTH, s), BF16),
                   jax.ShapeDtypeStruct((bsz, N_GATES, s), F32)],
        compiler_params=_params("parallel", "parallel"),
        name="rec_inproj",
    )(x, ctx, mods, gain, w_main, wkt, wgt, gb_row, gb_col)


def _mlstm_stages(qf_ref, vf_ref, ktf_ref, gf_ref, gtf_ref, qb_ref, vb_ref, ktb_ref, gb_ref, gtb_ref,
                  hf_ref, hb_ref, c_ref, m_ref):
    L, dh = ML_CHUNK, ML_HEAD_DIM
    t_idx = lax.broadcasted_iota(jnp.int32, (L, L), 0)
    s_idx = lax.broadcasted_iota(jnp.int32, (L, L), 1)
    ones_col = (lax.broadcasted_iota(jnp.int32, (L, LANES), 1) == 0).astype(BF16)
    scale = dh ** -0.5
    streams = ((qf_ref, vf_ref, ktf_ref, gf_ref, gtf_ref, hf_ref),
               (qb_ref, vb_ref, ktb_ref, gb_ref, gtb_ref, hb_ref))
    chains = []
    for d, (q_ref, v_ref, kt_ref, g_ref, gt_ref, h_ref) in enumerate(streams):
        earlier = (s_idx <= t_idx) if d == 0 else (s_idx >= t_idx)
        tri = earlier.astype(BF16)
        tri_t = ((t_idx <= s_idx) if d == 0 else (t_idx >= s_idx)).astype(BF16)
        gt = gt_ref[0]
        lf_rows = _log_sigmoid(gt)
        b_cols = _dot_split_rhs(tri, _log_sigmoid(g_ref[0]), 3)
        b_rows = _dot_split_lhs(lf_rows, tri_t, 3)
        tot = jnp.sum(lf_rows, axis=1, keepdims=True)
        q, v, kt = q_ref[0], v_ref[0], kt_ref[0]
        for hh in range(ML_HEADS):
            ic, fc = 2 * ML_HEADS * d + hh, 2 * ML_HEADS * d + ML_HEADS + hh
            chains.append(dict(j=d * ML_HEADS + hh, hh=hh, h_ref=h_ref, earlier=earlier,
                               b_col=b_cols[:, fc:fc + 1], b_row=b_rows[fc:fc + 1, :], i_row=gt[ic:ic + 1, :],
                               b_end=tot[fc:fc + 1, :], q=q[:, hh * dh:(hh + 1) * dh], kt=kt[hh * dh:(hh + 1) * dh, :],
                               v_aug=jnp.concatenate([v[:, hh * dh:(hh + 1) * dh], ones_col], axis=1)))
    yield
    for ch in chains:
        ch["cst"] = c_ref[ch["j"]]
        ch["qk"] = _dot(ch["q"], ch["kt"])
        ch["qc"] = _dot(ch["q"], ch["cst"].astype(BF16))
    yield
    for ch in chains:
        m_prev = m_ref[ch["j"]:ch["j"] + 1, 0:1]
        d_in = jnp.where(ch["earlier"], ch["b_col"] - ch["b_row"] + ch["i_row"], MASK_VALUE)
        d_prev = ch["b_col"] + m_prev
        m_t = jnp.maximum(d_prev, jnp.max(d_in, axis=1, keepdims=True))
        ch["s"] = (ch["qk"] * (scale * jnp.exp(d_in - m_t))).astype(BF16)
        ch["w_prev"], ch["floor"] = jnp.exp(d_prev - m_t), jnp.exp(-m_t)
        d_end = ch["b_end"] - ch["b_row"] + ch["i_row"]
        m_new = jnp.maximum(ch["b_end"] + m_prev, jnp.max(d_end, axis=1, keepdims=True))
        ch["kw"] = (ch["kt"].astype(F32) * (jnp.exp(d_end - m_new) * scale)).astype(BF16)
        ch["decay"] = jnp.exp(ch["b_end"] + m_prev - m_new)
        m_ref[ch["j"]:ch["j"] + 1, :] = jnp.broadcast_to(m_new, (1, LANES))
    yield
    for ch in chains:
        acc = _dot(ch["s"], ch["v_aug"]) + ch["w_prev"] * ch["qc"]
        den = jnp.maximum(jnp.abs(acc[:, dh:dh + 1]), ch["floor"])
        ch["h_ref"][0, :, ch["hh"] * dh:(ch["hh"] + 1) * dh] = acc[:, :dh] / den
    yield
    for ch in chains:
        c_ref[ch["j"]] = ch["decay"] * ch["cst"] + _dot(ch["kw"], ch["v_aug"])


def _chunk_order(n_lat, n_ctx):
    n = n_lat + n_ctx
    fwd = lambda c: (c + n_lat) % n
    bwd = lambda c: n - 1 - c
    return fwd, bwd


def _rw_prep_kernel(z_ref, zp_ref, zn_ref, lora_ref, conv_ref, vec_ref, w2_ref, a2_ref, g2_ref, ones_ref,
                    r_ref, v_ref, kk_ref, bonus_ref, gate_ref, lw_ref, key_ref, bv_ref, *, t_lat, s_tot):
    tm, w = z_ref.shape[1], RW_WIDTH
    row0 = pl.program_id(1) * tm
    has_prev = jnp.where(jnp.logical_and(row0 != 0, row0 != t_lat), 1.0, 0.0)
    has_next = jnp.where(jnp.logical_and(row0 + tm != t_lat, row0 + tm != s_tot), 1.0, 0.0)
    z = z_ref[0]
    prev_row = zp_ref[0][SUBLANES - 1:SUBLANES, :] * has_prev
    next_row = zn_ref[0][0:1, :] * has_next
    ridx = lax.broadcasted_iota(jnp.int32, z.shape, 0)
    z_prev = jnp.where(ridx == 0, prev_row, pltpu.roll(z, 1, 0))
    z_next = jnp.where(ridx == tm - 1, next_row, pltpu.roll(z, tm - 1, 0))
    cw = conv_ref[...]
    zc = cw[0:1] * z_prev + cw[1:2] * z + cw[2:3] * z_next
    r, kr, vr = zc[:, :w], zc[:, w:2 * w], zc[:, 2 * w:]
    vec = vec_ref[...]
    k_k, k_a, r_k = vec[0:1], vec[1:2], vec[2:3]
    ones = ones_ref[...]
    kkr = kr * k_k
    kk = kkr * lax.rsqrt(_dot_split_lhs(kkr * kkr, ones, 2) + NORM_EPS)
    r_ref[0] = r.astype(BF16)
    v_ref[0] = vr.astype(BF16)
    kk_ref[0] = kk.astype(BF16)
    bonus_ref[0] = (_dot_split_lhs(r * kr * r_k, ones, 2) * vr).astype(BF16)
    lora = lora_ref[0]
    dec = _dot(jnp.tanh(lora[:, 0:LANES]).astype(BF16), w2_ref[...])
    icl = _dot(lora[:, LANES:2 * LANES].astype(BF16), a2_ref[...])
    gate_ref[0] = _dot(jax.nn.sigmoid(lora[:, 2 * LANES:3 * LANES]).astype(BF16), g2_ref[...]).astype(BF16)
    for d in range(2):
        w_log = _log_sigmoid(vec[3 + d:4 + d] + dec[:, d * w:(d + 1) * w]) - 0.5
        a = jax.nn.sigmoid(vec[5 + d:6 + d] + icl[:, d * w:(d + 1) * w])
        lw_ref[d, 0] = -jnp.exp(w_log)
        key_ref[d, 0] = (kr * (1.0 + (a - 1.0) * k_a)).astype(BF16)
        bv_ref[d, 0] = (kk * a).astype(BF16)


def _rw_prep(rkv, lora, conv_w, vecs, w2cat, a2cat, g2, ones64, t_lat):
    bsz, s, _ = rkv.shape
    tm, w = ROW_TILE, RW_WIDTH
    nsub = tm // SUBLANES
    tok = lambda c: pl.BlockSpec((1, tm, c), lambda b, i: (b, i, 0))
    dirtok = pl.BlockSpec((2, 1, tm, w), lambda b, i: (0, b, i, 0))
    last = s // SUBLANES - 1
    return pl.pallas_call(
        functools.partial(_rw_prep_kernel, t_lat=t_lat, s_tot=s),
        grid=(bsz, s // tm),
        in_specs=[tok(3 * w),
                  pl.BlockSpec((1, SUBLANES, 3 * w), lambda b, i: (b, jnp.maximum(i * nsub - 1, 0), 0)),
                  pl.BlockSpec((1, SUBLANES, 3 * w), lambda b, i: (b, jnp.minimum((i + 1) * nsub, last), 0)),
                  tok(lora.shape[-1]),
                  _resident(conv_w.shape), _resident(vecs.shape), _resident(w2cat.shape), _resident(a2cat.shape),
                  _resident(g2.shape), _resident(ones64.shape)],
        out_specs=[tok(w)] * 5 + [dirtok] * 3,
        out_shape=[jax.ShapeDtypeStruct((bsz, s, w), BF16)] * 5
        + [jax.ShapeDtypeStruct((2, bsz, s, w), dt) for dt in (F32, BF16, BF16)],
        compiler_params=_params("parallel", "parallel"),
        name="rwkv_prep",
    )(rkv, rkv, rkv, lora, conv_w, vecs, w2cat, a2cat, g2, ones64)


def _rw_scan_stages(rf_ref, vf_ref, kkf_ref, lwf_ref, keyf_ref, bvf_ref, rb_ref, vb_ref, kkb_ref, lwb_ref, keyb_ref,
                    bvb_ref, yf_ref, yb_ref, s_ref, step):
    C, n = RW_CHUNK, RW_HEAD_DIM
    t2 = lax.broadcasted_iota(jnp.int32, (2 * C, 2 * C), 0) % C
    s2 = lax.broadcasted_iota(jnp.int32, (2 * C, 2 * C), 1) % C
    lower = lax.broadcasted_iota(jnp.int32, (2 * C, 2 * C), 0) >= C

    chains = []
    rows_of = (slice(step * C, (step + 1) * C), slice((1 - step) * C, (2 - step) * C))
    for d, (r_ref, v_ref, kk_ref, lw_ref, key_ref, bv_ref) in enumerate(
            ((rf_ref, vf_ref, kkf_ref, lwf_ref, keyf_ref, bvf_ref), (rb_ref, vb_ref, kkb_ref, lwb_ref, keyb_ref, bvb_ref))):
        rows = rows_of[d]
        rel = (s2 - t2) if d == 0 else (t2 - s2)
        keep = rel < jnp.where(lower, 1, 0)
        tri = (rel[:C, :C] <= 0).astype(BF16)
        lw = lw_ref[0, 0, rows, :]
        cum = _dot_split_rhs(tri, lw, 3)
        tot = jnp.sum(lw, axis=0, keepdims=True)
        r, v, kk, key, bv = (a.astype(F32) for a in (r_ref[0, rows, :], v_ref[0, rows, :], kk_ref[0, rows, :],
                                                     key_ref[0, 0, rows, :], bv_ref[0, 0, rows, :]))
        e_neg = jnp.exp(-cum)
        e_end = jnp.exp(tot - cum)
        alpha = (kk * jnp.exp(cum - lw)).astype(BF16)
        rho = (r * jnp.exp(cum)).astype(BF16)
        beta = (bv * e_neg).astype(BF16)
        kappa = (key * e_neg).astype(BF16)
        beta_end = (bv * e_end).astype(BF16)
        kappa_end = (key * e_end).astype(BF16)
        gamma = jnp.exp(tot)
        vb = v.astype(BF16)
        for h in range(RW_HEADS):
            sl = slice(h * n, (h + 1) * n)
            chains.append(dict(j=d * RW_HEADS + h, keep=keep, a=alpha[:, sl], rho=rho[:, sl], v=vb[:, sl],
                               lhs=jnp.concatenate([alpha[:, sl], rho[:, sl]], axis=0),
                               rhs=jnp.concatenate([beta[:, sl], kappa[:, sl]], axis=0),
                               end=jnp.concatenate([beta_end[:, sl], kappa_end[:, sl]], axis=0),
                               gamma=gamma[:, sl]))
    yield
    for ch in chains:
        ch["big"] = jnp.where(ch["keep"], _dot_nt(ch["lhs"], ch["rhs"]), 0.0).astype(BF16)
    yield
    for ch in chains:
        ch["lmv"] = _dot(ch["big"][:, C:], ch["v"])
    yield
    tt = lax.broadcasted_iota(jnp.int32, (C, C), 0)
    ss = lax.broadcasted_iota(jnp.int32, (C, C), 1)
    eye = (tt == ss).astype(F32)
    for ch in chains:
        ch["lab"] = ch["big"][:C, :C]
        ch["inv"] = eye - jnp.where((tt >> 1) == (ss >> 1), ch["lab"], 0.0).astype(F32)
    for k in range(1, 6):
        couples = jnp.logical_and((tt >> (k + 1)) == (ss >> (k + 1)), (tt >> k) != (ss >> k))
        for ch in chains:
            ch["ed"] = _dot(jnp.where(couples, ch["lab"], 0.0).astype(BF16), ch["inv"].astype(BF16)).astype(BF16)
        yield
        for ch in chains:
            ch["inv"] = ch["inv"] - _dot(ch["inv"].astype(BF16), ch["ed"])
        yield
    for ch in chains:
        x = -jnp.concatenate([ch["a"].astype(F32), ch["lmv"][:C]], axis=1)
        ch["x"] = _dot(ch["inv"].astype(BF16), x.astype(BF16))
    yield
    for ch in chains:
        ch["st"] = s_ref[ch["j"]]
        ch["ws"] = _dot_nt(jnp.concatenate([ch["x"][:, :n].astype(BF16), ch["rho"]], axis=0), ch["st"].astype(BF16))
    yield
    for ch in chains:
        ch["u"] = (ch["ws"][:C] + ch["x"][:, n:]).astype(BF16)
    for ch in chains:
        ch["y"] = ch["ws"][C:] + _dot(ch["big"][C:, :C], ch["u"]) + ch["lmv"][C:]
    yield
    for ch in chains:
        upd = _dot_tn(jnp.concatenate([ch["u"], ch["v"]], axis=0), ch["end"])
        s_ref[ch["j"]] = ch["st"] * ch["gamma"] + upd
    yf_ref[0, rows_of[0], :] = jnp.concatenate([ch["y"] for ch in chains[:RW_HEADS]], axis=1)
    yb_ref[0, rows_of[1], :] = jnp.concatenate([ch["y"] for ch in chains[RW_HEADS:]], axis=1)


RW_STAGES_PER_ML_STAGE = 4


def _rec_scan_kernel(*refs):
    rw_in, ml_in = refs[:12], refs[12:22]
    yf_ref, yb_ref, hf_ref, hb_ref, s_ref, c_ref, m_ref = refs[22:]

    @pl.when(pl.program_id(1) == 0)
    def _():
        s_ref[...] = jnp.zeros_like(s_ref)
        c_ref[...] = jnp.zeros_like(c_ref)
        m_ref[...] = jnp.zeros_like(m_ref)

    ml = _mlstm_stages(*ml_in, hf_ref, hb_ref, c_ref, m_ref)
    n_stage = 0
    for step in range(ML_CHUNK // RW_CHUNK):
        for _ in _rw_scan_stages(*rw_in, yf_ref, yb_ref, s_ref, step):
            n_stage += 1
            if n_stage % RW_STAGES_PER_ML_STAGE == 0:
                next(ml, None)
    for _ in ml:
        pass


def _rec_scan(r, v, kk, lw, key, bv, q, vm, kt, gates, gates_t, t_lat):
    bsz, s, w = r.shape
    L = ML_CHUNK
    orders = _chunk_order(t_lat // L, (s - t_lat) // L)
    rw_specs, ml_specs, out_rw, out_ml = [], [], [], []
    for d, order in enumerate(orders):
        tok = lambda c, order=order: pl.BlockSpec((1, L, c), lambda b, i: (b, order(i), 0))
        dirtok = pl.BlockSpec((1, 1, L, w), lambda b, i, order=order, d=d: (d, b, order(i), 0))
        timelast = lambda c, order=order: pl.BlockSpec((1, c, L), lambda b, i: (b, 0, order(i)))
        rw_specs += [tok(w), tok(w), tok(w), dirtok, dirtok, dirtok]
        ml_specs += [tok(ML_WIDTH), tok(ML_WIDTH), timelast(ML_WIDTH), tok(LANES), timelast(N_GATES)]
        out_rw.append(tok(w))
        out_ml.append(tok(ML_WIDTH))
    return pl.pallas_call(
        _rec_scan_kernel,
        grid=(bsz, s // L),
        in_specs=rw_specs + ml_specs,
        out_specs=out_rw + out_ml,
        out_shape=[jax.ShapeDtypeStruct((bsz, s, w), F32)] * 2 + [jax.ShapeDtypeStruct((bsz, s, ML_WIDTH), F32)] * 2,
        scratch_shapes=[pltpu.VMEM((2 * RW_HEADS, RW_HEAD_DIM, RW_HEAD_DIM), F32),
                        pltpu.VMEM((2 * ML_HEADS, ML_HEAD_DIM, ML_HEAD_DIM + LANES), F32),
                        pltpu.VMEM((2 * ML_HEADS, LANES), F32)],
        compiler_params=_params("parallel", "arbitrary"),
        name="recurrent_scan",
    )(r, v, kk, lw, key, bv, r, v, kk, lw, key, bv, q, vm, kt, gates, gates_t, q, vm, kt, gates, gates_t)


def _residual_mlp(mix, x, mod, g_ffn, wo_ref, w1_ref, w2_ref, out_ref):
    d = x.shape[-1]
    x1 = x + mod[:, 2 * d:3 * d] * _dot(mix, wo_ref[...])
    h2 = _norm_mod(x1, g_ffn, mod[:, 3 * d:4 * d], mod[:, 4 * d:5 * d]).astype(BF16)
    acc = jnp.zeros_like(x1)
    for k in range(w1_ref.shape[1] // d):
        hid = jnp.maximum(_dot(h2, w1_ref[:, k * d:(k + 1) * d]), 0.0)
        acc = acc + _dot((hid * hid).astype(BF16), w2_ref[k * d:(k + 1) * d, :])
    out_ref[0] = x1 + mod[:, 5 * d:6 * d] * acc


def _rec_post_kernel(hf_ref, hb_ref, o_ref, yf_ref, yb_ref, bonus_ref, gate_ref, x_ref, ctx_ref, mod_ref, gf_ref,
                     mlg_ref, rwg_ref, avg_ml_ref, avg_rw_ref, wo_ref, w1_ref, w2_ref, out_ref, *, n_lat_tiles):
    ml = _group_layernorm(hf_ref[0] + hb_ref[0], avg_ml_ref[...], mlg_ref[...]) * jax.nn.sigmoid(o_ref[0].astype(F32))
    rw = (_group_layernorm(yf_ref[0] + yb_ref[0], avg_rw_ref[...], rwg_ref[...]) + bonus_ref[0]) * gate_ref[0]
    mix = jnp.concatenate([ml, rw], axis=1).astype(BF16)
    x = _token_rows(x_ref, ctx_ref, n_lat_tiles)
    _residual_mlp(mix, x, mod_ref[0], gf_ref[...], wo_ref, w1_ref, w2_ref, out_ref)


def _att_post_kernel(ow_ref, og_ref, x_ref, mod_ref, gf_ref, wo_ref, w1_ref, w2_ref, out_ref):
    mix = jnp.concatenate([ow_ref[0], og_ref[0]], axis=1)
    _residual_mlp(mix, x_ref[0], mod_ref[0], gf_ref[...], wo_ref, w1_ref, w2_ref, out_ref)


def _post_call(kernel, name, token_inputs, residual, mods, consts, n_rows, n_lat_tiles):
    bsz, _, d = residual[0].shape
    tm = ROW_TILE
    mod_row = lambda b, i: (jnp.where(i >= n_lat_tiles, bsz, b), 0, 0)
    tok = lambda c: pl.BlockSpec((1, tm, c), lambda b, i: (b, i, 0))
    in_specs = [tok(arr.shape[-1]) for arr in token_inputs]
    in_specs += _token_specs(tm, d, n_lat_tiles) if len(residual) == 2 else [tok(d)]
    in_specs += [pl.BlockSpec((1, 1, mods.shape[-1]), mod_row)]
    in_specs += [_resident(c.shape) for c in consts]
    args = list(token_inputs) + list(residual)
    return pl.pallas_call(
        kernel,
        grid=(bsz, n_rows // tm),
        in_specs=in_specs,
        out_specs=pl.BlockSpec((1, tm, d), lambda b, i: (b, i, 0)),
        out_shape=jax.ShapeDtypeStruct((bsz, n_rows, d), F32),
        compiler_params=_params("parallel", "parallel"),
        name=name,
    )(*args, mods, *consts)


ATT_QW, ATT_QG, ATT_KW, ATT_KG, ATT_VW, ATT_VG, ATT_END = 0, 512, 1024, 1152, 1280, 1408, 1536
ATT_QK_END = ATT_VW


def _att_inproj_kernel(x_ref, mod_ref, g_ref, w_ref, qkg_ref, cos_ref, sin_ref, sq_ref,
                       qw_ref, qg_ref, kw_ref, kg_ref, vw_ref, vg0_ref, vg1_ref):
    d = x_ref.shape[-1]
    mod = mod_ref[0]
    h = _norm_mod(x_ref[0], g_ref[...], mod[:, 0:d], mod[:, d:2 * d]).astype(BF16)
    half = ATT_HEAD_DIM // 2
    cos2, sin2 = cos_ref[0], sin_ref[0]

    def norm_rope(lo, hi):
        z = _dot(h, w_ref[:, lo:hi])
        n = hi - lo
        sq = sq_ref[...]
        z = z * lax.rsqrt(_dot((z * z).astype(BF16), sq[:n, :n]) + NORM_EPS) * qkg_ref[:, lo:hi]
        reps = n // LANES
        cos = jnp.concatenate([cos2] * reps, axis=1)
        sin = jnp.concatenate([sin2] * reps, axis=1)
        lane = lax.broadcasted_iota(jnp.int32, z.shape, 1)
        partner = jnp.where(lane % ATT_HEAD_DIM < half, pltpu.roll(z, n - half, 1), pltpu.roll(z, half, 1))
        return (z * cos + partner * sin).astype(BF16)

    qw_ref[0] = norm_rope(ATT_QW, ATT_QG)
    qg_ref[0] = norm_rope(ATT_QG, ATT_KW)
    kk = norm_rope(ATT_KW, ATT_VW)
    kw_ref[0] = kk[:, :KV_WIDTH]
    kg_ref[0] = kk[:, KV_WIDTH:]
    vv = _dot(h, w_ref[:, ATT_VW:ATT_END])
    vw_ref[0] = vv[:, :KV_WIDTH].astype(BF16)
    vg = vv[:, KV_WIDTH:]
    lane = lax.broadcasted_iota(jnp.int32, vg.shape, 1)
    vg0_ref[0] = jnp.where(lane < ATT_HEAD_DIM, vg, jnp.where(lane == ATT_HEAD_DIM, 1.0, 0.0)).astype(BF16)
    vg1_ref[0] = jnp.where(lane >= ATT_HEAD_DIM, vg, jnp.where(lane == 0, 1.0, 0.0)).astype(BF16)


def _att_inproj(xs, mods, gain, w, qk_gain, cos, sin, sq_avg, n_lat_tiles):
    bsz, s, d = xs.shape
    tm = ROW_TILE
    tok = lambda c: pl.BlockSpec((1, tm, c), lambda b, i: (b, i, 0))
    mod_row = lambda b, i: (jnp.where(i >= n_lat_tiles, bsz, b), 0, 0)
    rope = pl.BlockSpec((1, tm, LANES), lambda b, i: (0, i, 0))
    qw = Q_HEADS * ATT_HEAD_DIM
    return pl.pallas_call(
        _att_inproj_kernel,
        grid=(bsz, s // tm),
        in_specs=[tok(d), pl.BlockSpec((1, 1, mods.shape[-1]), mod_row), _resident((1, d)), _resident(w.shape),
                  _resident(qk_gain.shape), rope, rope, _resident(sq_avg.shape)],
        out_specs=[tok(qw), tok(qw)] + [tok(KV_WIDTH)] * 5,
        out_shape=[jax.ShapeDtypeStruct((bsz, s, c), BF16) for c in (qw, qw) + (KV_WIDTH,) * 5],
        compiler_params=_params("parallel", "parallel"),
        name="att_inproj",
    )(xs, mods, gain, w, qk_gain, cos, sin, sq_avg)


def _stack_queries(q):
    lane = lax.broadcasted_iota(jnp.int32, (q.shape[0], LANES), 1)
    zero = jnp.zeros((q.shape[0], LANES), q.dtype)
    slabs = [q[:, g * LANES:(g + 1) * LANES] for g in range(KV_GROUP)]
    lo = [jnp.where(lane < ATT_HEAD_DIM, sl, zero) for sl in slabs]
    hi = [jnp.where(lane >= ATT_HEAD_DIM, sl, zero) for sl in slabs]
    return jnp.concatenate(lo + hi, axis=0)


def _unstack_outputs(o, rows):
    lane = lax.broadcasted_iota(jnp.int32, (rows, LANES), 1)
    slabs = [jnp.where(lane < ATT_HEAD_DIM, o[g * rows:(g + 1) * rows], o[(KV_GROUP + g) * rows:(KV_GROUP + g + 1) * rows])
             for g in range(KV_GROUP)]
    return jnp.concatenate(slabs, axis=1)


def _attn_kernel(qw_ref, kp_ref, ko_ref, kn_ref, kc_ref, vp_ref, vo_ref, vn_ref, vc_ref, sink_ref,
                 qg_ref, k_ref, v0_ref, v1_ref, ow_ref, og_ref, *, tk):
    L = Q_BLOCK
    i, nb = pl.program_id(1), pl.num_programs(1)
    lc = kc_ref.shape[1]
    qs_w = _stack_queries(qw_ref[0])
    keys = jnp.concatenate([kp_ref[0], ko_ref[0], kn_ref[0], kc_ref[0]], axis=0)
    vals = jnp.concatenate([vp_ref[0], vo_ref[0], vn_ref[0], vc_ref[0]], axis=0)
    s_w = _dot_nt(qs_w, keys)

    qs = _stack_queries(qg_ref[0])
    rows = qs.shape[0]
    half = rows // 2
    n_tiles = k_ref.shape[1] // tk
    scores = lambda j: _dot_nt(qs, k_ref[0, j * tk:(j + 1) * tk, :])
    s_next = scores(0)

    nk = 3 * L + lc
    row = lax.broadcasted_iota(jnp.int32, (L, nk), 0)
    col = lax.broadcasted_iota(jnp.int32, (L, nk), 1)
    off = col - L - row
    valid = jnp.logical_and(off <= WINDOW, off >= -WINDOW)
    valid = jnp.logical_and(valid, col >= jnp.where(i > 0, 0, L))
    valid = jnp.logical_and(valid, col < jnp.where(i < nb - 1, 3 * L, 2 * L))
    valid = jnp.logical_or(valid, col >= 3 * L)
    valid = jnp.concatenate([valid] * (2 * KV_GROUP), axis=0)
    s_w = jnp.where(valid, s_w, MASK_VALUE)
    sink = sink_ref[...]
    m_w = jnp.maximum(jnp.max(s_w, axis=1, keepdims=True), sink)
    p_w = jnp.exp(s_w - m_w)
    den_w = jnp.sum(p_w, axis=1, keepdims=True) + jnp.exp(sink - m_w)
    ow_ref[0] = _unstack_outputs(_dot(p_w.astype(BF16), vals) / den_w, L).astype(BF16)

    m = jnp.full((rows, 1), MASK_VALUE, F32)
    acc = jnp.zeros((rows, LANES), F32)
    for j in range(n_tiles):
        s = s_next
        if j + 1 < n_tiles:
            s_next = scores(j + 1)
        m_new = jnp.maximum(m, jnp.max(s, axis=1, keepdims=True))
        p = jnp.exp2(s - m_new).astype(BF16)
        pv = jnp.concatenate([_dot(p[:half], v0_ref[0, j * tk:(j + 1) * tk, :]),
                              _dot(p[half:], v1_ref[0, j * tk:(j + 1) * tk, :])], axis=0)
        acc = jnp.exp2(m - m_new) * acc + pv
        m = m_new
    den = jnp.concatenate([acc[:half, ATT_HEAD_DIM:ATT_HEAD_DIM + 1], acc[half:, 0:1]], axis=0)
    og_ref[0] = _unstack_outputs(acc / den, L).astype(BF16)


GLB_KEY_TILES = (1408, 768, 512, 384, 256, 128)


def _attention(qw, kw, vw, sink_col, qg, kg, vg0, vg1, t_lat):
    bsz, s, width = qw.shape
    L = Q_BLOCK
    nb = t_lat // L
    lc = s - t_lat
    tk = next(t for t in GLB_KEY_TILES if s % t == 0)
    qblk = pl.BlockSpec((1, L, width), lambda b, i: (b, i, 0))
    kv = lambda f: pl.BlockSpec((1, L, KV_WIDTH), lambda b, i: (b, f(i), 0))
    ctx = pl.BlockSpec((1, lc, KV_WIDTH), lambda b, i: (b, t_lat // lc, 0))
    nbr = [kv(lambda i: jnp.maximum(i - 1, 0)), kv(lambda i: i), kv(lambda i: jnp.minimum(i + 1, nb - 1)), ctx]
    whole = pl.BlockSpec((1, s, KV_WIDTH), lambda b, i: (b, 0, 0))
    return pl.pallas_call(
        functools.partial(_attn_kernel, tk=tk),
        grid=(bsz, nb),
        in_specs=[qblk] + nbr + nbr + [_resident(sink_col.shape), qblk, whole, whole, whole],
        out_specs=[qblk, qblk],
        out_shape=[jax.ShapeDtypeStruct((bsz, t_lat, width), BF16)] * 2,
        compiler_params=_params("parallel", "parallel"),
        name="attention",
    )(qw, kw, kw, kw, kw, vw, vw, vw, vw, sink_col, qg, kg, vg0, vg1)


def _block_diag_const(width, group, value):
    idx = np.arange(width) // group
    return jnp.asarray((idx[:, None] == idx[None, :]) * value, dtype=BF16)


def _head_perm():
    return np.concatenate([np.arange(0, ATT_HEAD_DIM, 2), np.arange(1, ATT_HEAD_DIM, 2)])


def _q_head_order():
    return [kv * KV_GROUP + g for g in range(KV_GROUP) for kv in range(KV_HEADS)]


def _rec_weights(w, gate_b, conv_w, w0, w2, a0, a2, k_k, k_a, r_k):
    o = np.cumsum((0,) + (ML_WIDTH,) * 4 + (ML_HEADS,) * 4 + (RW_WIDTH,) * 3
                  + (RW_DECAY_RANK,) * 2 + (RW_ICLR_RANK,) * 2 + (RW_GATE_RANK,))
    q, k, v, og = (w[:, o[i]:o[i + 1]] for i in range(4))
    gates = w[:, o[4]:o[8]]
    rkv = w[:, o[8]:o[11]]
    lora = w[:, o[11]:o[16]]
    gates_pad = jnp.pad(gates, ((0, 0), (0, LANES - N_GATES)))
    w_main = jnp.concatenate([q, v, og, rkv, lora, gates_pad], axis=1).astype(BF16)
    gb = gate_b.reshape(-1)
    gb_row = jnp.pad(gb, (0, LANES - N_GATES)).reshape(1, LANES)
    zeros = jnp.zeros((RW_DECAY_RANK, RW_WIDTH), F32)
    w2cat = jnp.concatenate([jnp.concatenate([w2[0], zeros], 1), jnp.concatenate([zeros, w2[1]], 1)], 0).astype(BF16)
    a2cat = jnp.concatenate([jnp.concatenate([a2[0], zeros], 1), jnp.concatenate([zeros, a2[1]], 1)], 0).astype(BF16)
    vecs = jnp.stack([k_k, k_a, r_k, w0[0], w0[1], a0[0], a0[1], jnp.zeros_like(k_k)])
    return dict(w_main=w_main, wkt=k.T.astype(BF16), wgt=gates.T.astype(BF16), gb_row=gb_row,
                gb_col=gb.reshape(N_GATES, 1), conv=conv_w, vecs=vecs, w2cat=w2cat, a2cat=a2cat)


def _att_weights(w, win_q_norm, win_k_norm, glb_q_norm, glb_k_norm, w_out):
    hd, perm = ATT_HEAD_DIM, _head_perm()
    qn, kn = Q_HEADS * hd, KV_HEADS * hd
    o = np.cumsum((0, qn, kn, kn, qn, kn, kn))
    qw, kw, vw, qg, kg, vg = (w[:, o[i]:o[i + 1]] for i in range(6))
    q_cols = np.concatenate([h * hd + perm for h in _q_head_order()])
    k_cols = np.concatenate([h * hd + perm for h in range(KV_HEADS)])
    w_all = jnp.concatenate([qw[:, q_cols], qg[:, q_cols], kw[:, k_cols], kg[:, k_cols], vw, vg], axis=1).astype(BF16)
    scale = hd ** -0.5
    log2e = float(np.log2(np.e))
    gains = jnp.concatenate([jnp.tile(win_q_norm[perm], Q_HEADS) * scale,
                             jnp.tile(glb_q_norm[perm], Q_HEADS) * (scale * log2e),
                             jnp.tile(win_k_norm[perm], KV_HEADS), jnp.tile(glb_k_norm[perm], KV_HEADS)]).reshape(1, -1)
    out_rows = np.concatenate([h * hd + np.arange(hd) for h in _q_head_order()])
    wo = jnp.concatenate([w_out[:qn][out_rows], w_out[qn:][out_rows]], axis=0).astype(BF16)
    return w_all, gains, wo


def _rope_tables(t_lat, s):
    n_freq = ATT_HEAD_DIM // 4
    pos = jnp.arange(t_lat)
    inv_freq = ROPE_THETA ** (-jnp.arange(n_freq, dtype=F32) / n_freq)
    ang = jnp.concatenate([(pos // GRID_W).astype(F32)[:, None] * inv_freq,
                           (pos % GRID_W).astype(F32)[:, None] * inv_freq], axis=-1)
    cos, sin = jnp.cos(ang), jnp.sin(ang)
    cos_h = jnp.concatenate([cos, cos], axis=-1)
    sin_h = jnp.concatenate([-sin, sin], axis=-1)
    pad = ((0, s - t_lat), (0, 0))
    cos_h = jnp.pad(cos_h, pad, constant_values=1.0)
    sin_h = jnp.pad(sin_h, pad)
    return jnp.tile(cos_h, (1, 2))[None], jnp.tile(sin_h, (1, 2))[None]


def kernel(x, c, ctx, c_ctx, norm_mix, norm_ffn, mod_w, mod_b, out_w, ffn_w1, ffn_w2, rec_in_w, ml_gate_b, ml_norm_g,
           rw_conv, rw_w0, rw_w2, rw_a0, rw_a2, rw_g2, rw_kk, rw_ka, rw_rk, rw_norm_g, att_in_w, win_q_norm,
           win_k_norm, win_sink, glb_q_norm, glb_k_norm):
    bsz, t_lat, d = x.shape
    lc = ctx.shape[1]
    s = t_lat + lc
    depth = mod_w.shape[0]
    assert depth == 2 and bsz + 1 <= SUBLANES
    assert t_lat % ROW_TILE == 0 and lc % ROW_TILE == 0 and t_lat % lc == 0 and t_lat % GRID_W == 0
    n_lat_tiles = t_lat // ROW_TILE

    cvec = jnp.zeros((SUBLANES, d), F32).at[:bsz].set(c).at[bsz].set(c_ctx)
    mods = _modulation(cvec, mod_w, mod_b)[:, :bsz + 1].reshape(depth, bsz + 1, 1, 6 * d)
    w1 = ffn_w1.astype(BF16)
    w2 = ffn_w2.astype(BF16)

    rw = _rec_weights(rec_in_w[0], ml_gate_b[0], rw_conv[0], rw_w0[0], rw_w2[0], rw_a0[0], rw_a2[0],
                      rw_kk[0], rw_ka[0], rw_rk[0])
    q, v, o, rkv, lora, gates, kt, gates_t = _rec_inproj(
        x, ctx, mods[0], norm_mix[0].reshape(1, d), rw["w_main"], rw["wkt"], rw["wgt"], rw["gb_row"], rw["gb_col"],
        n_lat_tiles)
    ones64 = _block_diag_const(RW_WIDTH, RW_HEAD_DIM, 1.0)
    r, vr, kk, bonus, gate, lw, key, bv = _rw_prep(rkv, lora, rw["conv"], rw["vecs"], rw["w2cat"], rw["a2cat"],
                                                   rw_g2[0].astype(BF16), ones64, t_lat)
    y_f, y_b, h_f, h_b = _rec_scan(r, vr, kk, lw, key, bv, q, v, kt, gates, gates_t, t_lat)
    consts = [norm_ffn[0].reshape(1, d), ml_norm_g[0].reshape(1, -1), rw_norm_g[0].reshape(1, -1),
              _block_diag_const(ML_WIDTH, ML_HEAD_DIM, 1.0 / ML_HEAD_DIM),
              _block_diag_const(RW_WIDTH, RW_HEAD_DIM, 1.0 / RW_HEAD_DIM),
              out_w[0].astype(BF16), w1[0], w2[0]]
    tokens = [h_f, h_b, o, y_f, y_b, bonus, gate]
    xs = _post_call(functools.partial(_rec_post_kernel, n_lat_tiles=n_lat_tiles), "rec_post", tokens, (x, ctx),
                    mods[0], consts, s, n_lat_tiles)

    w_att, qk_gain, wo = _att_weights(att_in_w[0], win_q_norm[0], win_k_norm[0], glb_q_norm[0], glb_k_norm[0], out_w[1])
    cos, sin = _rope_tables(t_lat, s)
    sq_avg = _block_diag_const(Q_HEADS * ATT_HEAD_DIM, ATT_HEAD_DIM, 1.0 / ATT_HEAD_DIM)
    qw, qg, kw, kg, vw, vg0, vg1 = _att_inproj(xs, mods[1], norm_mix[1].reshape(1, d), w_att, qk_gain, cos, sin,
                                               sq_avg, n_lat_tiles)
    sink_col = jnp.repeat(win_sink[0][np.array([kv * KV_GROUP + g for kv in range(KV_HEADS) for g in range(KV_GROUP)])],
                          Q_BLOCK).reshape(-1, 1)
    o_win, o_glb = _attention(qw, kw, vw, sink_col, qg, kg, vg0, vg1, t_lat)
    consts = [norm_ffn[1].reshape(1, d), wo, w1[1], w2[1]]
    return _post_call(_att_post_kernel, "att_post", [o_win, o_glb], (xs,), mods[1], consts, t_lat, n_lat_tiles)
```

```python
import functools

import numpy as np
import jax
import jax.numpy as jnp
from jax import lax
from jax.experimental import pallas as pl
from jax.experimental.pallas import tpu as pltpu

F32 = jnp.float32
BF16 = jnp.bfloat16

GRID_W = 64
NORM_EPS = 1e-6
GN_EPS = 64e-5
ROPE_THETA = 10000.0
ML_HEAD_DIM = 128
ML_HEADS = 4
ML_WIDTH = ML_HEADS * ML_HEAD_DIM
ML_CHUNK = 128
RW_HEAD_DIM = 64
RW_HEADS = 8
RW_WIDTH = RW_HEADS * RW_HEAD_DIM
RW_CHUNK = 64
RW_DECAY_RANK = 64
RW_ICLR_RANK = 64
RW_GATE_RANK = 128
ATT_HEAD_DIM = 64
Q_HEADS = 8
KV_HEADS = 2
KV_GROUP = Q_HEADS // KV_HEADS
KV_WIDTH = KV_HEADS * ATT_HEAD_DIM
WINDOW = 128
Q_BLOCK = 128

LANES = 128
SUBLANES = 8
VMEM_LIMIT_BYTES = 56 * 1024 * 1024
ROW_TILE = 256
HALO_ROWS = 16
MASK_VALUE = -1e30


def _params(*semantics):
    return pltpu.CompilerParams(dimension_semantics=semantics, vmem_limit_bytes=VMEM_LIMIT_BYTES)


def _resident(shape):
    zeros = (0,) * len(shape)
    return pl.BlockSpec(shape, lambda *_: zeros, pipeline_mode=pl.Buffered(1))


def _dot(a, b):
    return jnp.dot(a, b, preferred_element_type=F32)


def _dot_nt(a, b):
    return lax.dot_general(a, b, (((1,), (1,)), ((), ())), preferred_element_type=F32)


def _dot_tn(a, b):
    return lax.dot_general(a, b, (((0,), (0,)), ((), ())), preferred_element_type=F32)


def _split(x, n):
    parts, rest = [], x
    for _ in range(n):
        p = rest.astype(BF16)
        parts.append(p)
        rest = rest - p.astype(F32)
    return parts


def _dot_split_lhs(x, m, n):
    return sum(_dot(p, m) for p in _split(x, n))


def _dot_split_rhs(m, x, n):
    return sum(_dot(m, p) for p in _split(x, n))


def _dot_f32(a, b):
    ah, al = _split(a, 2)
    bh, bl = _split(b, 2)
    return _dot(ah, bh) + _dot(ah, bl) + _dot(al, bh)


def _log_sigmoid(x):
    return jnp.minimum(x, 0.0) - jnp.log1p(jnp.exp(-jnp.abs(x)))


def _norm_mod(x, gain, shift, scale):
    y = x * lax.rsqrt(jnp.mean(x * x, axis=-1, keepdims=True) + NORM_EPS)
    return (y * gain) * (1.0 + scale) + shift


def _group_layernorm(y, avg, gain):
    yc = y - _dot(y.astype(BF16), avg)
    var = _dot((yc * yc).astype(BF16), avg)
    return yc * lax.rsqrt(var + GN_EPS) * gain


def _mod_kernel(cv_ref, w_ref, b_ref, o_ref):
    cv = cv_ref[...]
    o_ref[0] = _dot_f32(cv * jax.nn.sigmoid(cv), w_ref[0]) + b_ref[0]


def _modulation(cvec, mod_w, mod_b):
    depth, d, n = mod_w.shape
    tn = n // 4
    return pl.pallas_call(
        _mod_kernel,
        grid=(depth, n // tn),
        in_specs=[pl.BlockSpec((SUBLANES, d), lambda l, j: (0, 0)),
                  pl.BlockSpec((1, d, tn), lambda l, j: (l, 0, j)),
                  pl.BlockSpec((1, 1, tn), lambda l, j: (l, 0, j))],
        out_specs=pl.BlockSpec((1, SUBLANES, tn), lambda l, j: (l, 0, j)),
        out_shape=jax.ShapeDtypeStruct((depth, SUBLANES, n), F32),
        compiler_params=_params("parallel", "parallel"),
        name="adaln_modulation",
    )(cvec, mod_w, mod_b.reshape(depth, 1, n))


REC_Q, REC_V, REC_O, REC_RKV, REC_LORA, REC_GATE, REC_END = 0, 512, 1024, 1536, 3072, 3456, 3584
N_GATES = 4 * ML_HEADS


def _token_rows(x_ref, ctx_ref, n_lat_tiles):
    return jnp.where(pl.program_id(1) >= n_lat_tiles, ctx_ref[0], x_ref[0])


def _token_specs(tm, d, n_lat_tiles):
    return [pl.BlockSpec((1, tm, d), lambda b, i: (b, jnp.minimum(i, n_lat_tiles - 1), 0)),
            pl.BlockSpec((1, tm, d), lambda b, i: (b, jnp.maximum(i - n_lat_tiles, 0), 0))]


def _rec_inproj_kernel(x_ref, ctx_ref, mod_ref, g_ref, w_ref, wkt_ref, wgt_ref, gb_ref, gbt_ref,
                       q_ref, v_ref, o_ref, rkv_ref, lora_ref, gate_ref, kt_ref, gatet_ref, *, n_lat_tiles):
    d = x_ref.shape[-1]
    mod = mod_ref[0]
    h = _norm_mod(_token_rows(x_ref, ctx_ref, n_lat_tiles), g_ref[...], mod[:, 0:d], mod[:, d:2 * d]).astype(BF16)
    q_ref[0] = _dot(h, w_ref[:, REC_Q:REC_V]).astype(BF16)
    v_ref[0] = _dot(h, w_ref[:, REC_V:REC_O]).astype(BF16)
    o_ref[0] = _dot(h, w_ref[:, REC_O:REC_RKV]).astype(BF16)
    rkv_ref[0] = _dot(h, w_ref[:, REC_RKV:REC_LORA]).astype(BF16)
    lora_ref[0] = _dot(h, w_ref[:, REC_LORA:REC_GATE])
    gate_ref[0] = _dot(h, w_ref[:, REC_GATE:REC_END]) + gb_ref[...]
    kt_ref[0] = _dot_nt(wkt_ref[...], h).astype(BF16)
    gatet_ref[0] = _dot_nt(wgt_ref[...], h) + gbt_ref[...]


def _rec_inproj(x, ctx, mods, gain, w_main, wkt, wgt, gb_row, gb_col, n_lat_tiles):
    bsz, _, d = x.shape
    s = x.shape[1] + ctx.shape[1]
    tm = ROW_TILE
    tok = lambda c: pl.BlockSpec((1, tm, c), lambda b, i: (b, i, 0))
    mod_row = lambda b, i: (jnp.where(i >= n_lat_tiles, bsz, b), 0, 0)
    return pl.pallas_call(
        functools.partial(_rec_inproj_kernel, n_lat_tiles=n_lat_tiles),
        grid=(bsz, s // tm),
        in_specs=_token_specs(tm, d, n_lat_tiles) + [
                  pl.BlockSpec((1, 1, mods.shape[-1]), mod_row),
                  _resident((1, d)), _resident(w_main.shape), _resident(wkt.shape), _resident(wgt.shape),
                  _resident(gb_row.shape), _resident(gb_col.shape)],
        out_specs=[tok(ML_WIDTH), tok(ML_WIDTH), tok(ML_WIDTH), tok(3 * RW_WIDTH), tok(REC_GATE - REC_LORA),
                   tok(LANES),
                   pl.BlockSpec((1, ML_WIDTH, tm), lambda b, i: (b, 0, i)),
                   pl.BlockSpec((1, N_GATES, tm), lambda b, i: (b, 0, i))],
        out_shape=[jax.ShapeDtypeStruct((bsz, s, ML_WIDTH), BF16),
                   jax.ShapeDtypeStruct((bsz, s, ML_WIDTH), BF16),
                   jax.ShapeDtypeStruct((bsz, s, ML_WIDTH), BF16),
                   jax.ShapeDtypeStruct((bsz, s, 3 * RW_WIDTH), BF16),
                   jax.ShapeDtypeStruct((bsz, s, REC_GATE - REC_LORA), F32),
                   jax.ShapeDtypeStruct((bsz, s, LANES), F32),
                   jax.ShapeDtypeStruct((bsz, ML_WIDTH, s), BF16),
                   jax.ShapeDtypeStruct((bsz, N_GATES, s), F32)],
        compiler_params=_params("parallel", "parallel"),
        name="rec_inproj",
    )(x, ctx, mods, gain, w_main, wkt, wgt, gb_row, gb_col)


def _mlstm_stages(qf_ref, vf_ref, ktf_ref, gf_ref, gtf_ref, qb_ref, vb_ref, ktb_ref, gb_ref, gtb_ref,
                  hf_ref, hb_ref, c_ref, m_ref):
    L, dh = ML_CHUNK, ML_HEAD_DIM
    t_idx = lax.broadcasted_iota(jnp.int32, (L, L), 0)
    s_idx = lax.broadcasted_iota(jnp.int32, (L, L), 1)
    ones_col = (lax.broadcasted_iota(jnp.int32, (L, LANES), 1) == 0).astype(BF16)
    scale = dh ** -0.5
    streams = ((qf_ref, vf_ref, ktf_ref, gf_ref, gtf_ref, hf_ref),
               (qb_ref, vb_ref, ktb_ref, gb_ref, gtb_ref, hb_ref))
    chains = []
    for d, (q_ref, v_ref, kt_ref, g_ref, gt_ref, h_ref) in enumerate(streams):
        earlier = (s_idx <= t_idx) if d == 0 else (s_idx >= t_idx)
        tri = earlier.astype(BF16)
        tri_t = ((t_idx <= s_idx) if d == 0 else (t_idx >= s_idx)).astype(BF16)
        gt = gt_ref[0]
        lf_rows = _log_sigmoid(gt)
        b_cols = _dot_split_rhs(tri, _log_sigmoid(g_ref[0]), 3)
        b_rows = _dot_split_lhs(lf_rows, tri_t, 3)
        tot = jnp.sum(lf_rows, axis=1, keepdims=True)
        q, v, kt = q_ref[0], v_ref[0], kt_ref[0]
        for hh in range(ML_HEADS):
            ic, fc = 2 * ML_HEADS * d + hh, 2 * ML_HEADS * d + ML_HEADS + hh
            chains.append(dict(j=d * ML_HEADS + hh, hh=hh, h_ref=h_ref, earlier=earlier,
                               b_col=b_cols[:, fc:fc + 1], b_row=b_rows[fc:fc + 1, :], i_row=gt[ic:ic + 1, :],
                               b_end=tot[fc:fc + 1, :], q=q[:, hh * dh:(hh + 1) * dh], kt=kt[hh * dh:(hh + 1) * dh, :],
                               v_aug=jnp.concatenate([v[:, hh * dh:(hh + 1) * dh], ones_col], axis=1)))
    yield
    for ch in chains:
        ch["cst"] = c_ref[ch["j"]]
        ch["qk"] = _dot(ch["q"], ch["kt"])
        ch["qc"] = _dot(ch["q"], ch["cst"].astype(BF16))
    yield
    for ch in chains:
        m_prev = m_ref[ch["j"]:ch["j"] + 1, 0:1]
        d_in = jnp.where(ch["earlier"], ch["b_col"] - ch["b_row"] + ch["i_row"], MASK_VALUE)
        d_prev = ch["b_col"] + m_prev
        m_t = jnp.maximum(d_prev, jnp.max(d_in, axis=1, keepdims=True))
        ch["s"] = (ch["qk"] * (scale * jnp.exp(d_in - m_t))).astype(BF16)
        ch["w_prev"], ch["floor"] = jnp.exp(d_prev - m_t), jnp.exp(-m_t)
        d_end = ch["b_end"] - ch["b_row"] + ch["i_row"]
        m_new = jnp.maximum(ch["b_end"] + m_prev, jnp.max(d_end, axis=1, keepdims=True))
        ch["kw"] = (ch["kt"].astype(F32) * (jnp.exp(d_end - m_new) * scale)).astype(BF16)
        ch["decay"] = jnp.exp(ch["b_end"] + m_prev - m_new)
        m_ref[ch["j"]:ch["j"] + 1, :] = jnp.broadcast_to(m_new, (1, LANES))
    yield
    for ch in chains:
        acc = _dot(ch["s"], ch["v_aug"]) + ch["w_prev"] * ch["qc"]
        den = jnp.maximum(jnp.abs(acc[:, dh:dh + 1]), ch["floor"])
        ch["h_ref"][0, :, ch["hh"] * dh:(ch["hh"] + 1) * dh] = acc[:, :dh] / den
    yield
    for ch in chains:
        c_ref[ch["j"]] = ch["decay"] * ch["cst"] + _dot(ch["kw"], ch["v_aug"])


def _chunk_order(n_lat, n_ctx):
    n = n_lat + n_ctx
    fwd = lambda c: (c + n_lat) % n
    bwd = lambda c: n - 1 - c
    return fwd, bwd


def _rw_prep_kernel(z_ref, zp_ref, zn_ref, lora_ref, conv_ref, vec_ref, w2_ref, a2_ref, g2_ref, ones_ref,
                    r_ref, v_ref, kk_ref, bonus_ref, gate_ref, lw_ref, key_ref, bv_ref, *, t_lat, s_tot):
    tm, w = z_ref.shape[1], RW_WIDTH
    row0 = pl.program_id(1) * tm
    has_prev = jnp.where(jnp.logical_and(row0 != 0, row0 != t_lat), 1.0, 0.0)
    has_next = jnp.where(jnp.logical_and(row0 + tm != t_lat, row0 + tm != s_tot), 1.0, 0.0)
    z = z_ref[0].astype(F32)
    prev_row = zp_ref[0][HALO_ROWS - 1:HALO_ROWS, :].astype(F32) * has_prev
    next_row = zn_ref[0][0:1, :].astype(F32) * has_next
    ridx = lax.broadcasted_iota(jnp.int32, z.shape, 0)
    z_prev = jnp.where(ridx == 0, prev_row, pltpu.roll(z, 1, 0))
    z_next = jnp.where(ridx == tm - 1, next_row, pltpu.roll(z, tm - 1, 0))
    cw = conv_ref[...]
    zc = cw[0:1] * z_prev + cw[1:2] * z + cw[2:3] * z_next
    r, kr, vr = zc[:, :w], zc[:, w:2 * w], zc[:, 2 * w:]
    vec = vec_ref[...]
    k_k, k_a, r_k = vec[0:1], vec[1:2], vec[2:3]
    ones = ones_ref[...]
    kkr = kr * k_k
    kk = kkr * lax.rsqrt(_dot_split_lhs(kkr * kkr, ones, 2) + NORM_EPS)
    r_ref[0] = r.astype(BF16)
    v_ref[0] = vr.astype(BF16)
    kk_ref[0] = kk.astype(BF16)
    bonus_ref[0] = (_dot_split_lhs(r * kr * r_k, ones, 2) * vr).astype(BF16)
    lora = lora_ref[0]
    dec = _dot(jnp.tanh(lora[:, 0:LANES]).astype(BF16), w2_ref[...])
    icl = _dot(lora[:, LANES:2 * LANES].astype(BF16), a2_ref[...])
    gate_ref[0] = _dot(jax.nn.sigmoid(lora[:, 2 * LANES:3 * LANES]).astype(BF16), g2_ref[...]).astype(BF16)
    for d in range(2):
        w_log = _log_sigmoid(vec[3 + d:4 + d] + dec[:, d * w:(d + 1) * w]) - 0.5
        a = jax.nn.sigmoid(vec[5 + d:6 + d] + icl[:, d * w:(d + 1) * w])
        lw_ref[d, 0] = -jnp.exp(w_log)
        key_ref[d, 0] = (kr * (1.0 + (a - 1.0) * k_a)).astype(BF16)
        bv_ref[d, 0] = (kk * a).astype(BF16)


def _rw_prep(rkv, lora, conv_w, vecs, w2cat, a2cat, g2, ones64, t_lat):
    bsz, s, _ = rkv.shape
    tm, w = ROW_TILE, RW_WIDTH
    nsub = tm // HALO_ROWS
    tok = lambda c: pl.BlockSpec((1, tm, c), lambda b, i: (b, i, 0))
    dirtok = pl.BlockSpec((2, 1, tm, w), lambda b, i: (0, b, i, 0))
    last = s // HALO_ROWS - 1
    return pl.pallas_call(
        functools.partial(_rw_prep_kernel, t_lat=t_lat, s_tot=s),
        grid=(bsz, s // tm),
        in_specs=[tok(3 * w),
                  pl.BlockSpec((1, HALO_ROWS, 3 * w), lambda b, i: (b, jnp.maximum(i * nsub - 1, 0), 0)),
                  pl.BlockSpec((1, HALO_ROWS, 3 * w), lambda b, i: (b, jnp.minimum((i + 1) * nsub, last), 0)),
                  tok(lora.shape[-1]),
                  _resident(conv_w.shape), _resident(vecs.shape), _resident(w2cat.shape), _resident(a2cat.shape),
                  _resident(g2.shape), _resident(ones64.shape)],
        out_specs=[tok(w)] * 5 + [dirtok] * 3,
        out_shape=[jax.ShapeDtypeStruct((bsz, s, w), BF16)] * 5
        + [jax.ShapeDtypeStruct((2, bsz, s, w), dt) for dt in (F32, BF16, BF16)],
        compiler_params=_params("parallel", "parallel"),
        name="rwkv_prep",
    )(rkv, rkv, rkv, lora, conv_w, vecs, w2cat, a2cat, g2, ones64)


def _rw_scan_stages(rf_ref, vf_ref, kkf_ref, lwf_ref, keyf_ref, bvf_ref, rb_ref, vb_ref, kkb_ref, lwb_ref, keyb_ref,
                    bvb_ref, yf_ref, yb_ref, s_ref, step):
    C, n = RW_CHUNK, RW_HEAD_DIM
    t2 = lax.broadcasted_iota(jnp.int32, (2 * C, 2 * C), 0) % C
    s2 = lax.broadcasted_iota(jnp.int32, (2 * C, 2 * C), 1) % C
    lower = lax.broadcasted_iota(jnp.int32, (2 * C, 2 * C), 0) >= C

    chains = []
    rows_of = (slice(step * C, (step + 1) * C), slice((1 - step) * C, (2 - step) * C))
    for d, (r_ref, v_ref, kk_ref, lw_ref, key_ref, bv_ref) in enumerate(
            ((rf_ref, vf_ref, kkf_ref, lwf_ref, keyf_ref, bvf_ref), (rb_ref, vb_ref, kkb_ref, lwb_ref, keyb_ref, bvb_ref))):
        rows = rows_of[d]
        rel = (s2 - t2) if d == 0 else (t2 - s2)
        keep = rel < jnp.where(lower, 1, 0)
        tri = (rel[:C, :C] <= 0).astype(BF16)
        lw = lw_ref[0, 0, rows, :]
        cum = _dot_split_rhs(tri, lw, 3)
        tot = jnp.sum(lw, axis=0, keepdims=True)
        r, v, kk, key, bv = (a.astype(F32) for a in (r_ref[0, rows, :], v_ref[0, rows, :], kk_ref[0, rows, :],
                                                     key_ref[0, 0, rows, :], bv_ref[0, 0, rows, :]))
        e_neg = jnp.exp(-cum)
        e_end = jnp.exp(tot - cum)
        alpha = (kk * jnp.exp(cum - lw)).astype(BF16)
        rho = (r * jnp.exp(cum)).astype(BF16)
        beta = (bv * e_neg).astype(BF16)
        kappa = (key * e_neg).astype(BF16)
        beta_end = (bv * e_end).astype(BF16)
        kappa_end = (key * e_end).astype(BF16)
        gamma = jnp.exp(tot)
        vb = v.astype(BF16)
        for h in range(RW_HEADS):
            sl = slice(h * n, (h + 1) * n)
            chains.append(dict(j=d * RW_HEADS + h, keep=keep, a=alpha[:, sl], rho=rho[:, sl], v=vb[:, sl],
                               lhs=jnp.concatenate([alpha[:, sl], rho[:, sl]], axis=0),
                               rhs=jnp.concatenate([beta[:, sl], kappa[:, sl]], axis=0),
                               end=jnp.concatenate([beta_end[:, sl], kappa_end[:, sl]], axis=0),
                               gamma=gamma[:, sl]))
    yield
    for ch in chains:
        ch["big"] = jnp.where(ch["keep"], _dot_nt(ch["lhs"], ch["rhs"]), 0.0).astype(BF16)
    yield
    for ch in chains:
        ch["lmv"] = _dot(ch["big"][:, C:], ch["v"])
    yield
    tt = lax.broadcasted_iota(jnp.int32, (C, C), 0)
    ss = lax.broadcasted_iota(jnp.int32, (C, C), 1)
    eye = (tt == ss).astype(F32)
    for ch in chains:
        ch["lab"] = ch["big"][:C, :C]
        ch["inv"] = eye - jnp.where((tt >> 1) == (ss >> 1), ch["lab"], 0.0).astype(F32)
    for k in range(1, 6):
        couples = jnp.logical_and((tt >> (k + 1)) == (ss >> (k + 1)), (tt >> k) != (ss >> k))
        for ch in chains:
            ch["ed"] = _dot(jnp.where(couples, ch["lab"], 0.0).astype(BF16), ch["inv"].astype(BF16)).astype(BF16)
        yield
        for ch in chains:
            ch["inv"] = ch["inv"] - _dot(ch["inv"].astype(BF16), ch["ed"])
        yield
    for ch in chains:
        x = -jnp.concatenate([ch["a"].astype(F32), ch["lmv"][:C]], axis=1)
        ch["x"] = _dot(ch["inv"].astype(BF16), x.astype(BF16))
    yield
    for ch in chains:
        ch["st"] = s_ref[ch["j"]]
        ch["ws"] = _dot_nt(jnp.concatenate([ch["x"][:, :n].astype(BF16), ch["rho"]], axis=0), ch["st"].astype(BF16))
    yield
    for ch in chains:
        ch["u"] = (ch["ws"][:C] + ch["x"][:, n:]).astype(BF16)
    for ch in chains:
        ch["y"] = ch["ws"][C:] + _dot(ch["big"][C:, :C], ch["u"]) + ch["lmv"][C:]
    yield
    for ch in chains:
        upd = _dot_tn(jnp.concatenate([ch["u"], ch["v"]], axis=0), ch["end"])
        s_ref[ch["j"]] = ch["st"] * ch["gamma"] + upd
    yf_ref[0, rows_of[0], :] = jnp.concatenate([ch["y"] for ch in chains[:RW_HEADS]], axis=1)
    yb_ref[0, rows_of[1], :] = jnp.concatenate([ch["y"] for ch in chains[RW_HEADS:]], axis=1)


RW_STAGES_PER_ML_STAGE = 4


def _rec_scan_kernel(*refs):
    rw_in, ml_in = refs[:12], refs[12:22]
    yf_ref, yb_ref, hf_ref, hb_ref, s_ref, c_ref, m_ref = refs[22:]

    @pl.when(pl.program_id(1) == 0)
    def _():
        s_ref[...] = jnp.zeros_like(s_ref)
        c_ref[...] = jnp.zeros_like(c_ref)
        m_ref[...] = jnp.zeros_like(m_ref)

    ml = _mlstm_stages(*ml_in, hf_ref, hb_ref, c_ref, m_ref)
    n_stage = 0
    for step in range(ML_CHUNK // RW_CHUNK):
        for _ in _rw_scan_stages(*rw_in, yf_ref, yb_ref, s_ref, step):
            n_stage += 1
            if n_stage % RW_STAGES_PER_ML_STAGE == 0:
                next(ml, None)
    for _ in ml:
        pass


def _rec_scan(r, v, kk, lw, key, bv, q, vm, kt, gates, gates_t, t_lat):
    bsz, s, w = r.shape
    L = ML_CHUNK
    orders = _chunk_order(t_lat // L, (s - t_lat) // L)
    rw_specs, ml_specs, out_rw, out_ml = [], [], [], []
    for d, order in enumerate(orders):
        tok = lambda c, order=order: pl.BlockSpec((1, L, c), lambda b, i: (b, order(i), 0))
        dirtok = pl.BlockSpec((1, 1, L, w), lambda b, i, order=order, d=d: (d, b, order(i), 0))
        timelast = lambda c, order=order: pl.BlockSpec((1, c, L), lambda b, i: (b, 0, order(i)))
        rw_specs += [tok(w), tok(w), tok(w), dirtok, dirtok, dirtok]
        ml_specs += [tok(ML_WIDTH), tok(ML_WIDTH), timelast(ML_WIDTH), tok(LANES), timelast(N_GATES)]
        out_rw.append(tok(w))
        out_ml.append(tok(ML_WIDTH))
    return pl.pallas_call(
        _rec_scan_kernel,
        grid=(bsz, s // L),
        in_specs=rw_specs + ml_specs,
        out_specs=out_rw + out_ml,
        out_shape=[jax.ShapeDtypeStruct((bsz, s, w), F32)] * 2 + [jax.ShapeDtypeStruct((bsz, s, ML_WIDTH), F32)] * 2,
        scratch_shapes=[pltpu.VMEM((2 * RW_HEADS, RW_HEAD_DIM, RW_HEAD_DIM), F32),
                        pltpu.VMEM((2 * ML_HEADS, ML_HEAD_DIM, ML_HEAD_DIM + LANES), F32),
                        pltpu.VMEM((2 * ML_HEADS, LANES), F32)],
        compiler_params=_params("parallel", "arbitrary"),
        name="recurrent_scan",
    )(r, v, kk, lw, key, bv, r, v, kk, lw, key, bv, q, vm, kt, gates, gates_t, q, vm, kt, gates, gates_t)


def _residual_mlp(mix, x, mod, g_ffn, wo_ref, w1_ref, w2_ref, out_ref):
    d = x.shape[-1]
    x1 = x + mod[:, 2 * d:3 * d] * _dot(mix, wo_ref[...])
    h2 = _norm_mod(x1, g_ffn, mod[:, 3 * d:4 * d], mod[:, 4 * d:5 * d]).astype(BF16)
    acc = jnp.zeros_like(x1)
    for k in range(w1_ref.shape[1] // d):
        hid = jnp.maximum(_dot(h2, w1_ref[:, k * d:(k + 1) * d]), 0.0)
        acc = acc + _dot((hid * hid).astype(BF16), w2_ref[k * d:(k + 1) * d, :])
    out_ref[0] = x1 + mod[:, 5 * d:6 * d] * acc


def _rec_post_kernel(hf_ref, hb_ref, o_ref, yf_ref, yb_ref, bonus_ref, gate_ref, x_ref, ctx_ref, mod_ref, gf_ref,
                     mlg_ref, rwg_ref, avg_ml_ref, avg_rw_ref, wo_ref, w1_ref, w2_ref, out_ref, *, n_lat_tiles):
    ml = _group_layernorm(hf_ref[0] + hb_ref[0], avg_ml_ref[...], mlg_ref[...]) * jax.nn.sigmoid(o_ref[0].astype(F32))
    rw = (_group_layernorm(yf_ref[0] + yb_ref[0], avg_rw_ref[...], rwg_ref[...]) + bonus_ref[0]) * gate_ref[0]
    mix = jnp.concatenate([ml, rw], axis=1).astype(BF16)
    x = _token_rows(x_ref, ctx_ref, n_lat_tiles)
    _residual_mlp(mix, x, mod_ref[0], gf_ref[...], wo_ref, w1_ref, w2_ref, out_ref)


def _att_post_kernel(ow_ref, og_ref, x_ref, mod_ref, gf_ref, wo_ref, w1_ref, w2_ref, out_ref):
    mix = jnp.concatenate([ow_ref[0], og_ref[0]], axis=1)
    _residual_mlp(mix, x_ref[0], mod_ref[0], gf_ref[...], wo_ref, w1_ref, w2_ref, out_ref)


def _post_call(kernel, name, token_inputs, residual, mods, consts, n_rows, n_lat_tiles):
    bsz, _, d = residual[0].shape
    tm = ROW_TILE
    mod_row = lambda b, i: (jnp.where(i >= n_lat_tiles, bsz, b), 0, 0)
    tok = lambda c: pl.BlockSpec((1, tm, c), lambda b, i: (b, i, 0))
    in_specs = [tok(arr.shape[-1]) for arr in token_inputs]
    in_specs += _token_specs(tm, d, n_lat_tiles) if len(residual) == 2 else [tok(d)]
    in_specs += [pl.BlockSpec((1, 1, mods.shape[-1]), mod_row)]
    in_specs += [_resident(c.shape) for c in consts]
    args = list(token_inputs) + list(residual)
    return pl.pallas_call(
        kernel,
        grid=(bsz, n_rows // tm),
        in_specs=in_specs,
        out_specs=pl.BlockSpec((1, tm, d), lambda b, i: (b, i, 0)),
        out_shape=jax.ShapeDtypeStruct((bsz, n_rows, d), F32),
        compiler_params=_params("parallel", "parallel"),
        name=name,
    )(*args, mods, *consts)


ATT_QW, ATT_QG, ATT_KW, ATT_KG, ATT_VW, ATT_VG, ATT_END = 0, 512, 1024, 1152, 1280, 1408, 1536
ATT_QK_END = ATT_VW


def _att_inproj_kernel(x_ref, mod_ref, g_ref, w_ref, qkg_ref, cos_ref, sin_ref, sq_ref,
                       qw_ref, qg_ref, kw_ref, kg_ref, vw_ref, vg0_ref, vg1_ref):
    d = x_ref.shape[-1]
    mod = mod_ref[0]
    h = _norm_mod(x_ref[0], g_ref[...], mod[:, 0:d], mod[:, d:2 * d]).astype(BF16)
    half = ATT_HEAD_DIM // 2
    cos2, sin2 = cos_ref[0], sin_ref[0]

    def norm_rope(lo, hi):
        z = _dot(h, w_ref[:, lo:hi])
        n = hi - lo
        sq = sq_ref[...]
        z = z * lax.rsqrt(_dot((z * z).astype(BF16), sq[:n, :n]) + NORM_EPS) * qkg_ref[:, lo:hi]
        reps = n // LANES
        cos = jnp.concatenate([cos2] * reps, axis=1)
        sin = jnp.concatenate([sin2] * reps, axis=1)
        lane = lax.broadcasted_iota(jnp.int32, z.shape, 1)
        partner = jnp.where(lane % ATT_HEAD_DIM < half, pltpu.roll(z, n - half, 1), pltpu.roll(z, half, 1))
        return (z * cos + partner * sin).astype(BF16)

    qw_ref[0] = norm_rope(ATT_QW, ATT_QG)
    qg_ref[0] = norm_rope(ATT_QG, ATT_KW)
    kk = norm_rope(ATT_KW, ATT_VW)
    kw_ref[0] = kk[:, :KV_WIDTH]
    kg_ref[0] = kk[:, KV_WIDTH:]
    vv = _dot(h, w_ref[:, ATT_VW:ATT_END])
    vw_ref[0] = vv[:, :KV_WIDTH].astype(BF16)
    vg = vv[:, KV_WIDTH:]
    lane = lax.broadcasted_iota(jnp.int32, vg.shape, 1)
    vg0_ref[0] = jnp.where(lane < ATT_HEAD_DIM, vg, jnp.where(lane == ATT_HEAD_DIM, 1.0, 0.0)).astype(BF16)
    vg1_ref[0] = jnp.where(lane >= ATT_HEAD_DIM, vg, jnp.where(lane == 0, 1.0, 0.0)).astype(BF16)


def _att_inproj(xs, mods, gain, w, qk_gain, cos, sin, sq_avg, n_lat_tiles):
    bsz, s, d = xs.shape
    tm = ROW_TILE
    tok = lambda c: pl.BlockSpec((1, tm, c), lambda b, i: (b, i, 0))
    mod_row = lambda b, i: (jnp.where(i >= n_lat_tiles, bsz, b), 0, 0)
    rope = pl.BlockSpec((1, tm, LANES), lambda b, i: (0, i, 0))
    qw = Q_HEADS * ATT_HEAD_DIM
    return pl.pallas_call(
        _att_inproj_kernel,
        grid=(bsz, s // tm),
        in_specs=[tok(d), pl.BlockSpec((1, 1, mods.shape[-1]), mod_row), _resident((1, d)), _resident(w.shape),
                  _resident(qk_gain.shape), rope, rope, _resident(sq_avg.shape)],
        out_specs=[tok(qw), tok(qw)] + [tok(KV_WIDTH)] * 5,
        out_shape=[jax.ShapeDtypeStruct((bsz, s, c), BF16) for c in (qw, qw) + (KV_WIDTH,) * 5],
        compiler_params=_params("parallel", "parallel"),
        name="att_inproj",
    )(xs, mods, gain, w, qk_gain, cos, sin, sq_avg)


def _stack_queries(q):
    lane = lax.broadcasted_iota(jnp.int32, (q.shape[0], LANES), 1)
    zero = jnp.zeros((q.shape[0], LANES), q.dtype)
    slabs = [q[:, g * LANES:(g + 1) * LANES] for g in range(KV_GROUP)]
    lo = [jnp.where(lane < ATT_HEAD_DIM, sl, zero) for sl in slabs]
    hi = [jnp.where(lane >= ATT_HEAD_DIM, sl, zero) for sl in slabs]
    return jnp.concatenate(lo + hi, axis=0)


def _unstack_outputs(o, rows):
    lane = lax.broadcasted_iota(jnp.int32, (rows, LANES), 1)
    slabs = [jnp.where(lane < ATT_HEAD_DIM, o[g * rows:(g + 1) * rows], o[(KV_GROUP + g) * rows:(KV_GROUP + g + 1) * rows])
             for g in range(KV_GROUP)]
    return jnp.concatenate(slabs, axis=1)


def _attn_kernel(qw_ref, kp_ref, ko_ref, kn_ref, kc_ref, vp_ref, vo_ref, vn_ref, vc_ref, sink_ref,
                 qg_ref, k_ref, v0_ref, v1_ref, ow_ref, og_ref, *, tk):
    L = Q_BLOCK
    i, nb = pl.program_id(1), pl.num_programs(1)
    lc = kc_ref.shape[1]
    qs_w = _stack_queries(qw_ref[0])
    keys = jnp.concatenate([kp_ref[0], ko_ref[0], kn_ref[0], kc_ref[0]], axis=0)
    vals = jnp.concatenate([vp_ref[0], vo_ref[0], vn_ref[0], vc_ref[0]], axis=0)
    s_w = _dot_nt(qs_w, keys)

    qs = _stack_queries(qg_ref[0])
    rows = qs.shape[0]
    half = rows // 2
    n_tiles = k_ref.shape[1] // tk
    scores = lambda j: _dot_nt(qs, k_ref[0, j * tk:(j + 1) * tk, :])
    s_next = scores(0)

    nk = 3 * L + lc
    row = lax.broadcasted_iota(jnp.int32, (L, nk), 0)
    col = lax.broadcasted_iota(jnp.int32, (L, nk), 1)
    off = col - L - row
    valid = jnp.logical_and(off <= WINDOW, off >= -WINDOW)
    valid = jnp.logical_and(valid, col >= jnp.where(i > 0, 0, L))
    valid = jnp.logical_and(valid, col < jnp.where(i < nb - 1, 3 * L, 2 * L))
    valid = jnp.logical_or(valid, col >= 3 * L)
    valid = jnp.concatenate([valid] * (2 * KV_GROUP), axis=0)
    s_w = jnp.where(valid, s_w, MASK_VALUE)
    sink = sink_ref[...]
    m_w = jnp.maximum(jnp.max(s_w, axis=1, keepdims=True), sink)
    p_w = jnp.exp(s_w - m_w)
    den_w = jnp.sum(p_w, axis=1, keepdims=True) + jnp.exp(sink - m_w)
    ow_ref[0] = _unstack_outputs(_dot(p_w.astype(BF16), vals) / den_w, L).astype(BF16)

    m = jnp.full((rows, 1), MASK_VALUE, F32)
    acc = jnp.zeros((rows, LANES), F32)
    for j in range(n_tiles):
        s = s_next
        if j + 1 < n_tiles:
            s_next = scores(j + 1)
        m_new = jnp.maximum(m, jnp.max(s, axis=1, keepdims=True))
        p = jnp.exp2(s - m_new).astype(BF16)
        pv = jnp.concatenate([_dot(p[:half], v0_ref[0, j * tk:(j + 1) * tk, :]),
                              _dot(p[half:], v1_ref[0, j * tk:(j + 1) * tk, :])], axis=0)
        acc = jnp.exp2(m - m_new) * acc + pv
        m = m_new
    den = jnp.concatenate([acc[:half, ATT_HEAD_DIM:ATT_HEAD_DIM + 1], acc[half:, 0:1]], axis=0)
    og_ref[0] = _unstack_outputs(acc / den, L).astype(BF16)


GLB_KEY_TILES = (2816, 1408, 768, 512, 384, 256, 128)


def _attention(qw, kw, vw, sink_col, qg, kg, vg0, vg1, t_lat):
    bsz, s, width = qw.shape
    L = Q_BLOCK
    nb = t_lat // L
    lc = s - t_lat
    tk = next(t for t in GLB_KEY_TILES if s % t == 0)
    qblk = pl.BlockSpec((1, L, width), lambda b, i: (b, i, 0))
    kv = lambda f: pl.BlockSpec((1, L, KV_WIDTH), lambda b, i: (b, f(i), 0))
    ctx = pl.BlockSpec((1, lc, KV_WIDTH), lambda b, i: (b, t_lat // lc, 0))
    nbr = [kv(lambda i: jnp.maximum(i - 1, 0)), kv(lambda i: i), kv(lambda i: jnp.minimum(i + 1, nb - 1)), ctx]
    whole = pl.BlockSpec((1, s, KV_WIDTH), lambda b, i: (b, 0, 0))
    return pl.pallas_call(
        functools.partial(_attn_kernel, tk=tk),
        grid=(bsz, nb),
        in_specs=[qblk] + nbr + nbr + [_resident(sink_col.shape), qblk, whole, whole, whole],
        out_specs=[qblk, qblk],
        out_shape=[jax.ShapeDtypeStruct((bsz, t_lat, width), BF16)] * 2,
        compiler_params=_params("parallel", "parallel"),
        name="attention",
    )(qw, kw, kw, kw, kw, vw, vw, vw, vw, sink_col, qg, kg, vg0, vg1)


def _block_diag_const(width, group, value):
    idx = np.arange(width) // group
    return jnp.asarray((idx[:, None] == idx[None, :]) * value, dtype=BF16)


def _head_perm():
    return np.concatenate([np.arange(0, ATT_HEAD_DIM, 2), np.arange(1, ATT_HEAD_DIM, 2)])


def _q_head_order():
    return [kv * KV_GROUP + g for g in range(KV_GROUP) for kv in range(KV_HEADS)]


def _rec_weights(w, gate_b, conv_w, w0, w2, a0, a2, k_k, k_a, r_k):
    o = np.cumsum((0,) + (ML_WIDTH,) * 4 + (ML_HEADS,) * 4 + (RW_WIDTH,) * 3
                  + (RW_DECAY_RANK,) * 2 + (RW_ICLR_RANK,) * 2 + (RW_GATE_RANK,))
    q, k, v, og = (w[:, o[i]:o[i + 1]] for i in range(4))
    gates = w[:, o[4]:o[8]]
    rkv = w[:, o[8]:o[11]]
    lora = w[:, o[11]:o[16]]
    gates_pad = jnp.pad(gates, ((0, 0), (0, LANES - N_GATES)))
    w_main = jnp.concatenate([q, v, og, rkv, lora, gates_pad], axis=1).astype(BF16)
    gb = gate_b.reshape(-1)
    gb_row = jnp.pad(gb, (0, LANES - N_GATES)).reshape(1, LANES)
    zeros = jnp.zeros((RW_DECAY_RANK, RW_WIDTH), F32)
    w2cat = jnp.concatenate([jnp.concatenate([w2[0], zeros], 1), jnp.concatenate([zeros, w2[1]], 1)], 0).astype(BF16)
    a2cat = jnp.concatenate([jnp.concatenate([a2[0], zeros], 1), jnp.concatenate([zeros, a2[1]], 1)], 0).astype(BF16)
    vecs = jnp.stack([k_k, k_a, r_k, w0[0], w0[1], a0[0], a0[1], jnp.zeros_like(k_k)])
    return dict(w_main=w_main, wkt=k.T.astype(BF16), wgt=gates.T.astype(BF16), gb_row=gb_row,
                gb_col=gb.reshape(N_GATES, 1), conv=conv_w, vecs=vecs, w2cat=w2cat, a2cat=a2cat)


def _att_weights(w, win_q_norm, win_k_norm, glb_q_norm, glb_k_norm, w_out):
    hd, perm = ATT_HEAD_DIM, _head_perm()
    qn, kn = Q_HEADS * hd, KV_HEADS * hd
    o = np.cumsum((0, qn, kn, kn, qn, kn, kn))
    qw, kw, vw, qg, kg, vg = (w[:, o[i]:o[i + 1]] for i in range(6))
    q_cols = np.concatenate([h * hd + perm for h in _q_head_order()])
    k_cols = np.concatenate([h * hd + perm for h in range(KV_HEADS)])
    w_all = jnp.concatenate([qw[:, q_cols], qg[:, q_cols], kw[:, k_cols], kg[:, k_cols], vw, vg], axis=1).astype(BF16)
    scale = hd ** -0.5
    log2e = float(np.log2(np.e))
    gains = jnp.concatenate([jnp.tile(win_q_norm[perm], Q_HEADS) * scale,
                             jnp.tile(glb_q_norm[perm], Q_HEADS) * (scale * log2e),
                             jnp.tile(win_k_norm[perm], KV_HEADS), jnp.tile(glb_k_norm[perm], KV_HEADS)]).reshape(1, -1)
    out_rows = np.concatenate([h * hd + np.arange(hd) for h in _q_head_order()])
    wo = jnp.concatenate([w_out[:qn][out_rows], w_out[qn:][out_rows]], axis=0).astype(BF16)
    return w_all, gains, wo


def _rope_tables(t_lat, s):
    n_freq = ATT_HEAD_DIM // 4
    pos = jnp.arange(t_lat)
    inv_freq = ROPE_THETA ** (-jnp.arange(n_freq, dtype=F32) / n_freq)
    ang = jnp.concatenate([(pos // GRID_W).astype(F32)[:, None] * inv_freq,
                           (pos % GRID_W).astype(F32)[:, None] * inv_freq], axis=-1)
    cos, sin = jnp.cos(ang), jnp.sin(ang)
    cos_h = jnp.concatenate([cos, cos], axis=-1)
    sin_h = jnp.concatenate([-sin, sin], axis=-1)
    pad = ((0, s - t_lat), (0, 0))
    cos_h = jnp.pad(cos_h, pad, constant_values=1.0)
    sin_h = jnp.pad(sin_h, pad)
    return jnp.tile(cos_h, (1, 2))[None], jnp.tile(sin_h, (1, 2))[None]


def kernel(x, c, ctx, c_ctx, norm_mix, norm_ffn, mod_w, mod_b, out_w, ffn_w1, ffn_w2, rec_in_w, ml_gate_b, ml_norm_g,
           rw_conv, rw_w0, rw_w2, rw_a0, rw_a2, rw_g2, rw_kk, rw_ka, rw_rk, rw_norm_g, att_in_w, win_q_norm,
           win_k_norm, win_sink, glb_q_norm, glb_k_norm):
    bsz, t_lat, d = x.shape
    lc = ctx.shape[1]
    s = t_lat + lc
    depth = mod_w.shape[0]
    assert depth == 2 and bsz + 1 <= SUBLANES
    assert t_lat % ROW_TILE == 0 and lc % ROW_TILE == 0 and t_lat % lc == 0 and t_lat % GRID_W == 0
    n_lat_tiles = t_lat // ROW_TILE

    cvec = jnp.zeros((SUBLANES, d), F32).at[:bsz].set(c).at[bsz].set(c_ctx)
    mods = _modulation(cvec, mod_w, mod_b)[:, :bsz + 1].reshape(depth, bsz + 1, 1, 6 * d)
    w1 = ffn_w1.astype(BF16)
    w2 = ffn_w2.astype(BF16)

    rw = _rec_weights(rec_in_w[0], ml_gate_b[0], rw_conv[0], rw_w0[0], rw_w2[0], rw_a0[0], rw_a2[0],
                      rw_kk[0], rw_ka[0], rw_rk[0])
    q, v, o, rkv, lora, gates, kt, gates_t = _rec_inproj(
        x, ctx, mods[0], norm_mix[0].reshape(1, d), rw["w_main"], rw["wkt"], rw["wgt"], rw["gb_row"], rw["gb_col"],
        n_lat_tiles)
    ones64 = _block_diag_const(RW_WIDTH, RW_HEAD_DIM, 1.0)
    r, vr, kk, bonus, gate, lw, key, bv = _rw_prep(rkv, lora, rw["conv"], rw["vecs"], rw["w2cat"], rw["a2cat"],
                                                   rw_g2[0].astype(BF16), ones64, t_lat)
    y_f, y_b, h_f, h_b = _rec_scan(r, vr, kk, lw, key, bv, q, v, kt, gates, gates_t, t_lat)
    consts = [norm_ffn[0].reshape(1, d), ml_norm_g[0].reshape(1, -1), rw_norm_g[0].reshape(1, -1),
              _block_diag_const(ML_WIDTH, ML_HEAD_DIM, 1.0 / ML_HEAD_DIM),
              _block_diag_const(RW_WIDTH, RW_HEAD_DIM, 1.0 / RW_HEAD_DIM),
              out_w[0].astype(BF16), w1[0], w2[0]]
    tokens = [h_f, h_b, o, y_f, y_b, bonus, gate]
    xs = _post_call(functools.partial(_rec_post_kernel, n_lat_tiles=n_lat_tiles), "rec_post", tokens, (x, ctx),
                    mods[0], consts, s, n_lat_tiles)

    w_att, qk_gain, wo = _att_weights(att_in_w[0], win_q_norm[0], win_k_norm[0], glb_q_norm[0], glb_k_norm[0], out_w[1])
    cos, sin = _rope_tables(t_lat, s)
    sq_avg = _block_diag_const(Q_HEADS * ATT_HEAD_DIM, ATT_HEAD_DIM, 1.0 / ATT_HEAD_DIM)
    qw, qg, kw, kg, vw, vg0, vg1 = _att_inproj(xs, mods[1], norm_mix[1].reshape(1, d), w_att, qk_gain, cos, sin,
                                               sq_avg, n_lat_tiles)
    sink_col = jnp.repeat(win_sink[0][np.array([kv * KV_GROUP + g for kv in range(KV_HEADS) for g in range(KV_GROUP)])],
                          Q_BLOCK).reshape(-1, 1)
    o_win, o_glb = _attention(qw, kw, vw, sink_col, qg, kg, vg0, vg1, t_lat)
    consts = [norm_ffn[1].reshape(1, d), wo, w1[1], w2[1]]
    return _post_call(_att_post_kernel, "att_post", [o_win, o_glb], (xs,), mods[1], consts, t_lat, n_lat_tiles)
```

```python
import functools

import numpy as np
import jax
import jax.numpy as jnp
from jax import lax
from jax.experimental import pallas as pl
from jax.experimental.pallas import tpu as pltpu

F32 = jnp.float32
BF16 = jnp.bfloat16

GRID_W = 64
NORM_EPS = 1e-6
GN_EPS = 64e-5
ROPE_THETA = 10000.0
ML_HEAD_DIM = 128
ML_HEADS = 4
ML_WIDTH = ML_HEADS * ML_HEAD_DIM
ML_CHUNK = 128
RW_HEAD_DIM = 64
RW_HEADS = 8
RW_WIDTH = RW_HEADS * RW_HEAD_DIM
RW_CHUNK = 64
RW_DECAY_RANK = 64
RW_ICLR_RANK = 64
RW_GATE_RANK = 128
RW_DECAY_SCALE = float(np.exp(-0.5))
ATT_HEAD_DIM = 64
Q_HEADS = 8
KV_HEADS = 2
KV_GROUP = Q_HEADS // KV_HEADS
KV_WIDTH = KV_HEADS * ATT_HEAD_DIM
WINDOW = 128
Q_BLOCK = 128

LANES = 128
SUBLANES = 8
VMEM_LIMIT_BYTES = 56 * 1024 * 1024
ROW_TILE = 256
HALO_ROWS = 16
MASK_VALUE = -1e30


def _params(*semantics):
    return pltpu.CompilerParams(dimension_semantics=semantics, vmem_limit_bytes=VMEM_LIMIT_BYTES)


def _resident(shape):
    zeros = (0,) * len(shape)
    return pl.BlockSpec(shape, lambda *_: zeros, pipeline_mode=pl.Buffered(1))


def _dot(a, b):
    return jnp.dot(a, b, preferred_element_type=F32)


def _dot_nt(a, b):
    return lax.dot_general(a, b, (((1,), (1,)), ((), ())), preferred_element_type=F32)


def _dot_tn(a, b):
    return lax.dot_general(a, b, (((0,), (0,)), ((), ())), preferred_element_type=F32)


def _split(x, n):
    parts, rest = [], x
    for _ in range(n):
        p = rest.astype(BF16)
        parts.append(p)
        rest = rest - p.astype(F32)
    return parts


def _dot_split_lhs(x, m, n):
    return sum(_dot(p, m) for p in _split(x, n))


def _dot_split_rhs(m, x, n):
    return sum(_dot(m, p) for p in _split(x, n))


def _dot_f32(a, b):
    ah, al = _split(a, 2)
    bh, bl = _split(b, 2)
    return _dot(ah, bh) + _dot(ah, bl) + _dot(al, bh)


def _log_sigmoid(x):
    return jnp.minimum(x, 0.0) - jnp.log1p(jnp.exp(-jnp.abs(x)))


def _norm_mod(x, gain, shift, scale):
    y = x * lax.rsqrt(jnp.mean(x * x, axis=-1, keepdims=True) + NORM_EPS)
    return (y * gain) * (1.0 + scale) + shift


def _group_layernorm(y, avg, gain):
    yc = y - _dot(y.astype(BF16), avg)
    var = _dot((yc * yc).astype(BF16), avg)
    return yc * lax.rsqrt(var + GN_EPS) * gain


def _mod_kernel(cv_ref, w_ref, b_ref, o_ref):
    cv = cv_ref[...]
    o_ref[0] = _dot_f32(cv * jax.nn.sigmoid(cv), w_ref[0]) + b_ref[0]


def _modulation(cvec, mod_w, mod_b):
    depth, d, n = mod_w.shape
    tn = n // 4
    return pl.pallas_call(
        _mod_kernel,
        grid=(depth, n // tn),
        in_specs=[pl.BlockSpec((SUBLANES, d), lambda l, j: (0, 0)),
                  pl.BlockSpec((1, d, tn), lambda l, j: (l, 0, j)),
                  pl.BlockSpec((1, 1, tn), lambda l, j: (l, 0, j))],
        out_specs=pl.BlockSpec((1, SUBLANES, tn), lambda l, j: (l, 0, j)),
        out_shape=jax.ShapeDtypeStruct((depth, SUBLANES, n), F32),
        compiler_params=_params("parallel", "parallel"),
        name="adaln_modulation",
    )(cvec, mod_w, mod_b.reshape(depth, 1, n))


REC_Q, REC_V, REC_O, REC_RKV, REC_LORA, REC_GATE, REC_END = 0, 512, 1024, 1536, 3072, 3456, 3584
N_GATES = 4 * ML_HEADS


def _token_rows(x_ref, ctx_ref, n_lat_tiles):
    return jnp.where(pl.program_id(1) >= n_lat_tiles, ctx_ref[0], x_ref[0])


def _token_specs(tm, d, n_lat_tiles):
    return [pl.BlockSpec((1, tm, d), lambda b, i: (b, jnp.minimum(i, n_lat_tiles - 1), 0)),
            pl.BlockSpec((1, tm, d), lambda b, i: (b, jnp.maximum(i - n_lat_tiles, 0), 0))]


def _rec_inproj_kernel(x_ref, ctx_ref, mod_ref, g_ref, w_ref, wkt_ref, wgt_ref, gb_ref, gbt_ref,
                       q_ref, v_ref, o_ref, rkv_ref, lora_ref, gate_ref, kt_ref, gatet_ref, *, n_lat_tiles):
    d = x_ref.shape[-1]
    mod = mod_ref[0]
    h = _norm_mod(_token_rows(x_ref, ctx_ref, n_lat_tiles), g_ref[...], mod[:, 0:d], mod[:, d:2 * d]).astype(BF16)
    q_ref[0] = _dot(h, w_ref[:, REC_Q:REC_V]).astype(BF16)
    v_ref[0] = _dot(h, w_ref[:, REC_V:REC_O]).astype(BF16)
    o_ref[0] = _dot(h, w_ref[:, REC_O:REC_RKV]).astype(BF16)
    rkv_ref[0] = _dot(h, w_ref[:, REC_RKV:REC_LORA]).astype(BF16)
    lora_ref[0] = _dot(h, w_ref[:, REC_LORA:REC_GATE])
    gate_ref[0] = _dot(h, w_ref[:, REC_GATE:REC_END]) + gb_ref[...]
    kt_ref[0] = _dot_nt(wkt_ref[...], h).astype(BF16)
    gatet_ref[0] = _dot_nt(wgt_ref[...], h) + gbt_ref[...]


def _rec_inproj(x, ctx, mods, gain, w_main, wkt, wgt, gb_row, gb_col, n_lat_tiles):
    bsz, _, d = x.shape
    s = x.shape[1] + ctx.shape[1]
    tm = ROW_TILE
    tok = lambda c: pl.BlockSpec((1, tm, c), lambda b, i: (b, i, 0))
    mod_row = lambda b, i: (jnp.where(i >= n_lat_tiles, bsz, b), 0, 0)
    return pl.pallas_call(
        functools.partial(_rec_inproj_kernel, n_lat_tiles=n_lat_tiles),
        grid=(bsz, s // tm),
        in_specs=_token_specs(tm, d, n_lat_tiles) + [
                  pl.BlockSpec((1, 1, mods.shape[-1]), mod_row),
                  _resident((1, d)), _resident(w_main.shape), _resident(wkt.shape), _resident(wgt.shape),
                  _resident(gb_row.shape), _resident(gb_col.shape)],
        out_specs=[tok(ML_WIDTH), tok(ML_WIDTH), tok(ML_WIDTH), tok(3 * RW_WIDTH), tok(REC_GATE - REC_LORA),
                   tok(LANES),
                   pl.BlockSpec((1, ML_WIDTH, tm), lambda b, i: (b, 0, i)),
                   pl.BlockSpec((1, N_GATES, tm), lambda b, i: (b, 0, i))],
        out_shape=[jax.ShapeDtypeStruct((bsz, s, ML_WIDTH), BF16),
                   jax.ShapeDtypeStruct((bsz, s, ML_WIDTH), BF16),
                   jax.ShapeDtypeStruct((bsz, s, ML_WIDTH), BF16),
                   jax.ShapeDtypeStruct((bsz, s, 3 * RW_WIDTH), BF16),
                   jax.ShapeDtypeStruct((bsz, s, REC_GATE - REC_LORA), F32),
                   jax.ShapeDtypeStruct((bsz, s, LANES), F32),
                   jax.ShapeDtypeStruct((bsz, ML_WIDTH, s), BF16),
                   jax.ShapeDtypeStruct((bsz, N_GATES, s), F32)],
        compiler_params=_params("parallel", "parallel"),
        name="rec_inproj",
    )(x, ctx, mods, gain, w_main, wkt, wgt, gb_row, gb_col)


def _mlstm_stages(qf_ref, vf_ref, ktf_ref, gf_ref, gtf_ref, qb_ref, vb_ref, ktb_ref, gb_ref, gtb_ref,
                  hf_ref, hb_ref, c_ref, m_ref):
    L, dh = ML_CHUNK, ML_HEAD_DIM
    t_idx = lax.broadcasted_iota(jnp.int32, (L, L), 0)
    s_idx = lax.broadcasted_iota(jnp.int32, (L, L), 1)
    ones_col = (lax.broadcasted_iota(jnp.int32, (L, LANES), 1) == 0).astype(BF16)
    scale = dh ** -0.5
    streams = ((qf_ref, vf_ref, ktf_ref, gf_ref, gtf_ref, hf_ref),
               (qb_ref, vb_ref, ktb_ref, gb_ref, gtb_ref, hb_ref))
    chains = []
    for d, (q_ref, v_ref, kt_ref, g_ref, gt_ref, h_ref) in enumerate(streams):
        earlier = (s_idx <= t_idx) if d == 0 else (s_idx >= t_idx)
        tri = earlier.astype(BF16)
        tri_t = ((t_idx <= s_idx) if d == 0 else (t_idx >= s_idx)).astype(BF16)
        gt = gt_ref[0]
        lf_rows = _log_sigmoid(gt)
        b_cols = _dot_split_rhs(tri, _log_sigmoid(g_ref[0]), 3)
        b_rows = _dot_split_lhs(lf_rows, tri_t, 3)
        tot = jnp.sum(lf_rows, axis=1, keepdims=True)
        q, v, kt = q_ref[0], v_ref[0], kt_ref[0]
        for hh in range(ML_HEADS):
            ic, fc = 2 * ML_HEADS * d + hh, 2 * ML_HEADS * d + ML_HEADS + hh
            chains.append(dict(j=d * ML_HEADS + hh, hh=hh, h_ref=h_ref, earlier=earlier,
                               b_col=b_cols[:, fc:fc + 1], b_row=b_rows[fc:fc + 1, :], i_row=gt[ic:ic + 1, :],
                               b_end=tot[fc:fc + 1, :], q=q[:, hh * dh:(hh + 1) * dh], kt=kt[hh * dh:(hh + 1) * dh, :],
                               v_aug=jnp.concatenate([v[:, hh * dh:(hh + 1) * dh], ones_col], axis=1)))
    yield
    for ch in chains:
        ch["cst"] = c_ref[ch["j"]]
        ch["qk"] = _dot(ch["q"], ch["kt"])
        ch["qc"] = _dot(ch["q"], ch["cst"].astype(BF16))
    yield
    for ch in chains:
        m_prev = m_ref[ch["j"]:ch["j"] + 1, 0:1]
        d_in = jnp.where(ch["earlier"], ch["b_col"] - ch["b_row"] + ch["i_row"], MASK_VALUE)
        d_prev = ch["b_col"] + m_prev
        m_t = jnp.maximum(d_prev, jnp.max(d_in, axis=1, keepdims=True))
        ch["s"] = (ch["qk"] * (scale * jnp.exp(d_in - m_t))).astype(BF16)
        ch["w_prev"], ch["floor"] = jnp.exp(d_prev - m_t), jnp.exp(-m_t)
        d_end = ch["b_end"] - ch["b_row"] + ch["i_row"]
        m_new = jnp.maximum(ch["b_end"] + m_prev, jnp.max(d_end, axis=1, keepdims=True))
        ch["kw"] = (ch["kt"].astype(F32) * (jnp.exp(d_end - m_new) * scale)).astype(BF16)
        ch["decay"] = jnp.exp(ch["b_end"] + m_prev - m_new)
        m_ref[ch["j"]:ch["j"] + 1, :] = jnp.broadcast_to(m_new, (1, LANES))
    yield
    for ch in chains:
        acc = _dot(ch["s"], ch["v_aug"]) + ch["w_prev"] * ch["qc"]
        den = jnp.maximum(jnp.abs(acc[:, dh:dh + 1]), ch["floor"])
        ch["h_ref"][0, :, ch["hh"] * dh:(ch["hh"] + 1) * dh] = acc[:, :dh] / den
    yield
    for ch in chains:
        c_ref[ch["j"]] = ch["decay"] * ch["cst"] + _dot(ch["kw"], ch["v_aug"])


def _chunk_order(n_lat, n_ctx):
    n = n_lat + n_ctx
    fwd = lambda c: (c + n_lat) % n
    bwd = lambda c: n - 1 - c
    return fwd, bwd


def _rw_prep_kernel(z_ref, zp_ref, zn_ref, lora_ref, conv_ref, vec_ref, w2_ref, a2_ref, g2_ref, ones_ref,
                    r_ref, v_ref, kk_ref, bonus_ref, gate_ref, lw_ref, key_ref, bv_ref, *, t_lat, s_tot):
    tm, w = z_ref.shape[1], RW_WIDTH
    row0 = pl.program_id(1) * tm
    has_prev = jnp.where(jnp.logical_and(row0 != 0, row0 != t_lat), 1.0, 0.0)
    has_next = jnp.where(jnp.logical_and(row0 + tm != t_lat, row0 + tm != s_tot), 1.0, 0.0)
    z = z_ref[0].astype(F32)
    prev_row = zp_ref[0][HALO_ROWS - 1:HALO_ROWS, :].astype(F32) * has_prev
    next_row = zn_ref[0][0:1, :].astype(F32) * has_next
    ridx = lax.broadcasted_iota(jnp.int32, z.shape, 0)
    z_prev = jnp.where(ridx == 0, prev_row, pltpu.roll(z, 1, 0))
    z_next = jnp.where(ridx == tm - 1, next_row, pltpu.roll(z, tm - 1, 0))
    cw = conv_ref[...]
    zc = cw[0:1] * z_prev + cw[1:2] * z + cw[2:3] * z_next
    r, kr, vr = zc[:, :w], zc[:, w:2 * w], zc[:, 2 * w:]
    vec = vec_ref[...]
    k_k, k_a, r_k = vec[0:1], vec[1:2], vec[2:3]
    ones = ones_ref[...]
    kkr = kr * k_k
    kk = kkr * lax.rsqrt(_dot((kkr * kkr).astype(BF16), ones) + NORM_EPS)
    r_ref[0] = r.astype(BF16)
    v_ref[0] = vr.astype(BF16)
    kk_ref[0] = kk.astype(BF16)
    bonus_ref[0] = (_dot((r * kr * r_k).astype(BF16), ones) * vr).astype(BF16)
    lora = lora_ref[0]
    dec = _dot(jnp.tanh(lora[:, 0:LANES]).astype(BF16), w2_ref[...])
    icl = _dot(lora[:, LANES:2 * LANES].astype(BF16), a2_ref[...])
    gate_ref[0] = _dot(jax.nn.sigmoid(lora[:, 2 * LANES:3 * LANES]).astype(BF16), g2_ref[...]).astype(BF16)
    for d in range(2):
        a = jax.nn.sigmoid(vec[5 + d:6 + d] + icl[:, d * w:(d + 1) * w])
        lw_ref[d, 0] = -RW_DECAY_SCALE * jax.nn.sigmoid(vec[3 + d:4 + d] + dec[:, d * w:(d + 1) * w])
        key_ref[d, 0] = (kr * (1.0 + (a - 1.0) * k_a)).astype(BF16)
        bv_ref[d, 0] = (kk * a).astype(BF16)


def _rw_prep(rkv, lora, conv_w, vecs, w2cat, a2cat, g2, ones64, t_lat):
    bsz, s, _ = rkv.shape
    tm, w = ROW_TILE, RW_WIDTH
    nsub = tm // HALO_ROWS
    tok = lambda c: pl.BlockSpec((1, tm, c), lambda b, i: (b, i, 0))
    dirtok = pl.BlockSpec((2, 1, tm, w), lambda b, i: (0, b, i, 0))
    last = s // HALO_ROWS - 1
    return pl.pallas_call(
        functools.partial(_rw_prep_kernel, t_lat=t_lat, s_tot=s),
        grid=(bsz, s // tm),
        in_specs=[tok(3 * w),
                  pl.BlockSpec((1, HALO_ROWS, 3 * w), lambda b, i: (b, jnp.maximum(i * nsub - 1, 0), 0)),
                  pl.BlockSpec((1, HALO_ROWS, 3 * w), lambda b, i: (b, jnp.minimum((i + 1) * nsub, last), 0)),
                  tok(lora.shape[-1]),
                  _resident(conv_w.shape), _resident(vecs.shape), _resident(w2cat.shape), _resident(a2cat.shape),
                  _resident(g2.shape), _resident(ones64.shape)],
        out_specs=[tok(w)] * 5 + [dirtok] * 3,
        out_shape=[jax.ShapeDtypeStruct((bsz, s, w), BF16)] * 5
        + [jax.ShapeDtypeStruct((2, bsz, s, w), dt) for dt in (F32, BF16, BF16)],
        compiler_params=_params("parallel", "parallel"),
        name="rwkv_prep",
    )(rkv, rkv, rkv, lora, conv_w, vecs, w2cat, a2cat, g2, ones64)


def _rw_scan_stages(rf_ref, vf_ref, kkf_ref, lwf_ref, keyf_ref, bvf_ref, rb_ref, vb_ref, kkb_ref, lwb_ref, keyb_ref,
                    bvb_ref, yf_ref, yb_ref, s_ref, step):
    C, n = RW_CHUNK, RW_HEAD_DIM
    t2 = lax.broadcasted_iota(jnp.int32, (2 * C, 2 * C), 0) % C
    s2 = lax.broadcasted_iota(jnp.int32, (2 * C, 2 * C), 1) % C
    lower = lax.broadcasted_iota(jnp.int32, (2 * C, 2 * C), 0) >= C

    chains = []
    rows_of = (slice(step * C, (step + 1) * C), slice((1 - step) * C, (2 - step) * C))
    for d, (r_ref, v_ref, kk_ref, lw_ref, key_ref, bv_ref) in enumerate(
            ((rf_ref, vf_ref, kkf_ref, lwf_ref, keyf_ref, bvf_ref), (rb_ref, vb_ref, kkb_ref, lwb_ref, keyb_ref, bvb_ref))):
        rows = rows_of[d]
        rel = (s2 - t2) if d == 0 else (t2 - s2)
        keep = rel < jnp.where(lower, 1, 0)
        tri = (rel[:C, :C] <= 0).astype(BF16)
        lw = lw_ref[0, 0, rows, :]
        cum = _dot_split_rhs(tri, lw, 3)
        tot = jnp.sum(lw, axis=0, keepdims=True)
        r, v, kk, key, bv = (a.astype(F32) for a in (r_ref[0, rows, :], v_ref[0, rows, :], kk_ref[0, rows, :],
                                                     key_ref[0, 0, rows, :], bv_ref[0, 0, rows, :]))
        e_neg = jnp.exp(-cum)
        e_end = jnp.exp(tot - cum)
        alpha = (kk * jnp.exp(cum - lw)).astype(BF16)
        rho = (r * jnp.exp(cum)).astype(BF16)
        beta = (bv * e_neg).astype(BF16)
        kappa = (key * e_neg).astype(BF16)
        beta_end = (bv * e_end).astype(BF16)
        kappa_end = (key * e_end).astype(BF16)
        gamma = jnp.exp(tot)
        vb = v.astype(BF16)
        for h in range(RW_HEADS):
            sl = slice(h * n, (h + 1) * n)
            chains.append(dict(j=d * RW_HEADS + h, keep=keep, a=alpha[:, sl], rho=rho[:, sl], v=vb[:, sl],
                               lhs=jnp.concatenate([alpha[:, sl], rho[:, sl]], axis=0),
                               rhs=jnp.concatenate([beta[:, sl], kappa[:, sl]], axis=0),
                               end=jnp.concatenate([beta_end[:, sl], kappa_end[:, sl]], axis=0),
                               gamma=gamma[:, sl]))
    yield
    for ch in chains:
        ch["big"] = jnp.where(ch["keep"], _dot_nt(ch["lhs"], ch["rhs"]), 0.0).astype(BF16)
    yield
    for ch in chains:
        ch["lmv"] = _dot(ch["big"][:, C:], ch["v"])
    yield
    tt = lax.broadcasted_iota(jnp.int32, (C, C), 0)
    ss = lax.broadcasted_iota(jnp.int32, (C, C), 1)
    eye = (tt == ss).astype(F32)
    for ch in chains:
        ch["lab"] = ch["big"][:C, :C]
        ch["inv"] = eye - jnp.where((tt >> 1) == (ss >> 1), ch["lab"], 0.0).astype(F32)
    for k in range(1, 6):
        couples = jnp.logical_and((tt >> (k + 1)) == (ss >> (k + 1)), (tt >> k) != (ss >> k))
        for ch in chains:
            ch["ed"] = _dot(jnp.where(couples, ch["lab"], 0.0).astype(BF16), ch["inv"].astype(BF16)).astype(BF16)
        yield
        for ch in chains:
            ch["inv"] = ch["inv"] - _dot(ch["inv"].astype(BF16), ch["ed"])
        yield
    for ch in chains:
        x = -jnp.concatenate([ch["a"].astype(F32), ch["lmv"][:C]], axis=1)
        ch["x"] = _dot(ch["inv"].astype(BF16), x.astype(BF16))
    yield
    for ch in chains:
        ch["st"] = s_ref[ch["j"]]
        ch["ws"] = _dot_nt(jnp.concatenate([ch["x"][:, :n].astype(BF16), ch["rho"]], axis=0), ch["st"].astype(BF16))
    yield
    for ch in chains:
        ch["u"] = (ch["ws"][:C] + ch["x"][:, n:]).astype(BF16)
    for ch in chains:
        ch["y"] = ch["ws"][C:] + _dot(ch["big"][C:, :C], ch["u"]) + ch["lmv"][C:]
    yield
    for ch in chains:
        upd = _dot_tn(jnp.concatenate([ch["u"], ch["v"]], axis=0), ch["end"])
        s_ref[ch["j"]] = ch["st"] * ch["gamma"] + upd
    yf_ref[0, rows_of[0], :] = jnp.concatenate([ch["y"] for ch in chains[:RW_HEADS]], axis=1)
    yb_ref[0, rows_of[1], :] = jnp.concatenate([ch["y"] for ch in chains[RW_HEADS:]], axis=1)


RW_STAGES_PER_ML_STAGE = 3


def _rec_scan_kernel(*refs):
    rw_in, ml_in = refs[:12], refs[12:22]
    yf_ref, yb_ref, hf_ref, hb_ref, s_ref, c_ref, m_ref = refs[22:]

    @pl.when(pl.program_id(1) == 0)
    def _():
        s_ref[...] = jnp.zeros_like(s_ref)
        c_ref[...] = jnp.zeros_like(c_ref)
        m_ref[...] = jnp.zeros_like(m_ref)

    ml = _mlstm_stages(*ml_in, hf_ref, hb_ref, c_ref, m_ref)
    n_stage = 0
    for step in range(ML_CHUNK // RW_CHUNK):
        for _ in _rw_scan_stages(*rw_in, yf_ref, yb_ref, s_ref, step):
            n_stage += 1
            if n_stage % RW_STAGES_PER_ML_STAGE == 0:
                next(ml, None)
    for _ in ml:
        pass


def _rec_scan(r, v, kk, lw, key, bv, q, vm, kt, gates, gates_t, t_lat):
    bsz, s, w = r.shape
    L = ML_CHUNK
    orders = _chunk_order(t_lat // L, (s - t_lat) // L)
    rw_specs, ml_specs, out_rw, out_ml = [], [], [], []
    for d, order in enumerate(orders):
        tok = lambda c, order=order: pl.BlockSpec((1, L, c), lambda b, i: (b, order(i), 0))
        dirtok = pl.BlockSpec((1, 1, L, w), lambda b, i, order=order, d=d: (d, b, order(i), 0))
        timelast = lambda c, order=order: pl.BlockSpec((1, c, L), lambda b, i: (b, 0, order(i)))
        rw_specs += [tok(w), tok(w), tok(w), dirtok, dirtok, dirtok]
        ml_specs += [tok(ML_WIDTH), tok(ML_WIDTH), timelast(ML_WIDTH), tok(LANES), timelast(N_GATES)]
        out_rw.append(tok(w))
        out_ml.append(tok(ML_WIDTH))
    return pl.pallas_call(
        _rec_scan_kernel,
        grid=(bsz, s // L),
        in_specs=rw_specs + ml_specs,
        out_specs=out_rw + out_ml,
        out_shape=[jax.ShapeDtypeStruct((bsz, s, w), F32)] * 2 + [jax.ShapeDtypeStruct((bsz, s, ML_WIDTH), F32)] * 2,
        scratch_shapes=[pltpu.VMEM((2 * RW_HEADS, RW_HEAD_DIM, RW_HEAD_DIM), F32),
                        pltpu.VMEM((2 * ML_HEADS, ML_HEAD_DIM, ML_HEAD_DIM + LANES), F32),
                        pltpu.VMEM((2 * ML_HEADS, LANES), F32)],
        compiler_params=_params("parallel", "arbitrary"),
        name="recurrent_scan",
    )(r, v, kk, lw, key, bv, r, v, kk, lw, key, bv, q, vm, kt, gates, gates_t, q, vm, kt, gates, gates_t)


def _residual_mlp(mix, x, mod, g_ffn, wo_ref, w1_ref, w2_ref, out_ref):
    d = x.shape[-1]
    x1 = x + mod[:, 2 * d:3 * d] * _dot(mix, wo_ref[...])
    h2 = _norm_mod(x1, g_ffn, mod[:, 3 * d:4 * d], mod[:, 4 * d:5 * d]).astype(BF16)
    acc = jnp.zeros_like(x1)
    for k in range(w1_ref.shape[1] // d):
        hid = jnp.maximum(_dot(h2, w1_ref[:, k * d:(k + 1) * d]), 0.0)
        acc = acc + _dot((hid * hid).astype(BF16), w2_ref[k * d:(k + 1) * d, :])
    out_ref[0] = x1 + mod[:, 5 * d:6 * d] * acc


def _rec_post_kernel(hf_ref, hb_ref, o_ref, yf_ref, yb_ref, bonus_ref, gate_ref, x_ref, ctx_ref, mod_ref, gf_ref,
                     mlg_ref, rwg_ref, avg_ml_ref, avg_rw_ref, wo_ref, w1_ref, w2_ref, out_ref, *, n_lat_tiles):
    ml = _group_layernorm(hf_ref[0] + hb_ref[0], avg_ml_ref[...], mlg_ref[...]) * jax.nn.sigmoid(o_ref[0].astype(F32))
    rw = (_group_layernorm(yf_ref[0] + yb_ref[0], avg_rw_ref[...], rwg_ref[...]) + bonus_ref[0]) * gate_ref[0]
    mix = jnp.concatenate([ml, rw], axis=1).astype(BF16)
    x = _token_rows(x_ref, ctx_ref, n_lat_tiles)
    _residual_mlp(mix, x, mod_ref[0], gf_ref[...], wo_ref, w1_ref, w2_ref, out_ref)


def _att_post_kernel(ow_ref, og_ref, x_ref, mod_ref, gf_ref, wo_ref, w1_ref, w2_ref, out_ref):
    mix = jnp.concatenate([ow_ref[0], og_ref[0]], axis=1)
    _residual_mlp(mix, x_ref[0], mod_ref[0], gf_ref[...], wo_ref, w1_ref, w2_ref, out_ref)


def _post_call(kernel, name, token_inputs, residual, mods, consts, n_rows, n_lat_tiles):
    bsz, _, d = residual[0].shape
    tm = ROW_TILE
    mod_row = lambda b, i: (jnp.where(i >= n_lat_tiles, bsz, b), 0, 0)
    tok = lambda c: pl.BlockSpec((1, tm, c), lambda b, i: (b, i, 0))
    in_specs = [tok(arr.shape[-1]) for arr in token_inputs]
    in_specs += _token_specs(tm, d, n_lat_tiles) if len(residual) == 2 else [tok(d)]
    in_specs += [pl.BlockSpec((1, 1, mods.shape[-1]), mod_row)]
    in_specs += [_resident(c.shape) for c in consts]
    args = list(token_inputs) + list(residual)
    return pl.pallas_call(
        kernel,
        grid=(bsz, n_rows // tm),
        in_specs=in_specs,
        out_specs=pl.BlockSpec((1, tm, d), lambda b, i: (b, i, 0)),
        out_shape=jax.ShapeDtypeStruct((bsz, n_rows, d), F32),
        compiler_params=_params("parallel", "parallel"),
        name=name,
    )(*args, mods, *consts)


ATT_QW, ATT_QG, ATT_KW, ATT_KG, ATT_VW, ATT_VG, ATT_END = 0, 512, 1024, 1152, 1280, 1408, 1536
ATT_QK_END = ATT_VW


def _att_inproj_kernel(x_ref, mod_ref, g_ref, w_ref, qkg_ref, cos_ref, sin_ref, sq_ref,
                       qw_ref, qg_ref, kw_ref, kg_ref, vw_ref, vg0_ref, vg1_ref):
    d = x_ref.shape[-1]
    mod = mod_ref[0]
    h = _norm_mod(x_ref[0], g_ref[...], mod[:, 0:d], mod[:, d:2 * d]).astype(BF16)
    half = ATT_HEAD_DIM // 2
    cos2, sin2 = cos_ref[0], sin_ref[0]

    def norm_rope(lo, hi):
        z = _dot(h, w_ref[:, lo:hi])
        n = hi - lo
        sq = sq_ref[...]
        z = z * lax.rsqrt(_dot((z * z).astype(BF16), sq[:n, :n]) + NORM_EPS) * qkg_ref[:, lo:hi]
        reps = n // LANES
        cos = jnp.concatenate([cos2] * reps, axis=1)
        sin = jnp.concatenate([sin2] * reps, axis=1)
        lane = lax.broadcasted_iota(jnp.int32, z.shape, 1)
        partner = jnp.where(lane % ATT_HEAD_DIM < half, pltpu.roll(z, n - half, 1), pltpu.roll(z, half, 1))
        return (z * cos + partner * sin).astype(BF16)

    qw_ref[0] = norm_rope(ATT_QW, ATT_QG)
    qg_ref[0] = norm_rope(ATT_QG, ATT_KW)
    kk = norm_rope(ATT_KW, ATT_VW)
    kw_ref[0] = kk[:, :KV_WIDTH]
    kg_ref[0] = kk[:, KV_WIDTH:]
    vv = _dot(h, w_ref[:, ATT_VW:ATT_END])
    vw_ref[0] = vv[:, :KV_WIDTH].astype(BF16)
    vg = vv[:, KV_WIDTH:]
    lane = lax.broadcasted_iota(jnp.int32, vg.shape, 1)
    vg0_ref[0] = jnp.where(lane < ATT_HEAD_DIM, vg, jnp.where(lane == ATT_HEAD_DIM, 1.0, 0.0)).astype(BF16)
    vg1_ref[0] = jnp.where(lane >= ATT_HEAD_DIM, vg, jnp.where(lane == 0, 1.0, 0.0)).astype(BF16)


def _att_inproj(xs, mods, gain, w, qk_gain, cos, sin, sq_avg, n_lat_tiles):
    bsz, s, d = xs.shape
    tm = ROW_TILE
    tok = lambda c: pl.BlockSpec((1, tm, c), lambda b, i: (b, i, 0))
    mod_row = lambda b, i: (jnp.where(i >= n_lat_tiles, bsz, b), 0, 0)
    rope = pl.BlockSpec((1, tm, LANES), lambda b, i: (0, i, 0))
    qw = Q_HEADS * ATT_HEAD_DIM
    return pl.pallas_call(
        _att_inproj_kernel,
        grid=(bsz, s // tm),
        in_specs=[tok(d), pl.BlockSpec((1, 1, mods.shape[-1]), mod_row), _resident((1, d)), _resident(w.shape),
                  _resident(qk_gain.shape), rope, rope, _resident(sq_avg.shape)],
        out_specs=[tok(qw), tok(qw)] + [tok(KV_WIDTH)] * 5,
        out_shape=[jax.ShapeDtypeStruct((bsz, s, c), BF16) for c in (qw, qw) + (KV_WIDTH,) * 5],
        compiler_params=_params("parallel", "parallel"),
        name="att_inproj",
    )(xs, mods, gain, w, qk_gain, cos, sin, sq_avg)


def _stack_queries(q):
    lane = lax.broadcasted_iota(jnp.int32, (q.shape[0], LANES), 1)
    zero = jnp.zeros((q.shape[0], LANES), q.dtype)
    slabs = [q[:, g * LANES:(g + 1) * LANES] for g in range(KV_GROUP)]
    lo = [jnp.where(lane < ATT_HEAD_DIM, sl, zero) for sl in slabs]
    hi = [jnp.where(lane >= ATT_HEAD_DIM, sl, zero) for sl in slabs]
    return jnp.concatenate(lo + hi, axis=0)


def _unstack_outputs(o, rows):
    lane = lax.broadcasted_iota(jnp.int32, (rows, LANES), 1)
    slabs = [jnp.where(lane < ATT_HEAD_DIM, o[g * rows:(g + 1) * rows], o[(KV_GROUP + g) * rows:(KV_GROUP + g + 1) * rows])
             for g in range(KV_GROUP)]
    return jnp.concatenate(slabs, axis=1)


def _attn_kernel(qw_ref, kp_ref, ko_ref, kn_ref, kc_ref, vp_ref, vo_ref, vn_ref, vc_ref, sink_ref,
                 qg_ref, k_ref, v0_ref, v1_ref, ow_ref, og_ref, *, tk):
    L = Q_BLOCK
    i, nb = pl.program_id(1), pl.num_programs(1)
    lc = kc_ref.shape[1]
    qs_w = _stack_queries(qw_ref[0])
    keys = jnp.concatenate([kp_ref[0], ko_ref[0], kn_ref[0], kc_ref[0]], axis=0)
    vals = jnp.concatenate([vp_ref[0], vo_ref[0], vn_ref[0], vc_ref[0]], axis=0)
    s_w = _dot_nt(qs_w, keys)

    qs = _stack_queries(qg_ref[0])
    rows = qs.shape[0]
    half = rows // 2
    n_tiles = k_ref.shape[1] // tk
    scores = lambda j: _dot_nt(qs, k_ref[0, j * tk:(j + 1) * tk, :])
    s_next = scores(0)

    nk = 3 * L + lc
    row = lax.broadcasted_iota(jnp.int32, (L, nk), 0)
    col = lax.broadcasted_iota(jnp.int32, (L, nk), 1)
    off = col - L - row
    valid = jnp.logical_and(off <= WINDOW, off >= -WINDOW)
    valid = jnp.logical_and(valid, col >= jnp.where(i > 0, 0, L))
    valid = jnp.logical_and(valid, col < jnp.where(i < nb - 1, 3 * L, 2 * L))
    valid = jnp.logical_or(valid, col >= 3 * L)
    valid = jnp.concatenate([valid] * (2 * KV_GROUP), axis=0)
    s_w = jnp.where(valid, s_w, MASK_VALUE)
    sink = sink_ref[...]
    m_w = jnp.maximum(jnp.max(s_w, axis=1, keepdims=True), sink)
    p_w = jnp.exp(s_w - m_w)
    den_w = jnp.sum(p_w, axis=1, keepdims=True) + jnp.exp(sink - m_w)
    ow_ref[0] = _unstack_outputs(_dot(p_w.astype(BF16), vals) / den_w, L).astype(BF16)

    m = jnp.full((rows, 1), MASK_VALUE, F32)
    acc = jnp.zeros((rows, LANES), F32)
    for j in range(n_tiles):
        s = s_next
        if j + 1 < n_tiles:
            s_next = scores(j + 1)
        m_new = jnp.maximum(m, jnp.max(s, axis=1, keepdims=True))
        p = jnp.exp2(s - m_new).astype(BF16)
        pv = jnp.concatenate([_dot(p[:half], v0_ref[0, j * tk:(j + 1) * tk, :]),
                              _dot(p[half:], v1_ref[0, j * tk:(j + 1) * tk, :])], axis=0)
        acc = jnp.exp2(m - m_new) * acc + pv
        m = m_new
    den = jnp.concatenate([acc[:half, ATT_HEAD_DIM:ATT_HEAD_DIM + 1], acc[half:, 0:1]], axis=0)
    og_ref[0] = _unstack_outputs(acc / den, L).astype(BF16)


GLB_KEY_TILES = (2816, 1408, 768, 512, 384, 256, 128)


def _attention(qw, kw, vw, sink_col, qg, kg, vg0, vg1, t_lat):
    bsz, s, width = qw.shape
    L = Q_BLOCK
    nb = t_lat // L
    lc = s - t_lat
    tk = next(t for t in GLB_KEY_TILES if s % t == 0)
    qblk = pl.BlockSpec((1, L, width), lambda b, i: (b, i, 0))
    kv = lambda f: pl.BlockSpec((1, L, KV_WIDTH), lambda b, i: (b, f(i), 0))
    ctx = pl.BlockSpec((1, lc, KV_WIDTH), lambda b, i: (b, t_lat // lc, 0))
    nbr = [kv(lambda i: jnp.maximum(i - 1, 0)), kv(lambda i: i), kv(lambda i: jnp.minimum(i + 1, nb - 1)), ctx]
    whole = pl.BlockSpec((1, s, KV_WIDTH), lambda b, i: (b, 0, 0))
    return pl.pallas_call(
        functools.partial(_attn_kernel, tk=tk),
        grid=(bsz, nb),
        in_specs=[qblk] + nbr + nbr + [_resident(sink_col.shape), qblk, whole, whole, whole],
        out_specs=[qblk, qblk],
        out_shape=[jax.ShapeDtypeStruct((bsz, t_lat, width), BF16)] * 2,
        compiler_params=_params("parallel", "parallel"),
        name="attention",
    )(qw, kw, kw, kw, kw, vw, vw, vw, vw, sink_col, qg, kg, vg0, vg1)


def _block_diag_const(width, group, value):
    idx = np.arange(width) // group
    return jnp.asarray((idx[:, None] == idx[None, :]) * value, dtype=BF16)


def _head_perm():
    return np.concatenate([np.arange(0, ATT_HEAD_DIM, 2), np.arange(1, ATT_HEAD_DIM, 2)])


def _q_head_order():
    return [kv * KV_GROUP + g for g in range(KV_GROUP) for kv in range(KV_HEADS)]


def _rec_weights(w, gate_b, conv_w, w0, w2, a0, a2, k_k, k_a, r_k):
    o = np.cumsum((0,) + (ML_WIDTH,) * 4 + (ML_HEADS,) * 4 + (RW_WIDTH,) * 3
                  + (RW_DECAY_RANK,) * 2 + (RW_ICLR_RANK,) * 2 + (RW_GATE_RANK,))
    q, k, v, og = (w[:, o[i]:o[i + 1]] for i in range(4))
    gates = w[:, o[4]:o[8]]
    rkv = w[:, o[8]:o[11]]
    lora = w[:, o[11]:o[16]]
    gates_pad = jnp.pad(gates, ((0, 0), (0, LANES - N_GATES)))
    w_main = jnp.concatenate([q, v, og, rkv, lora, gates_pad], axis=1).astype(BF16)
    gb = gate_b.reshape(-1)
    gb_row = jnp.pad(gb, (0, LANES - N_GATES)).reshape(1, LANES)
    zeros = jnp.zeros((RW_DECAY_RANK, RW_WIDTH), F32)
    w2cat = jnp.concatenate([jnp.concatenate([w2[0], zeros], 1), jnp.concatenate([zeros, w2[1]], 1)], 0).astype(BF16)
    a2cat = jnp.concatenate([jnp.concatenate([a2[0], zeros], 1), jnp.concatenate([zeros, a2[1]], 1)], 0).astype(BF16)
    vecs = jnp.stack([k_k, k_a, r_k, w0[0], w0[1], a0[0], a0[1], jnp.zeros_like(k_k)])
    return dict(w_main=w_main, wkt=k.T.astype(BF16), wgt=gates.T.astype(BF16), gb_row=gb_row,
                gb_col=gb.reshape(N_GATES, 1), conv=conv_w, vecs=vecs, w2cat=w2cat, a2cat=a2cat)


def _att_weights(w, win_q_norm, win_k_norm, glb_q_norm, glb_k_norm, w_out):
    hd, perm = ATT_HEAD_DIM, _head_perm()
    qn, kn = Q_HEADS * hd, KV_HEADS * hd
    o = np.cumsum((0, qn, kn, kn, qn, kn, kn))
    qw, kw, vw, qg, kg, vg = (w[:, o[i]:o[i + 1]] for i in range(6))
    q_cols = np.concatenate([h * hd + perm for h in _q_head_order()])
    k_cols = np.concatenate([h * hd + perm for h in range(KV_HEADS)])
    w_all = jnp.concatenate([qw[:, q_cols], qg[:, q_cols], kw[:, k_cols], kg[:, k_cols], vw, vg], axis=1).astype(BF16)
    scale = hd ** -0.5
    log2e = float(np.log2(np.e))
    gains = jnp.concatenate([jnp.tile(win_q_norm[perm], Q_HEADS) * scale,
                             jnp.tile(glb_q_norm[perm], Q_HEADS) * (scale * log2e),
                             jnp.tile(win_k_norm[perm], KV_HEADS), jnp.tile(glb_k_norm[perm], KV_HEADS)]).reshape(1, -1)
    out_rows = np.concatenate([h * hd + np.arange(hd) for h in _q_head_order()])
    wo = jnp.concatenate([w_out[:qn][out_rows], w_out[qn:][out_rows]], axis=0).astype(BF16)
    return w_all, gains, wo


def _rope_tables(t_lat, s):
    n_freq = ATT_HEAD_DIM // 4
    pos = jnp.arange(t_lat)
    inv_freq = ROPE_THETA ** (-jnp.arange(n_freq, dtype=F32) / n_freq)
    ang = jnp.concatenate([(pos // GRID_W).astype(F32)[:, None] * inv_freq,
                           (pos % GRID_W).astype(F32)[:, None] * inv_freq], axis=-1)
    cos, sin = jnp.cos(ang), jnp.sin(ang)
    cos_h = jnp.concatenate([cos, cos], axis=-1)
    sin_h = jnp.concatenate([-sin, sin], axis=-1)
    pad = ((0, s - t_lat), (0, 0))
    cos_h = jnp.pad(cos_h, pad, constant_values=1.0)
    sin_h = jnp.pad(sin_h, pad)
    return jnp.tile(cos_h, (1, 2))[None], jnp.tile(sin_h, (1, 2))[None]


def kernel(x, c, ctx, c_ctx, norm_mix, norm_ffn, mod_w, mod_b, out_w, ffn_w1, ffn_w2, rec_in_w, ml_gate_b, ml_norm_g,
           rw_conv, rw_w0, rw_w2, rw_a0, rw_a2, rw_g2, rw_kk, rw_ka, rw_rk, rw_norm_g, att_in_w, win_q_norm,
           win_k_norm, win_sink, glb_q_norm, glb_k_norm):
    bsz, t_lat, d = x.shape
    lc = ctx.shape[1]
    s = t_lat + lc
    depth = mod_w.shape[0]
    assert depth == 2 and bsz + 1 <= SUBLANES
    assert t_lat % ROW_TILE == 0 and lc % ROW_TILE == 0 and t_lat % lc == 0 and t_lat % GRID_W == 0
    n_lat_tiles = t_lat // ROW_TILE

    cvec = jnp.zeros((SUBLANES, d), F32).at[:bsz].set(c).at[bsz].set(c_ctx)
    mods = _modulation(cvec, mod_w, mod_b)[:, :bsz + 1].reshape(depth, bsz + 1, 1, 6 * d)
    w1 = ffn_w1.astype(BF16)
    w2 = ffn_w2.astype(BF16)

    rw = _rec_weights(rec_in_w[0], ml_gate_b[0], rw_conv[0], rw_w0[0], rw_w2[0], rw_a0[0], rw_a2[0],
                      rw_kk[0], rw_ka[0], rw_rk[0])
    q, v, o, rkv, lora, gates, kt, gates_t = _rec_inproj(
        x, ctx, mods[0], norm_mix[0].reshape(1, d), rw["w_main"], rw["wkt"], rw["wgt"], rw["gb_row"], rw["gb_col"],
        n_lat_tiles)
    ones64 = _block_diag_const(RW_WIDTH, RW_HEAD_DIM, 1.0)
    r, vr, kk, bonus, gate, lw, key, bv = _rw_prep(rkv, lora, rw["conv"], rw["vecs"], rw["w2cat"], rw["a2cat"],
                                                   rw_g2[0].astype(BF16), ones64, t_lat)
    y_f, y_b, h_f, h_b = _rec_scan(r, vr, kk, lw, key, bv, q, v, kt, gates, gates_t, t_lat)
    consts = [norm_ffn[0].reshape(1, d), ml_norm_g[0].reshape(1, -1), rw_norm_g[0].reshape(1, -1),
              _block_diag_const(ML_WIDTH, ML_HEAD_DIM, 1.0 / ML_HEAD_DIM),
              _block_diag_const(RW_WIDTH, RW_HEAD_DIM, 1.0 / RW_HEAD_DIM),
              out_w[0].astype(BF16), w1[0], w2[0]]
    tokens = [h_f, h_b, o, y_f, y_b, bonus, gate]
    xs = _post_call(functools.partial(_rec_post_kernel, n_lat_tiles=n_lat_tiles), "rec_post", tokens, (x, ctx),
                    mods[0], consts, s, n_lat_tiles)

    w_att, qk_gain, wo = _att_weights(att_in_w[0], win_q_norm[0], win_k_norm[0], glb_q_norm[0], glb_k_norm[0], out_w[1])
    cos, sin = _rope_tables(t_lat, s)
    sq_avg = _block_diag_const(Q_HEADS * ATT_HEAD_DIM, ATT_HEAD_DIM, 1.0 / ATT_HEAD_DIM)
    qw, qg, kw, kg, vw, vg0, vg1 = _att_inproj(xs, mods[1], norm_mix[1].reshape(1, d), w_att, qk_gain, cos, sin,
                                               sq_avg, n_lat_tiles)
    sink_col = jnp.repeat(win_sink[0][np.array([kv * KV_GROUP + g for kv in range(KV_HEADS) for g in range(KV_GROUP)])],
                          Q_BLOCK).reshape(-1, 1)
    o_win, o_glb = _attention(qw, kw, vw, sink_col, qg, kg, vg0, vg1, t_lat)
    consts = [norm_ffn[1].reshape(1, d), wo, w1[1], w2[1]]
    return _post_call(_att_post_kernel, "att_post", [o_win, o_glb], (xs,), mods[1], consts, t_lat, n_lat_tiles)
```

```python
import functools

import numpy as np
import jax
import jax.numpy as jnp
from jax import lax
from jax.experimental import pallas as pl
from jax.experimental.pallas import tpu as pltpu

F32 = jnp.float32
BF16 = jnp.bfloat16

GRID_W = 64
NORM_EPS = 1e-6
GN_EPS = 64e-5
ROPE_THETA = 10000.0
ML_HEAD_DIM = 128
ML_HEADS = 4
ML_WIDTH = ML_HEADS * ML_HEAD_DIM
ML_CHUNK = 128
RW_HEAD_DIM = 64
RW_HEADS = 8
RW_WIDTH = RW_HEADS * RW_HEAD_DIM
RW_CHUNK = 64
RW_DECAY_RANK = 64
RW_ICLR_RANK = 64
RW_GATE_RANK = 128
RW_DECAY_SCALE = float(np.exp(-0.5))
ATT_HEAD_DIM = 64
Q_HEADS = 8
KV_HEADS = 2
KV_GROUP = Q_HEADS // KV_HEADS
KV_WIDTH = KV_HEADS * ATT_HEAD_DIM
WINDOW = 128
Q_BLOCK = 128

LANES = 128
SUBLANES = 8
VMEM_LIMIT_BYTES = 56 * 1024 * 1024
ROW_TILE = 256
HALO_ROWS = 16
POST_LATENT_TILES = (512, 256)
MASK_VALUE = -1e30


def _params(*semantics):
    return pltpu.CompilerParams(dimension_semantics=semantics, vmem_limit_bytes=VMEM_LIMIT_BYTES)


def _resident(shape):
    zeros = (0,) * len(shape)
    return pl.BlockSpec(shape, lambda *_: zeros, pipeline_mode=pl.Buffered(1))


def _dot(a, b):
    return jnp.dot(a, b, preferred_element_type=F32)


def _dot_nt(a, b):
    return lax.dot_general(a, b, (((1,), (1,)), ((), ())), preferred_element_type=F32)


def _dot_tn(a, b):
    return lax.dot_general(a, b, (((0,), (0,)), ((), ())), preferred_element_type=F32)


def _split(x, n):
    parts, rest = [], x
    for _ in range(n):
        p = rest.astype(BF16)
        parts.append(p)
        rest = rest - p.astype(F32)
    return parts


def _dot_split_lhs(x, m, n):
    return sum(_dot(p, m) for p in _split(x, n))


def _dot_split_rhs(m, x, n):
    return sum(_dot(m, p) for p in _split(x, n))


def _dot_f32(a, b):
    ah, al = _split(a, 2)
    bh, bl = _split(b, 2)
    return _dot(ah, bh) + _dot(ah, bl) + _dot(al, bh)


def _log_sigmoid(x):
    return jnp.minimum(x, 0.0) - jnp.log1p(jnp.exp(-jnp.abs(x)))


def _norm_mod(x, gain, shift, scale):
    y = x * lax.rsqrt(jnp.mean(x * x, axis=-1, keepdims=True) + NORM_EPS)
    return (y * gain) * (1.0 + scale) + shift


def _group_layernorm(y, avg, gain):
    yc = y - _dot(y.astype(BF16), avg)
    var = _dot((yc * yc).astype(BF16), avg)
    return yc * lax.rsqrt(var + GN_EPS) * gain


def _mod_kernel(cv_ref, w_ref, b_ref, o_ref):
    cv = cv_ref[...]
    o_ref[0] = _dot_f32(cv * jax.nn.sigmoid(cv), w_ref[0]) + b_ref[0]


def _modulation(cvec, mod_w, mod_b):
    depth, d, n = mod_w.shape
    tn = n // 4
    return pl.pallas_call(
        _mod_kernel,
        grid=(depth, n // tn),
        in_specs=[pl.BlockSpec((SUBLANES, d), lambda l, j: (0, 0)),
                  pl.BlockSpec((1, d, tn), lambda l, j: (l, 0, j)),
                  pl.BlockSpec((1, 1, tn), lambda l, j: (l, 0, j))],
        out_specs=pl.BlockSpec((1, SUBLANES, tn), lambda l, j: (l, 0, j)),
        out_shape=jax.ShapeDtypeStruct((depth, SUBLANES, n), F32),
        compiler_params=_params("parallel", "parallel"),
        name="adaln_modulation",
    )(cvec, mod_w, mod_b.reshape(depth, 1, n))


REC_Q, REC_V, REC_O, REC_RKV, REC_LORA, REC_GATE, REC_END = 0, 512, 1024, 1536, 3072, 3456, 3584
N_GATES = 4 * ML_HEADS


def _token_rows(x_ref, ctx_ref, n_lat_tiles):
    return jnp.where(pl.program_id(1) >= n_lat_tiles, ctx_ref[0], x_ref[0])


def _token_specs(tm, d, n_lat_tiles):
    return [pl.BlockSpec((1, tm, d), lambda b, i: (b, jnp.minimum(i, n_lat_tiles - 1), 0)),
            pl.BlockSpec((1, tm, d), lambda b, i: (b, jnp.maximum(i - n_lat_tiles, 0), 0))]


def _rec_inproj_kernel(x_ref, ctx_ref, mod_ref, g_ref, w_ref, wkt_ref, wgt_ref, gb_ref, gbt_ref,
                       q_ref, v_ref, o_ref, rkv_ref, lora_ref, gate_ref, kt_ref, gatet_ref, *, n_lat_tiles):
    d = x_ref.shape[-1]
    mod = mod_ref[0]
    h = _norm_mod(_token_rows(x_ref, ctx_ref, n_lat_tiles), g_ref[...], mod[:, 0:d], mod[:, d:2 * d]).astype(BF16)
    q_ref[0] = _dot(h, w_ref[:, REC_Q:REC_V]).astype(BF16)
    v_ref[0] = _dot(h, w_ref[:, REC_V:REC_O]).astype(BF16)
    o_ref[0] = _dot(h, w_ref[:, REC_O:REC_RKV]).astype(BF16)
    rkv_ref[0] = _dot(h, w_ref[:, REC_RKV:REC_LORA]).astype(BF16)
    lora_ref[0] = _dot(h, w_ref[:, REC_LORA:REC_GATE])
    gate_ref[0] = _dot(h, w_ref[:, REC_GATE:REC_END]) + gb_ref[...]
    kt_ref[0] = _dot_nt(wkt_ref[...], h).astype(BF16)
    gatet_ref[0] = _dot_nt(wgt_ref[...], h) + gbt_ref[...]


def _rec_inproj(x, ctx, mods, gain, w_main, wkt, wgt, gb_row, gb_col, n_lat_tiles):
    bsz, _, d = x.shape
    s = x.shape[1] + ctx.shape[1]
    tm = ROW_TILE
    tok = lambda c: pl.BlockSpec((1, tm, c), lambda b, i: (b, i, 0))
    mod_row = lambda b, i: (jnp.where(i >= n_lat_tiles, bsz, b), 0, 0)
    return pl.pallas_call(
        functools.partial(_rec_inproj_kernel, n_lat_tiles=n_lat_tiles),
        grid=(bsz, s // tm),
        in_specs=_token_specs(tm, d, n_lat_tiles) + [
                  pl.BlockSpec((1, 1, mods.shape[-1]), mod_row),
                  _resident((1, d)), _resident(w_main.shape), _resident(wkt.shape), _resident(wgt.shape),
                  _resident(gb_row.shape), _resident(gb_col.shape)],
        out_specs=[tok(ML_WIDTH), tok(ML_WIDTH), tok(ML_WIDTH), tok(3 * RW_WIDTH), tok(REC_GATE - REC_LORA),
                   tok(LANES),
                   pl.BlockSpec((1, ML_WIDTH, tm), lambda b, i: (b, 0, i)),
                   pl.BlockSpec((1, N_GATES, tm), lambda b, i: (b, 0, i))],
        out_shape=[jax.ShapeDtypeStruct((bsz, s, ML_WIDTH), BF16),
                   jax.ShapeDtypeStruct((bsz, s, ML_WIDTH), BF16),
                   jax.ShapeDtypeStruct((bsz, s, ML_WIDTH), BF16),
                   jax.ShapeDtypeStruct((bsz, s, 3 * RW_WIDTH), BF16),
                   jax.ShapeDtypeStruct((bsz, s, REC_GATE - REC_LORA), F32),
                   jax.ShapeDtypeStruct((bsz, s, LANES), F32),
                   jax.ShapeDtypeStruct((bsz, ML_WIDTH, s), BF16),
                   jax.ShapeDtypeStruct((bsz, N_GATES, s), F32)],
        compiler_params=_params("parallel", "parallel"),
        name="rec_inproj",
    )(x, ctx, mods, gain, w_main, wkt, wgt, gb_row, gb_col)


def _mlstm_stages(qf_ref, vf_ref, ktf_ref, gf_ref, gtf_ref, qb_ref, vb_ref, ktb_ref, gb_ref, gtb_ref,
                  hf_ref, hb_ref, c_ref, m_ref):
    L, dh = ML_CHUNK, ML_HEAD_DIM
    t_idx = lax.broadcasted_iota(jnp.int32, (L, L), 0)
    s_idx = lax.broadcasted_iota(jnp.int32, (L, L), 1)
    ones_col = (lax.broadcasted_iota(jnp.int32, (L, LANES), 1) == 0).astype(BF16)
    scale = dh ** -0.5
    streams = ((qf_ref, vf_ref, ktf_ref, gf_ref, gtf_ref, hf_ref),
               (qb_ref, vb_ref, ktb_ref, gb_ref, gtb_ref, hb_ref))
    chains = []
    for d, (q_ref, v_ref, kt_ref, g_ref, gt_ref, h_ref) in enumerate(streams):
        earlier = (s_idx <= t_idx) if d == 0 else (s_idx >= t_idx)
        tri = earlier.astype(BF16)
        tri_t = ((t_idx <= s_idx) if d == 0 else (t_idx >= s_idx)).astype(BF16)
        gt = gt_ref[0]
        lf_rows = _log_sigmoid(gt)
        b_cols = _dot_split_rhs(tri, _log_sigmoid(g_ref[0]), 3)
        b_rows = _dot_split_lhs(lf_rows, tri_t, 3)
        tot = jnp.sum(lf_rows, axis=1, keepdims=True)
        q, v, kt = q_ref[0], v_ref[0], kt_ref[0]
        for hh in range(ML_HEADS):
            ic, fc = 2 * ML_HEADS * d + hh, 2 * ML_HEADS * d + ML_HEADS + hh
            chains.append(dict(j=d * ML_HEADS + hh, hh=hh, h_ref=h_ref, earlier=earlier,
                               b_col=b_cols[:, fc:fc + 1], b_row=b_rows[fc:fc + 1, :], i_row=gt[ic:ic + 1, :],
                               b_end=tot[fc:fc + 1, :], q=q[:, hh * dh:(hh + 1) * dh], kt=kt[hh * dh:(hh + 1) * dh, :],
                               v_aug=jnp.concatenate([v[:, hh * dh:(hh + 1) * dh], ones_col], axis=1)))
    yield
    for ch in chains:
        ch["cst"] = c_ref[ch["j"]]
        ch["qk"] = _dot(ch["q"], ch["kt"])
        ch["qc"] = _dot(ch["q"], ch["cst"].astype(BF16))
    yield
    for ch in chains:
        m_prev = m_ref[ch["j"]:ch["j"] + 1, 0:1]
        d_in = jnp.where(ch["earlier"], ch["b_col"] - ch["b_row"] + ch["i_row"], MASK_VALUE)
        d_prev = ch["b_col"] + m_prev
        m_t = jnp.maximum(d_prev, jnp.max(d_in, axis=1, keepdims=True))
        ch["s"] = (ch["qk"] * (scale * jnp.exp(d_in - m_t))).astype(BF16)
        ch["w_prev"], ch["floor"] = jnp.exp(d_prev - m_t), jnp.exp(-m_t)
        d_end = ch["b_end"] - ch["b_row"] + ch["i_row"]
        m_new = jnp.maximum(ch["b_end"] + m_prev, jnp.max(d_end, axis=1, keepdims=True))
        ch["kw"] = (ch["kt"].astype(F32) * (jnp.exp(d_end - m_new) * scale)).astype(BF16)
        ch["decay"] = jnp.exp(ch["b_end"] + m_prev - m_new)
        m_ref[ch["j"]:ch["j"] + 1, :] = jnp.broadcast_to(m_new, (1, LANES))
    yield
    for ch in chains:
        acc = _dot(ch["s"], ch["v_aug"]) + ch["w_prev"] * ch["qc"]
        den = jnp.maximum(jnp.abs(acc[:, dh:dh + 1]), ch["floor"])
        ch["h_ref"][0, :, ch["hh"] * dh:(ch["hh"] + 1) * dh] = acc[:, :dh] / den
    yield
    for ch in chains:
        c_ref[ch["j"]] = ch["decay"] * ch["cst"] + _dot(ch["kw"], ch["v_aug"])


def _chunk_order(n_lat, n_ctx):
    n = n_lat + n_ctx
    fwd = lambda c: (c + n_lat) % n
    bwd = lambda c: n - 1 - c
    return fwd, bwd


def _rw_prep_kernel(z_ref, zp_ref, zn_ref, lora_ref, conv_ref, vec_ref, w2_ref, a2_ref, g2_ref, ones_ref,
                    r_ref, v_ref, kk_ref, bonus_ref, gate_ref, lw_ref, key_ref, bv_ref, *, t_lat, s_tot):
    tm, w = z_ref.shape[1], RW_WIDTH
    row0 = pl.program_id(1) * tm
    has_prev = jnp.where(jnp.logical_and(row0 != 0, row0 != t_lat), 1.0, 0.0)
    has_next = jnp.where(jnp.logical_and(row0 + tm != t_lat, row0 + tm != s_tot), 1.0, 0.0)
    z = z_ref[0].astype(F32)
    prev_row = zp_ref[0][HALO_ROWS - 1:HALO_ROWS, :].astype(F32) * has_prev
    next_row = zn_ref[0][0:1, :].astype(F32) * has_next
    ridx = lax.broadcasted_iota(jnp.int32, z.shape, 0)
    z_prev = jnp.where(ridx == 0, prev_row, pltpu.roll(z, 1, 0))
    z_next = jnp.where(ridx == tm - 1, next_row, pltpu.roll(z, tm - 1, 0))
    cw = conv_ref[...]
    zc = cw[0:1] * z_prev + cw[1:2] * z + cw[2:3] * z_next
    r, kr, vr = zc[:, :w], zc[:, w:2 * w], zc[:, 2 * w:]
    vec = vec_ref[...]
    k_k, k_a, r_k = vec[0:1], vec[1:2], vec[2:3]
    ones = ones_ref[...]
    kkr = kr * k_k
    kk = kkr * lax.rsqrt(_dot((kkr * kkr).astype(BF16), ones) + NORM_EPS)
    r_ref[0] = r.astype(BF16)
    v_ref[0] = vr.astype(BF16)
    kk_ref[0] = kk.astype(BF16)
    bonus_ref[0] = (_dot((r * kr * r_k).astype(BF16), ones) * vr).astype(BF16)
    lora = lora_ref[0]
    dec = _dot(jnp.tanh(lora[:, 0:LANES]).astype(BF16), w2_ref[...])
    icl = _dot(lora[:, LANES:2 * LANES].astype(BF16), a2_ref[...])
    gate_ref[0] = _dot(jax.nn.sigmoid(lora[:, 2 * LANES:3 * LANES]).astype(BF16), g2_ref[...]).astype(BF16)
    for d in range(2):
        a = jax.nn.sigmoid(vec[5 + d:6 + d] + icl[:, d * w:(d + 1) * w])
        lw_ref[d, 0] = -RW_DECAY_SCALE * jax.nn.sigmoid(vec[3 + d:4 + d] + dec[:, d * w:(d + 1) * w])
        key_ref[d, 0] = (kr * (1.0 + (a - 1.0) * k_a)).astype(BF16)
        bv_ref[d, 0] = (kk * a).astype(BF16)


def _rw_prep(rkv, lora, conv_w, vecs, w2cat, a2cat, g2, ones64, t_lat):
    bsz, s, _ = rkv.shape
    tm, w = ROW_TILE, RW_WIDTH
    nsub = tm // HALO_ROWS
    tok = lambda c: pl.BlockSpec((1, tm, c), lambda b, i: (b, i, 0))
    dirtok = pl.BlockSpec((2, 1, tm, w), lambda b, i: (0, b, i, 0))
    last = s // HALO_ROWS - 1
    return pl.pallas_call(
        functools.partial(_rw_prep_kernel, t_lat=t_lat, s_tot=s),
        grid=(bsz, s // tm),
        in_specs=[tok(3 * w),
                  pl.BlockSpec((1, HALO_ROWS, 3 * w), lambda b, i: (b, jnp.maximum(i * nsub - 1, 0), 0)),
                  pl.BlockSpec((1, HALO_ROWS, 3 * w), lambda b, i: (b, jnp.minimum((i + 1) * nsub, last), 0)),
                  tok(lora.shape[-1]),
                  _resident(conv_w.shape), _resident(vecs.shape), _resident(w2cat.shape), _resident(a2cat.shape),
                  _resident(g2.shape), _resident(ones64.shape)],
        out_specs=[tok(w)] * 5 + [dirtok] * 3,
        out_shape=[jax.ShapeDtypeStruct((bsz, s, w), BF16)] * 5
        + [jax.ShapeDtypeStruct((2, bsz, s, w), dt) for dt in (F32, BF16, BF16)],
        compiler_params=_params("parallel", "parallel"),
        name="rwkv_prep",
    )(rkv, rkv, rkv, lora, conv_w, vecs, w2cat, a2cat, g2, ones64)


def _rw_scan_stages(rf_ref, vf_ref, kkf_ref, lwf_ref, keyf_ref, bvf_ref, rb_ref, vb_ref, kkb_ref, lwb_ref, keyb_ref,
                    bvb_ref, yf_ref, yb_ref, s_ref, step):
    C, n = RW_CHUNK, RW_HEAD_DIM
    t2 = lax.broadcasted_iota(jnp.int32, (2 * C, 2 * C), 0) % C
    s2 = lax.broadcasted_iota(jnp.int32, (2 * C, 2 * C), 1) % C
    lower = lax.broadcasted_iota(jnp.int32, (2 * C, 2 * C), 0) >= C

    chains = []
    rows_of = (slice(step * C, (step + 1) * C), slice((1 - step) * C, (2 - step) * C))
    for d, (r_ref, v_ref, kk_ref, lw_ref, key_ref, bv_ref) in enumerate(
            ((rf_ref, vf_ref, kkf_ref, lwf_ref, keyf_ref, bvf_ref), (rb_ref, vb_ref, kkb_ref, lwb_ref, keyb_ref, bvb_ref))):
        rows = rows_of[d]
        rel = (s2 - t2) if d == 0 else (t2 - s2)
        keep = rel < jnp.where(lower, 1, 0)
        tri = (rel[:C, :C] <= 0).astype(BF16)
        lw = lw_ref[0, 0, rows, :]
        cum = _dot_split_rhs(tri, lw, 3)
        tot = jnp.sum(lw, axis=0, keepdims=True)
        r, v, kk, key, bv = (a.astype(F32) for a in (r_ref[0, rows, :], v_ref[0, rows, :], kk_ref[0, rows, :],
                                                     key_ref[0, 0, rows, :], bv_ref[0, 0, rows, :]))
        e_neg = jnp.exp(-cum)
        e_end = jnp.exp(tot - cum)
        alpha = (kk * jnp.exp(cum - lw)).astype(BF16)
        rho = (r * jnp.exp(cum)).astype(BF16)
        beta = (bv * e_neg).astype(BF16)
        kappa = (key * e_neg).astype(BF16)
        beta_end = (bv * e_end).astype(BF16)
        kappa_end = (key * e_end).astype(BF16)
        gamma = jnp.exp(tot)
        vb = v.astype(BF16)
        for h in range(RW_HEADS):
            sl = slice(h * n, (h + 1) * n)
            chains.append(dict(j=d * RW_HEADS + h, keep=keep, a=alpha[:, sl], rho=rho[:, sl], v=vb[:, sl],
                               lhs=jnp.concatenate([alpha[:, sl], rho[:, sl]], axis=0),
                               rhs=jnp.concatenate([beta[:, sl], kappa[:, sl]], axis=0),
                               end=jnp.concatenate([beta_end[:, sl], kappa_end[:, sl]], axis=0),
                               gamma=gamma[:, sl]))
    yield
    for ch in chains:
        ch["big"] = jnp.where(ch["keep"], _dot_nt(ch["lhs"], ch["rhs"]), 0.0).astype(BF16)
    yield
    for ch in chains:
        ch["lmv"] = _dot(ch["big"][:, C:], ch["v"])
    yield
    tt = lax.broadcasted_iota(jnp.int32, (C, C), 0)
    ss = lax.broadcasted_iota(jnp.int32, (C, C), 1)
    eye = (tt == ss).astype(F32)
    for ch in chains:
        ch["lab"] = ch["big"][:C, :C]
        ch["inv"] = eye - jnp.where((tt >> 1) == (ss >> 1), ch["lab"], 0.0).astype(F32)
    for k in range(1, 6):
        couples = jnp.logical_and((tt >> (k + 1)) == (ss >> (k + 1)), (tt >> k) != (ss >> k))
        for ch in chains:
            ch["ed"] = _dot(jnp.where(couples, ch["lab"], 0.0).astype(BF16), ch["inv"].astype(BF16)).astype(BF16)
        yield
        for ch in chains:
            ch["inv"] = ch["inv"] - _dot(ch["inv"].astype(BF16), ch["ed"])
        yield
    for ch in chains:
        x = -jnp.concatenate([ch["a"].astype(F32), ch["lmv"][:C]], axis=1)
        ch["x"] = _dot(ch["inv"].astype(BF16), x.astype(BF16))
    yield
    for ch in chains:
        ch["st"] = s_ref[ch["j"]]
        ch["ws"] = _dot_nt(jnp.concatenate([ch["x"][:, :n].astype(BF16), ch["rho"]], axis=0), ch["st"].astype(BF16))
    yield
    for ch in chains:
        ch["u"] = (ch["ws"][:C] + ch["x"][:, n:]).astype(BF16)
    for ch in chains:
        ch["y"] = ch["ws"][C:] + _dot(ch["big"][C:, :C], ch["u"]) + ch["lmv"][C:]
    yield
    for ch in chains:
        upd = _dot_tn(jnp.concatenate([ch["u"], ch["v"]], axis=0), ch["end"])
        s_ref[ch["j"]] = ch["st"] * ch["gamma"] + upd
    yf_ref[0, rows_of[0], :] = jnp.concatenate([ch["y"] for ch in chains[:RW_HEADS]], axis=1)
    yb_ref[0, rows_of[1], :] = jnp.concatenate([ch["y"] for ch in chains[RW_HEADS:]], axis=1)


RW_STAGES_PER_ML_STAGE = 3


def _rec_scan_kernel(*refs):
    rw_in, ml_in = refs[:12], refs[12:22]
    yf_ref, yb_ref, hf_ref, hb_ref, s_ref, c_ref, m_ref = refs[22:]

    @pl.when(pl.program_id(1) == 0)
    def _():
        s_ref[...] = jnp.zeros_like(s_ref)
        c_ref[...] = jnp.zeros_like(c_ref)
        m_ref[...] = jnp.zeros_like(m_ref)

    ml = _mlstm_stages(*ml_in, hf_ref, hb_ref, c_ref, m_ref)
    n_stage = 0
    for step in range(ML_CHUNK // RW_CHUNK):
        for _ in _rw_scan_stages(*rw_in, yf_ref, yb_ref, s_ref, step):
            n_stage += 1
            if n_stage % RW_STAGES_PER_ML_STAGE == 0:
                next(ml, None)
    for _ in ml:
        pass


def _rec_scan(r, v, kk, lw, key, bv, q, vm, kt, gates, gates_t, t_lat):
    bsz, s, w = r.shape
    L = ML_CHUNK
    orders = _chunk_order(t_lat // L, (s - t_lat) // L)
    rw_specs, ml_specs, out_rw, out_ml = [], [], [], []
    for d, order in enumerate(orders):
        tok = lambda c, order=order: pl.BlockSpec((1, L, c), lambda b, i: (b, order(i), 0))
        dirtok = pl.BlockSpec((1, 1, L, w), lambda b, i, order=order, d=d: (d, b, order(i), 0))
        timelast = lambda c, order=order: pl.BlockSpec((1, c, L), lambda b, i: (b, 0, order(i)))
        rw_specs += [tok(w), tok(w), tok(w), dirtok, dirtok, dirtok]
        ml_specs += [tok(ML_WIDTH), tok(ML_WIDTH), timelast(ML_WIDTH), tok(LANES), timelast(N_GATES)]
        out_rw.append(tok(w))
        out_ml.append(tok(ML_WIDTH))
    return pl.pallas_call(
        _rec_scan_kernel,
        grid=(bsz, s // L),
        in_specs=rw_specs + ml_specs,
        out_specs=out_rw + out_ml,
        out_shape=[jax.ShapeDtypeStruct((bsz, s, w), F32)] * 2 + [jax.ShapeDtypeStruct((bsz, s, ML_WIDTH), F32)] * 2,
        scratch_shapes=[pltpu.VMEM((2 * RW_HEADS, RW_HEAD_DIM, RW_HEAD_DIM), F32),
                        pltpu.VMEM((2 * ML_HEADS, ML_HEAD_DIM, ML_HEAD_DIM + LANES), F32),
                        pltpu.VMEM((2 * ML_HEADS, LANES), F32)],
        compiler_params=_params("parallel", "arbitrary"),
        name="recurrent_scan",
    )(r, v, kk, lw, key, bv, r, v, kk, lw, key, bv, q, vm, kt, gates, gates_t, q, vm, kt, gates, gates_t)


def _residual_mlp(mix, x, mod, g_ffn, wo_ref, w1_ref, w2_ref, out_ref):
    d = x.shape[-1]
    x1 = x + mod[:, 2 * d:3 * d] * _dot(mix, wo_ref[...])
    h2 = _norm_mod(x1, g_ffn, mod[:, 3 * d:4 * d], mod[:, 4 * d:5 * d]).astype(BF16)
    acc = jnp.zeros_like(x1)
    for k in range(w1_ref.shape[1] // d):
        hid = jnp.maximum(_dot(h2, w1_ref[:, k * d:(k + 1) * d]), 0.0)
        acc = acc + _dot((hid * hid).astype(BF16), w2_ref[k * d:(k + 1) * d, :])
    out_ref[0] = x1 + mod[:, 5 * d:6 * d] * acc


def _rec_post_kernel(hf_ref, hb_ref, o_ref, yf_ref, yb_ref, bonus_ref, gate_ref, x_ref, ctx_ref, mod_ref, gf_ref,
                     mlg_ref, rwg_ref, avg_ml_ref, avg_rw_ref, wo_ref, w1_ref, w2_ref, out_ref, *, n_lat_tiles):
    ml = _group_layernorm(hf_ref[0] + hb_ref[0], avg_ml_ref[...], mlg_ref[...]) * jax.nn.sigmoid(o_ref[0].astype(F32))
    rw = (_group_layernorm(yf_ref[0] + yb_ref[0], avg_rw_ref[...], rwg_ref[...]) + bonus_ref[0]) * gate_ref[0]
    mix = jnp.concatenate([ml, rw], axis=1).astype(BF16)
    x = _token_rows(x_ref, ctx_ref, n_lat_tiles)
    _residual_mlp(mix, x, mod_ref[0], gf_ref[...], wo_ref, w1_ref, w2_ref, out_ref)


def _att_post_kernel(ow_ref, og_ref, x_ref, mod_ref, gf_ref, wo_ref, w1_ref, w2_ref, out_ref):
    mix = jnp.concatenate([ow_ref[0], og_ref[0]], axis=1)
    _residual_mlp(mix, x_ref[0], mod_ref[0], gf_ref[...], wo_ref, w1_ref, w2_ref, out_ref)


def _post_call(kernel, name, token_inputs, residual, mods, consts, n_rows, n_lat_tiles, tm=ROW_TILE):
    bsz, _, d = residual[0].shape
    mod_row = lambda b, i: (jnp.where(i >= n_lat_tiles, bsz, b), 0, 0)
    tok = lambda c: pl.BlockSpec((1, tm, c), lambda b, i: (b, i, 0))
    in_specs = [tok(arr.shape[-1]) for arr in token_inputs]
    in_specs += _token_specs(tm, d, n_lat_tiles) if len(residual) == 2 else [tok(d)]
    in_specs += [pl.BlockSpec((1, 1, mods.shape[-1]), mod_row)]
    in_specs += [_resident(c.shape) for c in consts]
    args = list(token_inputs) + list(residual)
    return pl.pallas_call(
        kernel,
        grid=(bsz, n_rows // tm),
        in_specs=in_specs,
        out_specs=pl.BlockSpec((1, tm, d), lambda b, i: (b, i, 0)),
        out_shape=jax.ShapeDtypeStruct((bsz, n_rows, d), F32),
        compiler_params=_params("parallel", "parallel"),
        name=name,
    )(*args, mods, *consts)


ATT_QW, ATT_QG, ATT_KW, ATT_KG, ATT_VW, ATT_VG, ATT_END = 0, 512, 1024, 1152, 1280, 1408, 1536
ATT_QK_END = ATT_VW


def _att_inproj_kernel(x_ref, mod_ref, g_ref, w_ref, qkg_ref, cos_ref, sin_ref, sq_ref,
                       qw_ref, qg_ref, kw_ref, kg_ref, vw_ref, vg0_ref, vg1_ref):
    d = x_ref.shape[-1]
    mod = mod_ref[0]
    h = _norm_mod(x_ref[0], g_ref[...], mod[:, 0:d], mod[:, d:2 * d]).astype(BF16)
    half = ATT_HEAD_DIM // 2
    cos2, sin2 = cos_ref[0], sin_ref[0]

    def norm_rope(lo, hi):
        z = _dot(h, w_ref[:, lo:hi])
        n = hi - lo
        sq = sq_ref[...]
        z = z * lax.rsqrt(_dot((z * z).astype(BF16), sq[:n, :n]) + NORM_EPS) * qkg_ref[:, lo:hi]
        reps = n // LANES
        cos = jnp.concatenate([cos2] * reps, axis=1)
        sin = jnp.concatenate([sin2] * reps, axis=1)
        lane = lax.broadcasted_iota(jnp.int32, z.shape, 1)
        partner = jnp.where(lane % ATT_HEAD_DIM < half, pltpu.roll(z, n - half, 1), pltpu.roll(z, half, 1))
        return (z * cos + partner * sin).astype(BF16)

    qw_ref[0] = norm_rope(ATT_QW, ATT_QG)
    qg_ref[0] = norm_rope(ATT_QG, ATT_KW)
    kk = norm_rope(ATT_KW, ATT_VW)
    kw_ref[0] = kk[:, :KV_WIDTH]
    kg_ref[0] = kk[:, KV_WIDTH:]
    vv = _dot(h, w_ref[:, ATT_VW:ATT_END])
    vw_ref[0] = vv[:, :KV_WIDTH].astype(BF16)
    vg = vv[:, KV_WIDTH:]
    lane = lax.broadcasted_iota(jnp.int32, vg.shape, 1)
    vg0_ref[0] = jnp.where(lane < ATT_HEAD_DIM, vg, jnp.where(lane == ATT_HEAD_DIM, 1.0, 0.0)).astype(BF16)
    vg1_ref[0] = jnp.where(lane >= ATT_HEAD_DIM, vg, jnp.where(lane == 0, 1.0, 0.0)).astype(BF16)


def _att_inproj(xs, mods, gain, w, qk_gain, cos, sin, sq_avg, n_lat_tiles):
    bsz, s, d = xs.shape
    tm = ROW_TILE
    tok = lambda c: pl.BlockSpec((1, tm, c), lambda b, i: (b, i, 0))
    mod_row = lambda b, i: (jnp.where(i >= n_lat_tiles, bsz, b), 0, 0)
    rope = pl.BlockSpec((1, tm, LANES), lambda b, i: (0, i, 0))
    qw = Q_HEADS * ATT_HEAD_DIM
    return pl.pallas_call(
        _att_inproj_kernel,
        grid=(bsz, s // tm),
        in_specs=[tok(d), pl.BlockSpec((1, 1, mods.shape[-1]), mod_row), _resident((1, d)), _resident(w.shape),
                  _resident(qk_gain.shape), rope, rope, _resident(sq_avg.shape)],
        out_specs=[tok(qw), tok(qw)] + [tok(KV_WIDTH)] * 5,
        out_shape=[jax.ShapeDtypeStruct((bsz, s, c), BF16) for c in (qw, qw) + (KV_WIDTH,) * 5],
        compiler_params=_params("parallel", "parallel"),
        name="att_inproj",
    )(xs, mods, gain, w, qk_gain, cos, sin, sq_avg)


def _stack_queries(q):
    lane = lax.broadcasted_iota(jnp.int32, (q.shape[0], LANES), 1)
    zero = jnp.zeros((q.shape[0], LANES), q.dtype)
    slabs = [q[:, g * LANES:(g + 1) * LANES] for g in range(KV_GROUP)]
    lo = [jnp.where(lane < ATT_HEAD_DIM, sl, zero) for sl in slabs]
    hi = [jnp.where(lane >= ATT_HEAD_DIM, sl, zero) for sl in slabs]
    return jnp.concatenate(lo + hi, axis=0)


def _unstack_outputs(o, rows):
    lane = lax.broadcasted_iota(jnp.int32, (rows, LANES), 1)
    slabs = [jnp.where(lane < ATT_HEAD_DIM, o[g * rows:(g + 1) * rows], o[(KV_GROUP + g) * rows:(KV_GROUP + g + 1) * rows])
             for g in range(KV_GROUP)]
    return jnp.concatenate(slabs, axis=1)


def _attn_kernel(qw_ref, kp_ref, ko_ref, kn_ref, kc_ref, vp_ref, vo_ref, vn_ref, vc_ref, sink_ref,
                 qg_ref, k_ref, v0_ref, v1_ref, ow_ref, og_ref, *, tk):
    L = Q_BLOCK
    i, nb = pl.program_id(1), pl.num_programs(1)
    lc = kc_ref.shape[1]
    qs_w = _stack_queries(qw_ref[0])
    keys = jnp.concatenate([kp_ref[0], ko_ref[0], kn_ref[0], kc_ref[0]], axis=0)
    vals = jnp.concatenate([vp_ref[0], vo_ref[0], vn_ref[0], vc_ref[0]], axis=0)
    s_w = _dot_nt(qs_w, keys)

    qs = _stack_queries(qg_ref[0])
    rows = qs.shape[0]
    half = rows // 2
    n_tiles = k_ref.shape[1] // tk
    scores = lambda j: _dot_nt(qs, k_ref[0, j * tk:(j + 1) * tk, :])
    s_next = scores(0)

    nk = 3 * L + lc
    row = lax.broadcasted_iota(jnp.int32, (L, nk), 0)
    col = lax.broadcasted_iota(jnp.int32, (L, nk), 1)
    off = col - L - row
    valid = jnp.logical_and(off <= WINDOW, off >= -WINDOW)
    valid = jnp.logical_and(valid, col >= jnp.where(i > 0, 0, L))
    valid = jnp.logical_and(valid, col < jnp.where(i < nb - 1, 3 * L, 2 * L))
    valid = jnp.logical_or(valid, col >= 3 * L)
    valid = jnp.concatenate([valid] * (2 * KV_GROUP), axis=0)
    s_w = jnp.where(valid, s_w, MASK_VALUE)
    sink = sink_ref[...]
    m_w = jnp.maximum(jnp.max(s_w, axis=1, keepdims=True), sink)
    p_w = jnp.exp(s_w - m_w)
    den_w = jnp.sum(p_w, axis=1, keepdims=True) + jnp.exp(sink - m_w)
    ow_ref[0] = _unstack_outputs(_dot(p_w.astype(BF16), vals) / den_w, L).astype(BF16)

    m = jnp.full((rows, 1), MASK_VALUE, F32)
    acc = jnp.zeros((rows, LANES), F32)
    for j in range(n_tiles):
        s = s_next
        if j + 1 < n_tiles:
            s_next = scores(j + 1)
        m_new = jnp.maximum(m, jnp.max(s, axis=1, keepdims=True))
        p = jnp.exp2(s - m_new).astype(BF16)
        pv = jnp.concatenate([_dot(p[:half], v0_ref[0, j * tk:(j + 1) * tk, :]),
                              _dot(p[half:], v1_ref[0, j * tk:(j + 1) * tk, :])], axis=0)
        acc = jnp.exp2(m - m_new) * acc + pv
        m = m_new
    den = jnp.concatenate([acc[:half, ATT_HEAD_DIM:ATT_HEAD_DIM + 1], acc[half:, 0:1]], axis=0)
    og_ref[0] = _unstack_outputs(acc / den, L).astype(BF16)


GLB_KEY_TILES = (2816, 1408, 768, 512, 384, 256, 128)


def _attention(qw, kw, vw, sink_col, qg, kg, vg0, vg1, t_lat):
    bsz, s, width = qw.shape
    L = Q_BLOCK
    nb = t_lat // L
    lc = s - t_lat
    tk = next(t for t in GLB_KEY_TILES if s % t == 0)
    qblk = pl.BlockSpec((1, L, width), lambda b, i: (b, i, 0))
    kv = lambda f: pl.BlockSpec((1, L, KV_WIDTH), lambda b, i: (b, f(i), 0))
    ctx = pl.BlockSpec((1, lc, KV_WIDTH), lambda b, i: (b, t_lat // lc, 0))
    nbr = [kv(lambda i: jnp.maximum(i - 1, 0)), kv(lambda i: i), kv(lambda i: jnp.minimum(i + 1, nb - 1)), ctx]
    whole = pl.BlockSpec((1, s, KV_WIDTH), lambda b, i: (b, 0, 0))
    return pl.pallas_call(
        functools.partial(_attn_kernel, tk=tk),
        grid=(bsz, nb),
        in_specs=[qblk] + nbr + nbr + [_resident(sink_col.shape), qblk, whole, whole, whole],
        out_specs=[qblk, qblk],
        out_shape=[jax.ShapeDtypeStruct((bsz, t_lat, width), BF16)] * 2,
        compiler_params=_params("parallel", "parallel"),
        name="attention",
    )(qw, kw, kw, kw, kw, vw, vw, vw, vw, sink_col, qg, kg, vg0, vg1)


def _block_diag_const(width, group, value):
    idx = np.arange(width) // group
    return jnp.asarray((idx[:, None] == idx[None, :]) * value, dtype=BF16)


def _head_perm():
    return np.concatenate([np.arange(0, ATT_HEAD_DIM, 2), np.arange(1, ATT_HEAD_DIM, 2)])


def _q_head_order():
    return [kv * KV_GROUP + g for g in range(KV_GROUP) for kv in range(KV_HEADS)]


def _rec_weights(w, gate_b, conv_w, w0, w2, a0, a2, k_k, k_a, r_k):
    o = np.cumsum((0,) + (ML_WIDTH,) * 4 + (ML_HEADS,) * 4 + (RW_WIDTH,) * 3
                  + (RW_DECAY_RANK,) * 2 + (RW_ICLR_RANK,) * 2 + (RW_GATE_RANK,))
    q, k, v, og = (w[:, o[i]:o[i + 1]] for i in range(4))
    gates = w[:, o[4]:o[8]]
    rkv = w[:, o[8]:o[11]]
    lora = w[:, o[11]:o[16]]
    gates_pad = jnp.pad(gates, ((0, 0), (0, LANES - N_GATES)))
    w_main = jnp.concatenate([q, v, og, rkv, lora, gates_pad], axis=1).astype(BF16)
    gb = gate_b.reshape(-1)
    gb_row = jnp.pad(gb, (0, LANES - N_GATES)).reshape(1, LANES)
    zeros = jnp.zeros((RW_DECAY_RANK, RW_WIDTH), F32)
    w2cat = jnp.concatenate([jnp.concatenate([w2[0], zeros], 1), jnp.concatenate([zeros, w2[1]], 1)], 0).astype(BF16)
    a2cat = jnp.concatenate([jnp.concatenate([a2[0], zeros], 1), jnp.concatenate([zeros, a2[1]], 1)], 0).astype(BF16)
    vecs = jnp.stack([k_k, k_a, r_k, w0[0], w0[1], a0[0], a0[1], jnp.zeros_like(k_k)])
    return dict(w_main=w_main, wkt=k.T.astype(BF16), wgt=gates.T.astype(BF16), gb_row=gb_row,
                gb_col=gb.reshape(N_GATES, 1), conv=conv_w, vecs=vecs, w2cat=w2cat, a2cat=a2cat)


def _att_weights(w, win_q_norm, win_k_norm, glb_q_norm, glb_k_norm, w_out):
    hd, perm = ATT_HEAD_DIM, _head_perm()
    qn, kn = Q_HEADS * hd, KV_HEADS * hd
    o = np.cumsum((0, qn, kn, kn, qn, kn, kn))
    qw, kw, vw, qg, kg, vg = (w[:, o[i]:o[i + 1]] for i in range(6))
    q_cols = np.concatenate([h * hd + perm for h in _q_head_order()])
    k_cols = np.concatenate([h * hd + perm for h in range(KV_HEADS)])
    w_all = jnp.concatenate([qw[:, q_cols], qg[:, q_cols], kw[:, k_cols], kg[:, k_cols], vw, vg], axis=1).astype(BF16)
    scale = hd ** -0.5
    log2e = float(np.log2(np.e))
    gains = jnp.concatenate([jnp.tile(win_q_norm[perm], Q_HEADS) * scale,
                             jnp.tile(glb_q_norm[perm], Q_HEADS) * (scale * log2e),
                             jnp.tile(win_k_norm[perm], KV_HEADS), jnp.tile(glb_k_norm[perm], KV_HEADS)]).reshape(1, -1)
    out_rows = np.concatenate([h * hd + np.arange(hd) for h in _q_head_order()])
    wo = jnp.concatenate([w_out[:qn][out_rows], w_out[qn:][out_rows]], axis=0).astype(BF16)
    return w_all, gains, wo


def _rope_tables(t_lat, s):
    n_freq = ATT_HEAD_DIM // 4
    pos = jnp.arange(t_lat)
    inv_freq = ROPE_THETA ** (-jnp.arange(n_freq, dtype=F32) / n_freq)
    ang = jnp.concatenate([(pos // GRID_W).astype(F32)[:, None] * inv_freq,
                           (pos % GRID_W).astype(F32)[:, None] * inv_freq], axis=-1)
    cos, sin = jnp.cos(ang), jnp.sin(ang)
    cos_h = jnp.concatenate([cos, cos], axis=-1)
    sin_h = jnp.concatenate([-sin, sin], axis=-1)
    pad = ((0, s - t_lat), (0, 0))
    cos_h = jnp.pad(cos_h, pad, constant_values=1.0)
    sin_h = jnp.pad(sin_h, pad)
    return jnp.tile(cos_h, (1, 2))[None], jnp.tile(sin_h, (1, 2))[None]


def kernel(x, c, ctx, c_ctx, norm_mix, norm_ffn, mod_w, mod_b, out_w, ffn_w1, ffn_w2, rec_in_w, ml_gate_b, ml_norm_g,
           rw_conv, rw_w0, rw_w2, rw_a0, rw_a2, rw_g2, rw_kk, rw_ka, rw_rk, rw_norm_g, att_in_w, win_q_norm,
           win_k_norm, win_sink, glb_q_norm, glb_k_norm):
    bsz, t_lat, d = x.shape
    lc = ctx.shape[1]
    s = t_lat + lc
    depth = mod_w.shape[0]
    assert depth == 2 and bsz + 1 <= SUBLANES
    assert t_lat % ROW_TILE == 0 and lc % ROW_TILE == 0 and t_lat % lc == 0 and t_lat % GRID_W == 0
    n_lat_tiles = t_lat // ROW_TILE

    cvec = jnp.zeros((SUBLANES, d), F32).at[:bsz].set(c).at[bsz].set(c_ctx)
    mods = _modulation(cvec, mod_w, mod_b)[:, :bsz + 1].reshape(depth, bsz + 1, 1, 6 * d)
    w1 = ffn_w1.astype(BF16)
    w2 = ffn_w2.astype(BF16)

    rw = _rec_weights(rec_in_w[0], ml_gate_b[0], rw_conv[0], rw_w0[0], rw_w2[0], rw_a0[0], rw_a2[0],
                      rw_kk[0], rw_ka[0], rw_rk[0])
    q, v, o, rkv, lora, gates, kt, gates_t = _rec_inproj(
        x, ctx, mods[0], norm_mix[0].reshape(1, d), rw["w_main"], rw["wkt"], rw["wgt"], rw["gb_row"], rw["gb_col"],
        n_lat_tiles)
    ones64 = _block_diag_const(RW_WIDTH, RW_HEAD_DIM, 1.0)
    r, vr, kk, bonus, gate, lw, key, bv = _rw_prep(rkv, lora, rw["conv"], rw["vecs"], rw["w2cat"], rw["a2cat"],
                                                   rw_g2[0].astype(BF16), ones64, t_lat)
    y_f, y_b, h_f, h_b = _rec_scan(r, vr, kk, lw, key, bv, q, v, kt, gates, gates_t, t_lat)
    consts = [norm_ffn[0].reshape(1, d), ml_norm_g[0].reshape(1, -1), rw_norm_g[0].reshape(1, -1),
              _block_diag_const(ML_WIDTH, ML_HEAD_DIM, 1.0 / ML_HEAD_DIM),
              _block_diag_const(RW_WIDTH, RW_HEAD_DIM, 1.0 / RW_HEAD_DIM),
              out_w[0].astype(BF16), w1[0], w2[0]]
    tokens = [h_f, h_b, o, y_f, y_b, bonus, gate]
    xs = _post_call(functools.partial(_rec_post_kernel, n_lat_tiles=n_lat_tiles), "rec_post", tokens, (x, ctx),
                    mods[0], consts, s, n_lat_tiles)

    w_att, qk_gain, wo = _att_weights(att_in_w[0], win_q_norm[0], win_k_norm[0], glb_q_norm[0], glb_k_norm[0], out_w[1])
    cos, sin = _rope_tables(t_lat, s)
    sq_avg = _block_diag_const(Q_HEADS * ATT_HEAD_DIM, ATT_HEAD_DIM, 1.0 / ATT_HEAD_DIM)
    qw, qg, kw, kg, vw, vg0, vg1 = _att_inproj(xs, mods[1], norm_mix[1].reshape(1, d), w_att, qk_gain, cos, sin,
                                               sq_avg, n_lat_tiles)
    sink_col = jnp.repeat(win_sink[0][np.array([kv * KV_GROUP + g for kv in range(KV_HEADS) for g in range(KV_GROUP)])],
                          Q_BLOCK).reshape(-1, 1)
    o_win, o_glb = _attention(qw, kw, vw, sink_col, qg, kg, vg0, vg1, t_lat)
    consts = [norm_ffn[1].reshape(1, d), wo, w1[1], w2[1]]
    tm = next(t for t in POST_LATENT_TILES if t_lat % t == 0)
    return _post_call(_att_post_kernel, "att_post", [o_win, o_glb], (xs,), mods[1], consts, t_lat, t_lat // tm, tm)
```

```python
import functools

import numpy as np
import jax
import jax.numpy as jnp
from jax import lax
from jax.experimental import pallas as pl
from jax.experimental.pallas import tpu as pltpu

F32 = jnp.float32
BF16 = jnp.bfloat16

GRID_W = 64
NORM_EPS = 1e-6
GN_EPS = 64e-5
ROPE_THETA = 10000.0
ML_HEAD_DIM = 128
ML_HEADS = 4
ML_WIDTH = ML_HEADS * ML_HEAD_DIM
ML_CHUNK = 128
RW_HEAD_DIM = 64
RW_HEADS = 8
RW_WIDTH = RW_HEADS * RW_HEAD_DIM
RW_CHUNK = 64
RW_DECAY_RANK = 64
RW_ICLR_RANK = 64
RW_GATE_RANK = 128
RW_DECAY_SCALE = float(np.exp(-0.5))
ATT_HEAD_DIM = 64
Q_HEADS = 8
KV_HEADS = 2
KV_GROUP = Q_HEADS // KV_HEADS
KV_WIDTH = KV_HEADS * ATT_HEAD_DIM
WINDOW = 128
Q_BLOCK = 128

LANES = 128
SUBLANES = 8
VMEM_LIMIT_BYTES = 56 * 1024 * 1024
ROW_TILE = 256
HALO_ROWS = 16
POST_LATENT_TILES = (512, 256)
MASK_VALUE = -1e30


def _params(*semantics):
    return pltpu.CompilerParams(dimension_semantics=semantics, vmem_limit_bytes=VMEM_LIMIT_BYTES)


def _resident(shape):
    zeros = (0,) * len(shape)
    return pl.BlockSpec(shape, lambda *_: zeros, pipeline_mode=pl.Buffered(1))


def _dot(a, b):
    return jnp.dot(a, b, preferred_element_type=F32)


def _dot_nt(a, b):
    return lax.dot_general(a, b, (((1,), (1,)), ((), ())), preferred_element_type=F32)


def _dot_tn(a, b):
    return lax.dot_general(a, b, (((0,), (0,)), ((), ())), preferred_element_type=F32)


def _split(x, n):
    parts, rest = [], x
    for _ in range(n):
        p = rest.astype(BF16)
        parts.append(p)
        rest = rest - p.astype(F32)
    return parts


def _dot_split_lhs(x, m, n):
    return sum(_dot(p, m) for p in _split(x, n))


def _dot_split_rhs(m, x, n):
    return sum(_dot(m, p) for p in _split(x, n))


def _dot_f32(a, b):
    ah, al = _split(a, 2)
    bh, bl = _split(b, 2)
    return _dot(ah, bh) + _dot(ah, bl) + _dot(al, bh)


def _log_sigmoid(x):
    return jnp.minimum(x, 0.0) - jnp.log1p(jnp.exp(-jnp.abs(x)))


def _norm_mod(x, gain, shift, scale):
    y = x * lax.rsqrt(jnp.mean(x * x, axis=-1, keepdims=True) + NORM_EPS)
    return (y * gain) * (1.0 + scale) + shift


def _group_layernorm(y, avg, gain):
    yc = y - _dot(y.astype(BF16), avg)
    var = _dot((yc * yc).astype(BF16), avg)
    return yc * lax.rsqrt(var + GN_EPS) * gain


def _mod_kernel(cv_ref, w_ref, b_ref, o_ref):
    cv = cv_ref[...]
    o_ref[0] = _dot_f32(cv * jax.nn.sigmoid(cv), w_ref[0]) + b_ref[0]


def _modulation(cvec, mod_w, mod_b):
    depth, d, n = mod_w.shape
    tn = n // 4
    return pl.pallas_call(
        _mod_kernel,
        grid=(depth, n // tn),
        in_specs=[pl.BlockSpec((SUBLANES, d), lambda l, j: (0, 0)),
                  pl.BlockSpec((1, d, tn), lambda l, j: (l, 0, j)),
                  pl.BlockSpec((1, 1, tn), lambda l, j: (l, 0, j))],
        out_specs=pl.BlockSpec((1, SUBLANES, tn), lambda l, j: (l, 0, j)),
        out_shape=jax.ShapeDtypeStruct((depth, SUBLANES, n), F32),
        compiler_params=_params("parallel", "parallel"),
        name="adaln_modulation",
    )(cvec, mod_w, mod_b.reshape(depth, 1, n))


REC_Q, REC_V, REC_O, REC_RKV, REC_LORA, REC_GATE, REC_END = 0, 512, 1024, 1536, 3072, 3456, 3584
N_GATES = 4 * ML_HEADS


def _token_rows(x_ref, ctx_ref, n_lat_tiles):
    return jnp.where(pl.program_id(1) >= n_lat_tiles, ctx_ref[0], x_ref[0])


def _token_specs(tm, d, n_lat_tiles):
    return [pl.BlockSpec((1, tm, d), lambda b, i: (b, jnp.minimum(i, n_lat_tiles - 1), 0)),
            pl.BlockSpec((1, tm, d), lambda b, i: (b, jnp.maximum(i - n_lat_tiles, 0), 0))]


def _rec_inproj_kernel(x_ref, ctx_ref, mod_ref, g_ref, w_ref, wkt_ref, wgt_ref, gb_ref, gbt_ref,
                       q_ref, v_ref, o_ref, rkv_ref, lora_ref, gate_ref, kt_ref, gatet_ref, *, n_lat_tiles):
    d = x_ref.shape[-1]
    mod = mod_ref[0]
    h = _norm_mod(_token_rows(x_ref, ctx_ref, n_lat_tiles), g_ref[...], mod[:, 0:d], mod[:, d:2 * d]).astype(BF16)
    q_ref[0] = _dot(h, w_ref[:, REC_Q:REC_V]).astype(BF16)
    v_ref[0] = _dot(h, w_ref[:, REC_V:REC_O]).astype(BF16)
    o_ref[0] = _dot(h, w_ref[:, REC_O:REC_RKV]).astype(BF16)
    rkv_ref[0] = _dot(h, w_ref[:, REC_RKV:REC_LORA]).astype(BF16)
    lora_ref[0] = _dot(h, w_ref[:, REC_LORA:REC_GATE])
    gate_ref[0] = _dot(h, w_ref[:, REC_GATE:REC_END]) + gb_ref[...]
    kt_ref[0] = _dot_nt(wkt_ref[...], h).astype(BF16)
    gatet_ref[0] = _dot_nt(wgt_ref[...], h) + gbt_ref[...]


def _rec_inproj(x, ctx, mods, gain, w_main, wkt, wgt, gb_row, gb_col, n_lat_tiles):
    bsz, _, d = x.shape
    s = x.shape[1] + ctx.shape[1]
    tm = ROW_TILE
    tok = lambda c: pl.BlockSpec((1, tm, c), lambda b, i: (b, i, 0))
    mod_row = lambda b, i: (jnp.where(i >= n_lat_tiles, bsz, b), 0, 0)
    return pl.pallas_call(
        functools.partial(_rec_inproj_kernel, n_lat_tiles=n_lat_tiles),
        grid=(bsz, s // tm),
        in_specs=_token_specs(tm, d, n_lat_tiles) + [
                  pl.BlockSpec((1, 1, mods.shape[-1]), mod_row),
                  _resident((1, d)), _resident(w_main.shape), _resident(wkt.shape), _resident(wgt.shape),
                  _resident(gb_row.shape), _resident(gb_col.shape)],
        out_specs=[tok(ML_WIDTH), tok(ML_WIDTH), tok(ML_WIDTH), tok(3 * RW_WIDTH), tok(REC_GATE - REC_LORA),
                   tok(LANES),
                   pl.BlockSpec((1, ML_WIDTH, tm), lambda b, i: (b, 0, i)),
                   pl.BlockSpec((1, N_GATES, tm), lambda b, i: (b, 0, i))],
        out_shape=[jax.ShapeDtypeStruct((bsz, s, ML_WIDTH), BF16),
                   jax.ShapeDtypeStruct((bsz, s, ML_WIDTH), BF16),
                   jax.ShapeDtypeStruct((bsz, s, ML_WIDTH), BF16),
                   jax.ShapeDtypeStruct((bsz, s, 3 * RW_WIDTH), BF16),
                   jax.ShapeDtypeStruct((bsz, s, REC_GATE - REC_LORA), F32),
                   jax.ShapeDtypeStruct((bsz, s, LANES), F32),
                   jax.ShapeDtypeStruct((bsz, ML_WIDTH, s), BF16),
                   jax.ShapeDtypeStruct((bsz, N_GATES, s), F32)],
        compiler_params=_params("parallel", "parallel"),
        name="rec_inproj",
    )(x, ctx, mods, gain, w_main, wkt, wgt, gb_row, gb_col)


def _mlstm_stages(qf_ref, vf_ref, ktf_ref, gf_ref, gtf_ref, qb_ref, vb_ref, ktb_ref, gb_ref, gtb_ref,
                  hf_ref, hb_ref, c_ref, m_ref):
    L, dh = ML_CHUNK, ML_HEAD_DIM
    t_idx = lax.broadcasted_iota(jnp.int32, (L, L), 0)
    s_idx = lax.broadcasted_iota(jnp.int32, (L, L), 1)
    ones_col = (lax.broadcasted_iota(jnp.int32, (L, LANES), 1) == 0).astype(BF16)
    scale = dh ** -0.5
    streams = ((qf_ref, vf_ref, ktf_ref, gf_ref, gtf_ref, hf_ref),
               (qb_ref, vb_ref, ktb_ref, gb_ref, gtb_ref, hb_ref))
    chains = []
    for d, (q_ref, v_ref, kt_ref, g_ref, gt_ref, h_ref) in enumerate(streams):
        earlier = (s_idx <= t_idx) if d == 0 else (s_idx >= t_idx)
        tri = earlier.astype(BF16)
        tri_t = ((t_idx <= s_idx) if d == 0 else (t_idx >= s_idx)).astype(BF16)
        gt = gt_ref[0]
        lf_rows = _log_sigmoid(gt)
        b_cols = _dot_split_rhs(tri, _log_sigmoid(g_ref[0]), 3)
        b_rows = _dot_split_lhs(lf_rows, tri_t, 3)
        tot = jnp.sum(lf_rows, axis=1, keepdims=True)
        q, v, kt = q_ref[0], v_ref[0], kt_ref[0]
        for hh in range(ML_HEADS):
            ic, fc = 2 * ML_HEADS * d + hh, 2 * ML_HEADS * d + ML_HEADS + hh
            chains.append(dict(j=d * ML_HEADS + hh, hh=hh, h_ref=h_ref, earlier=earlier,
                               b_col=b_cols[:, fc:fc + 1], b_row=b_rows[fc:fc + 1, :], i_row=gt[ic:ic + 1, :],
                               b_end=tot[fc:fc + 1, :], q=q[:, hh * dh:(hh + 1) * dh], kt=kt[hh * dh:(hh + 1) * dh, :],
                               v_aug=jnp.concatenate([v[:, hh * dh:(hh + 1) * dh], ones_col], axis=1)))
    yield
    for ch in chains:
        ch["cst"] = c_ref[ch["j"]]
        ch["qk"] = _dot(ch["q"], ch["kt"])
        ch["qc"] = _dot(ch["q"], ch["cst"].astype(BF16))
    yield
    for ch in chains:
        m_prev = m_ref[ch["j"]:ch["j"] + 1, 0:1]
        d_in = jnp.where(ch["earlier"], ch["b_col"] - ch["b_row"] + ch["i_row"], MASK_VALUE)
        d_prev = ch["b_col"] + m_prev
        m_t = jnp.maximum(d_prev, jnp.max(d_in, axis=1, keepdims=True))
        ch["s"] = (ch["qk"] * (scale * jnp.exp(d_in - m_t))).astype(BF16)
        ch["w_prev"], ch["floor"] = jnp.exp(d_prev - m_t), jnp.exp(-m_t)
        d_end = ch["b_end"] - ch["b_row"] + ch["i_row"]
        m_new = jnp.maximum(ch["b_end"] + m_prev, jnp.max(d_end, axis=1, keepdims=True))
        ch["kw"] = (ch["kt"].astype(F32) * (jnp.exp(d_end - m_new) * scale)).astype(BF16)
        ch["decay"] = jnp.exp(ch["b_end"] + m_prev - m_new)
        m_ref[ch["j"]:ch["j"] + 1, :] = jnp.broadcast_to(m_new, (1, LANES))
    yield
    for ch in chains:
        acc = _dot(ch["s"], ch["v_aug"]) + ch["w_prev"] * ch["qc"]
        den = jnp.maximum(jnp.abs(acc[:, dh:dh + 1]), ch["floor"])
        ch["h_ref"][0, :, ch["hh"] * dh:(ch["hh"] + 1) * dh] = acc[:, :dh] / den
    yield
    for ch in chains:
        c_ref[ch["j"]] = ch["decay"] * ch["cst"] + _dot(ch["kw"], ch["v_aug"])


def _chunk_order(n_lat, n_ctx):
    n = n_lat + n_ctx
    fwd = lambda c: (c + n_lat) % n
    bwd = lambda c: n - 1 - c
    return fwd, bwd


def _rw_prep_kernel(z_ref, zp_ref, zn_ref, lora_ref, conv_ref, vec_ref, w2_ref, a2_ref, g2_ref, ones_ref,
                    r_ref, v_ref, kk_ref, bonus_ref, gate_ref, lw_ref, key_ref, bv_ref, *, t_lat, s_tot):
    tm, w = z_ref.shape[1], RW_WIDTH
    row0 = pl.program_id(1) * tm
    has_prev = jnp.where(jnp.logical_and(row0 != 0, row0 != t_lat), 1.0, 0.0)
    has_next = jnp.where(jnp.logical_and(row0 + tm != t_lat, row0 + tm != s_tot), 1.0, 0.0)
    z = z_ref[0].astype(F32)
    prev_row = zp_ref[0][HALO_ROWS - 1:HALO_ROWS, :].astype(F32) * has_prev
    next_row = zn_ref[0][0:1, :].astype(F32) * has_next
    ridx = lax.broadcasted_iota(jnp.int32, z.shape, 0)
    z_prev = jnp.where(ridx == 0, prev_row, pltpu.roll(z, 1, 0))
    z_next = jnp.where(ridx == tm - 1, next_row, pltpu.roll(z, tm - 1, 0))
    cw = conv_ref[...]
    zc = cw[0:1] * z_prev + cw[1:2] * z + cw[2:3] * z_next
    r, kr, vr = zc[:, :w], zc[:, w:2 * w], zc[:, 2 * w:]
    vec = vec_ref[...]
    k_k, k_a, r_k = vec[0:1], vec[1:2], vec[2:3]
    ones = ones_ref[...]
    kkr = kr * k_k
    kk = kkr * lax.rsqrt(_dot((kkr * kkr).astype(BF16), ones) + NORM_EPS)
    r_ref[0] = r.astype(BF16)
    v_ref[0] = vr.astype(BF16)
    kk_ref[0] = kk.astype(BF16)
    bonus_ref[0] = (_dot((r * kr * r_k).astype(BF16), ones) * vr).astype(BF16)
    lora = lora_ref[0]
    dec = _dot(jnp.tanh(lora[:, 0:LANES]).astype(BF16), w2_ref[...])
    icl = _dot(lora[:, LANES:2 * LANES].astype(BF16), a2_ref[...])
    gate_ref[0] = _dot(jax.nn.sigmoid(lora[:, 2 * LANES:3 * LANES]).astype(BF16), g2_ref[...]).astype(BF16)
    for d in range(2):
        a = jax.nn.sigmoid(vec[5 + d:6 + d] + icl[:, d * w:(d + 1) * w])
        lw_ref[d, 0] = -RW_DECAY_SCALE * jax.nn.sigmoid(vec[3 + d:4 + d] + dec[:, d * w:(d + 1) * w])
        key_ref[d, 0] = (kr * (1.0 + (a - 1.0) * k_a)).astype(BF16)
        bv_ref[d, 0] = (kk * a).astype(BF16)


def _rw_prep(rkv, lora, conv_w, vecs, w2cat, a2cat, g2, ones64, t_lat):
    bsz, s, _ = rkv.shape
    tm, w = ROW_TILE, RW_WIDTH
    nsub = tm // HALO_ROWS
    tok = lambda c: pl.BlockSpec((1, tm, c), lambda b, i: (b, i, 0))
    dirtok = pl.BlockSpec((2, 1, tm, w), lambda b, i: (0, b, i, 0))
    last = s // HALO_ROWS - 1
    return pl.pallas_call(
        functools.partial(_rw_prep_kernel, t_lat=t_lat, s_tot=s),
        grid=(bsz, s // tm),
        in_specs=[tok(3 * w),
                  pl.BlockSpec((1, HALO_ROWS, 3 * w), lambda b, i: (b, jnp.maximum(i * nsub - 1, 0), 0)),
                  pl.BlockSpec((1, HALO_ROWS, 3 * w), lambda b, i: (b, jnp.minimum((i + 1) * nsub, last), 0)),
                  tok(lora.shape[-1]),
                  _resident(conv_w.shape), _resident(vecs.shape), _resident(w2cat.shape), _resident(a2cat.shape),
                  _resident(g2.shape), _resident(ones64.shape)],
        out_specs=[tok(w)] * 5 + [dirtok] * 3,
        out_shape=[jax.ShapeDtypeStruct((bsz, s, w), BF16)] * 5
        + [jax.ShapeDtypeStruct((2, bsz, s, w), dt) for dt in (F32, BF16, BF16)],
        compiler_params=_params("parallel", "parallel"),
        name="rwkv_prep",
    )(rkv, rkv, rkv, lora, conv_w, vecs, w2cat, a2cat, g2, ones64)


def _rw_scan_stages(rf_ref, vf_ref, kkf_ref, lwf_ref, keyf_ref, bvf_ref, rb_ref, vb_ref, kkb_ref, lwb_ref, keyb_ref,
                    bvb_ref, yf_ref, yb_ref, s_ref, step):
    C, n = RW_CHUNK, RW_HEAD_DIM
    t2 = lax.broadcasted_iota(jnp.int32, (2 * C, 2 * C), 0) % C
    s2 = lax.broadcasted_iota(jnp.int32, (2 * C, 2 * C), 1) % C
    lower = lax.broadcasted_iota(jnp.int32, (2 * C, 2 * C), 0) >= C

    chains = []
    rows_of = (slice(step * C, (step + 1) * C), slice((1 - step) * C, (2 - step) * C))
    for d, (r_ref, v_ref, kk_ref, lw_ref, key_ref, bv_ref) in enumerate(
            ((rf_ref, vf_ref, kkf_ref, lwf_ref, keyf_ref, bvf_ref), (rb_ref, vb_ref, kkb_ref, lwb_ref, keyb_ref, bvb_ref))):
        rows = rows_of[d]
        rel = (s2 - t2) if d == 0 else (t2 - s2)
        keep = rel < jnp.where(lower, 1, 0)
        tri = (rel[:C, :C] <= 0).astype(BF16)
        lw = lw_ref[0, 0, rows, :]
        cum = _dot_split_rhs(tri, lw, 3)
        tot = jnp.sum(lw, axis=0, keepdims=True)
        r, v, kk, key, bv = (a.astype(F32) for a in (r_ref[0, rows, :], v_ref[0, rows, :], kk_ref[0, rows, :],
                                                     key_ref[0, 0, rows, :], bv_ref[0, 0, rows, :]))
        e_neg = jnp.exp(-cum)
        e_end = jnp.exp(tot - cum)
        alpha = (kk * jnp.exp(cum - lw)).astype(BF16)
        rho = (r * jnp.exp(cum)).astype(BF16)
        beta = (bv * e_neg).astype(BF16)
        kappa = (key * e_neg).astype(BF16)
        beta_end = (bv * e_end).astype(BF16)
        kappa_end = (key * e_end).astype(BF16)
        gamma = jnp.exp(tot)
        vb = v.astype(BF16)
        for h in range(RW_HEADS):
            sl = slice(h * n, (h + 1) * n)
            chains.append(dict(j=d * RW_HEADS + h, keep=keep, a=alpha[:, sl], rho=rho[:, sl], v=vb[:, sl],
                               lhs=jnp.concatenate([alpha[:, sl], rho[:, sl]], axis=0),
                               rhs=jnp.concatenate([beta[:, sl], kappa[:, sl]], axis=0),
                               end=jnp.concatenate([beta_end[:, sl], kappa_end[:, sl]], axis=0),
                               gamma=gamma[:, sl]))
    yield
    for ch in chains:
        ch["big"] = jnp.where(ch["keep"], _dot_nt(ch["lhs"], ch["rhs"]), 0.0).astype(BF16)
    yield
    for ch in chains:
        ch["lmv"] = _dot(ch["big"][:, C:], ch["v"])
    yield
    tt = lax.broadcasted_iota(jnp.int32, (C, C), 0)
    ss = lax.broadcasted_iota(jnp.int32, (C, C), 1)
    eye = (tt == ss).astype(F32)
    for ch in chains:
        ch["lab"] = ch["big"][:C, :C]
        ch["inv"] = eye - jnp.where((tt >> 1) == (ss >> 1), ch["lab"], 0.0).astype(F32)
    for k in range(1, 6):
        couples = jnp.logical_and((tt >> (k + 1)) == (ss >> (k + 1)), (tt >> k) != (ss >> k))
        for ch in chains:
            ch["ed"] = _dot(jnp.where(couples, ch["lab"], 0.0).astype(BF16), ch["inv"].astype(BF16)).astype(BF16)
        yield
        for ch in chains:
            ch["inv"] = ch["inv"] - _dot(ch["inv"].astype(BF16), ch["ed"])
        yield
    for ch in chains:
        x = -jnp.concatenate([ch["a"].astype(F32), ch["lmv"][:C]], axis=1)
        ch["x"] = _dot(ch["inv"].astype(BF16), x.astype(BF16))
    yield
    for ch in chains:
        ch["st"] = s_ref[ch["j"]]
        ch["ws"] = _dot_nt(jnp.concatenate([ch["x"][:, :n].astype(BF16), ch["rho"]], axis=0), ch["st"].astype(BF16))
    yield
    for ch in chains:
        ch["u"] = (ch["ws"][:C] + ch["x"][:, n:]).astype(BF16)
    for ch in chains:
        ch["y"] = ch["ws"][C:] + _dot(ch["big"][C:, :C], ch["u"]) + ch["lmv"][C:]
    yield
    for ch in chains:
        upd = _dot_tn(jnp.concatenate([ch["u"], ch["v"]], axis=0), ch["end"])
        s_ref[ch["j"]] = ch["st"] * ch["gamma"] + upd
    yf_ref[0, rows_of[0], :] = jnp.concatenate([ch["y"] for ch in chains[:RW_HEADS]], axis=1)
    yb_ref[0, rows_of[1], :] = jnp.concatenate([ch["y"] for ch in chains[RW_HEADS:]], axis=1)


RW_STAGES_PER_ML_STAGE = 3
RW_STATE_FREE_STAGES = 14


def _rec_scan_kernel(*refs):
    rw_in, ml_in = refs[:12], refs[12:22]
    yf_ref, yb_ref, hf_ref, hb_ref, s_ref, c_ref, m_ref = refs[22:]

    @pl.when(pl.program_id(1) == 0)
    def _():
        s_ref[...] = jnp.zeros_like(s_ref)
        c_ref[...] = jnp.zeros_like(c_ref)
        m_ref[...] = jnp.zeros_like(m_ref)

    ml = _mlstm_stages(*ml_in, hf_ref, hb_ref, c_ref, m_ref)
    first, second = (_rw_scan_stages(*rw_in, yf_ref, yb_ref, s_ref, step) for step in range(2))
    for n_stage in range(RW_STATE_FREE_STAGES):
        next(first)
        next(second)
        if (n_stage + 1) % RW_STAGES_PER_ML_STAGE == 0:
            next(ml, None)
    for gen in (first, ml, second):
        for _ in gen:
            pass


def _rec_scan(r, v, kk, lw, key, bv, q, vm, kt, gates, gates_t, t_lat):
    bsz, s, w = r.shape
    L = ML_CHUNK
    orders = _chunk_order(t_lat // L, (s - t_lat) // L)
    rw_specs, ml_specs, out_rw, out_ml = [], [], [], []
    for d, order in enumerate(orders):
        tok = lambda c, order=order: pl.BlockSpec((1, L, c), lambda b, i: (b, order(i), 0))
        dirtok = pl.BlockSpec((1, 1, L, w), lambda b, i, order=order, d=d: (d, b, order(i), 0))
        timelast = lambda c, order=order: pl.BlockSpec((1, c, L), lambda b, i: (b, 0, order(i)))
        rw_specs += [tok(w), tok(w), tok(w), dirtok, dirtok, dirtok]
        ml_specs += [tok(ML_WIDTH), tok(ML_WIDTH), timelast(ML_WIDTH), tok(LANES), timelast(N_GATES)]
        out_rw.append(tok(w))
        out_ml.append(tok(ML_WIDTH))
    return pl.pallas_call(
        _rec_scan_kernel,
        grid=(bsz, s // L),
        in_specs=rw_specs + ml_specs,
        out_specs=out_rw + out_ml,
        out_shape=[jax.ShapeDtypeStruct((bsz, s, w), F32)] * 2 + [jax.ShapeDtypeStruct((bsz, s, ML_WIDTH), F32)] * 2,
        scratch_shapes=[pltpu.VMEM((2 * RW_HEADS, RW_HEAD_DIM, RW_HEAD_DIM), F32),
                        pltpu.VMEM((2 * ML_HEADS, ML_HEAD_DIM, ML_HEAD_DIM + LANES), F32),
                        pltpu.VMEM((2 * ML_HEADS, LANES), F32)],
        compiler_params=_params("parallel", "arbitrary"),
        name="recurrent_scan",
    )(r, v, kk, lw, key, bv, r, v, kk, lw, key, bv, q, vm, kt, gates, gates_t, q, vm, kt, gates, gates_t)


def _residual_mlp(mix, x, mod, g_ffn, wo_ref, w1_ref, w2_ref, out_ref):
    d = x.shape[-1]
    x1 = x + mod[:, 2 * d:3 * d] * _dot(mix, wo_ref[...])
    h2 = _norm_mod(x1, g_ffn, mod[:, 3 * d:4 * d], mod[:, 4 * d:5 * d]).astype(BF16)
    acc = jnp.zeros_like(x1)
    for k in range(w1_ref.shape[1] // d):
        hid = jnp.maximum(_dot(h2, w1_ref[:, k * d:(k + 1) * d]), 0.0)
        acc = acc + _dot((hid * hid).astype(BF16), w2_ref[k * d:(k + 1) * d, :])
    out_ref[0] = x1 + mod[:, 5 * d:6 * d] * acc


def _rec_post_kernel(hf_ref, hb_ref, o_ref, yf_ref, yb_ref, bonus_ref, gate_ref, x_ref, ctx_ref, mod_ref, gf_ref,
                     mlg_ref, rwg_ref, avg_ml_ref, avg_rw_ref, wo_ref, w1_ref, w2_ref, out_ref, *, n_lat_tiles):
    ml = _group_layernorm(hf_ref[0] + hb_ref[0], avg_ml_ref[...], mlg_ref[...]) * jax.nn.sigmoid(o_ref[0].astype(F32))
    rw = (_group_layernorm(yf_ref[0] + yb_ref[0], avg_rw_ref[...], rwg_ref[...]) + bonus_ref[0]) * gate_ref[0]
    mix = jnp.concatenate([ml, rw], axis=1).astype(BF16)
    x = _token_rows(x_ref, ctx_ref, n_lat_tiles)
    _residual_mlp(mix, x, mod_ref[0], gf_ref[...], wo_ref, w1_ref, w2_ref, out_ref)


def _att_post_kernel(ow_ref, og_ref, x_ref, mod_ref, gf_ref, wo_ref, w1_ref, w2_ref, out_ref):
    mix = jnp.concatenate([ow_ref[0], og_ref[0]], axis=1)
    _residual_mlp(mix, x_ref[0], mod_ref[0], gf_ref[...], wo_ref, w1_ref, w2_ref, out_ref)


def _post_call(kernel, name, token_inputs, residual, mods, consts, n_rows, n_lat_tiles, tm=ROW_TILE):
    bsz, _, d = residual[0].shape
    mod_row = lambda b, i: (jnp.where(i >= n_lat_tiles, bsz, b), 0, 0)
    tok = lambda c: pl.BlockSpec((1, tm, c), lambda b, i: (b, i, 0))
    in_specs = [tok(arr.shape[-1]) for arr in token_inputs]
    in_specs += _token_specs(tm, d, n_lat_tiles) if len(residual) == 2 else [tok(d)]
    in_specs += [pl.BlockSpec((1, 1, mods.shape[-1]), mod_row)]
    in_specs += [_resident(c.shape) for c in consts]
    args = list(token_inputs) + list(residual)
    return pl.pallas_call(
        kernel,
        grid=(bsz, n_rows // tm),
        in_specs=in_specs,
        out_specs=pl.BlockSpec((1, tm, d), lambda b, i: (b, i, 0)),
        out_shape=jax.ShapeDtypeStruct((bsz, n_rows, d), F32),
        compiler_params=_params("parallel", "parallel"),
        name=name,
    )(*args, mods, *consts)


ATT_QW, ATT_QG, ATT_KW, ATT_KG, ATT_VW, ATT_VG, ATT_END = 0, 512, 1024, 1152, 1280, 1408, 1536
ATT_QK_END = ATT_VW


def _att_inproj_kernel(x_ref, mod_ref, g_ref, w_ref, qkg_ref, cos_ref, sin_ref, sq_ref,
                       qw_ref, qg_ref, kw_ref, kg_ref, vw_ref, vg0_ref, vg1_ref):
    d = x_ref.shape[-1]
    mod = mod_ref[0]
    h = _norm_mod(x_ref[0], g_ref[...], mod[:, 0:d], mod[:, d:2 * d]).astype(BF16)
    half = ATT_HEAD_DIM // 2
    cos2, sin2 = cos_ref[0], sin_ref[0]

    def norm_rope(lo, hi):
        z = _dot(h, w_ref[:, lo:hi])
        n = hi - lo
        sq = sq_ref[...]
        z = z * lax.rsqrt(_dot((z * z).astype(BF16), sq[:n, :n]) + NORM_EPS) * qkg_ref[:, lo:hi]
        reps = n // LANES
        cos = jnp.concatenate([cos2] * reps, axis=1)
        sin = jnp.concatenate([sin2] * reps, axis=1)
        lane = lax.broadcasted_iota(jnp.int32, z.shape, 1)
        partner = jnp.where(lane % ATT_HEAD_DIM < half, pltpu.roll(z, n - half, 1), pltpu.roll(z, half, 1))
        return (z * cos + partner * sin).astype(BF16)

    qw_ref[0] = norm_rope(ATT_QW, ATT_QG)
    qg_ref[0] = norm_rope(ATT_QG, ATT_KW)
    kk = norm_rope(ATT_KW, ATT_VW)
    kw_ref[0] = kk[:, :KV_WIDTH]
    kg_ref[0] = kk[:, KV_WIDTH:]
    vv = _dot(h, w_ref[:, ATT_VW:ATT_END])
    vw_ref[0] = vv[:, :KV_WIDTH].astype(BF16)
    vg = vv[:, KV_WIDTH:]
    lane = lax.broadcasted_iota(jnp.int32, vg.shape, 1)
    vg0_ref[0] = jnp.where(lane < ATT_HEAD_DIM, vg, jnp.where(lane == ATT_HEAD_DIM, 1.0, 0.0)).astype(BF16)
    vg1_ref[0] = jnp.where(lane >= ATT_HEAD_DIM, vg, jnp.where(lane == 0, 1.0, 0.0)).astype(BF16)


def _att_inproj(xs, mods, gain, w, qk_gain, cos, sin, sq_avg, n_lat_tiles):
    bsz, s, d = xs.shape
    tm = ROW_TILE
    tok = lambda c: pl.BlockSpec((1, tm, c), lambda b, i: (b, i, 0))
    mod_row = lambda b, i: (jnp.where(i >= n_lat_tiles, bsz, b), 0, 0)
    rope = pl.BlockSpec((1, tm, LANES), lambda b, i: (0, i, 0))
    qw = Q_HEADS * ATT_HEAD_DIM
    return pl.pallas_call(
        _att_inproj_kernel,
        grid=(bsz, s // tm),
        in_specs=[tok(d), pl.BlockSpec((1, 1, mods.shape[-1]), mod_row), _resident((1, d)), _resident(w.shape),
                  _resident(qk_gain.shape), rope, rope, _resident(sq_avg.shape)],
        out_specs=[tok(qw), tok(qw)] + [tok(KV_WIDTH)] * 5,
        out_shape=[jax.ShapeDtypeStruct((bsz, s, c), BF16) for c in (qw, qw) + (KV_WIDTH,) * 5],
        compiler_params=_params("parallel", "parallel"),
        name="att_inproj",
    )(xs, mods, gain, w, qk_gain, cos, sin, sq_avg)


def _stack_queries(q):
    lane = lax.broadcasted_iota(jnp.int32, (q.shape[0], LANES), 1)
    zero = jnp.zeros((q.shape[0], LANES), q.dtype)
    slabs = [q[:, g * LANES:(g + 1) * LANES] for g in range(KV_GROUP)]
    lo = [jnp.where(lane < ATT_HEAD_DIM, sl, zero) for sl in slabs]
    hi = [jnp.where(lane >= ATT_HEAD_DIM, sl, zero) for sl in slabs]
    return jnp.concatenate(lo + hi, axis=0)


def _unstack_outputs(o, rows):
    lane = lax.broadcasted_iota(jnp.int32, (rows, LANES), 1)
    slabs = [jnp.where(lane < ATT_HEAD_DIM, o[g * rows:(g + 1) * rows], o[(KV_GROUP + g) * rows:(KV_GROUP + g + 1) * rows])
             for g in range(KV_GROUP)]
    return jnp.concatenate(slabs, axis=1)


def _attn_kernel(qw_ref, kp_ref, ko_ref, kn_ref, kc_ref, vp_ref, vo_ref, vn_ref, vc_ref, sink_ref,
                 qg_ref, k_ref, v0_ref, v1_ref, ow_ref, og_ref, *, tk):
    L = Q_BLOCK
    i, nb = pl.program_id(1), pl.num_programs(1)
    lc = kc_ref.shape[1]
    qs_w = _stack_queries(qw_ref[0])
    keys = jnp.concatenate([kp_ref[0], ko_ref[0], kn_ref[0], kc_ref[0]], axis=0)
    vals = jnp.concatenate([vp_ref[0], vo_ref[0], vn_ref[0], vc_ref[0]], axis=0)
    s_w = _dot_nt(qs_w, keys)

    qs = _stack_queries(qg_ref[0])
    rows = qs.shape[0]
    half = rows // 2
    n_tiles = k_ref.shape[1] // tk
    scores = lambda j: _dot_nt(qs, k_ref[0, j * tk:(j + 1) * tk, :])
    s_next = scores(0)

    nk = 3 * L + lc
    row = lax.broadcasted_iota(jnp.int32, (L, nk), 0)
    col = lax.broadcasted_iota(jnp.int32, (L, nk), 1)
    off = col - L - row
    valid = jnp.logical_and(off <= WINDOW, off >= -WINDOW)
    valid = jnp.logical_and(valid, col >= jnp.where(i > 0, 0, L))
    valid = jnp.logical_and(valid, col < jnp.where(i < nb - 1, 3 * L, 2 * L))
    valid = jnp.logical_or(valid, col >= 3 * L)
    valid = jnp.concatenate([valid] * (2 * KV_GROUP), axis=0)
    s_w = jnp.where(valid, s_w, MASK_VALUE)
    sink = sink_ref[...]
    m_w = jnp.maximum(jnp.max(s_w, axis=1, keepdims=True), sink)
    p_w = jnp.exp(s_w - m_w)
    den_w = jnp.sum(p_w, axis=1, keepdims=True) + jnp.exp(sink - m_w)
    ow_ref[0] = _unstack_outputs(_dot(p_w.astype(BF16), vals) / den_w, L).astype(BF16)

    m = jnp.full((rows, 1), MASK_VALUE, F32)
    acc = jnp.zeros((rows, LANES), F32)
    for j in range(n_tiles):
        s = s_next
        if j + 1 < n_tiles:
            s_next = scores(j + 1)
        m_new = jnp.maximum(m, jnp.max(s, axis=1, keepdims=True))
        p = jnp.exp2(s - m_new).astype(BF16)
        pv = jnp.concatenate([_dot(p[:half], v0_ref[0, j * tk:(j + 1) * tk, :]),
                              _dot(p[half:], v1_ref[0, j * tk:(j + 1) * tk, :])], axis=0)
        acc = jnp.exp2(m - m_new) * acc + pv
        m = m_new
    den = jnp.concatenate([acc[:half, ATT_HEAD_DIM:ATT_HEAD_DIM + 1], acc[half:, 0:1]], axis=0)
    og_ref[0] = _unstack_outputs(acc / den, L).astype(BF16)


GLB_KEY_TILES = (2816, 1408, 768, 512, 384, 256, 128)


def _attention(qw, kw, vw, sink_col, qg, kg, vg0, vg1, t_lat):
    bsz, s, width = qw.shape
    L = Q_BLOCK
    nb = t_lat // L
    lc = s - t_lat
    tk = next(t for t in GLB_KEY_TILES if s % t == 0)
    qblk = pl.BlockSpec((1, L, width), lambda b, i: (b, i, 0))
    kv = lambda f: pl.BlockSpec((1, L, KV_WIDTH), lambda b, i: (b, f(i), 0))
    ctx = pl.BlockSpec((1, lc, KV_WIDTH), lambda b, i: (b, t_lat // lc, 0))
    nbr = [kv(lambda i: jnp.maximum(i - 1, 0)), kv(lambda i: i), kv(lambda i: jnp.minimum(i + 1, nb - 1)), ctx]
    whole = pl.BlockSpec((1, s, KV_WIDTH), lambda b, i: (b, 0, 0))
    return pl.pallas_call(
        functools.partial(_attn_kernel, tk=tk),
        grid=(bsz, nb),
        in_specs=[qblk] + nbr + nbr + [_resident(sink_col.shape), qblk, whole, whole, whole],
        out_specs=[qblk, qblk],
        out_shape=[jax.ShapeDtypeStruct((bsz, t_lat, width), BF16)] * 2,
        compiler_params=_params("parallel", "parallel"),
        name="attention",
    )(qw, kw, kw, kw, kw, vw, vw, vw, vw, sink_col, qg, kg, vg0, vg1)


def _block_diag_const(width, group, value):
    idx = np.arange(width) // group
    return jnp.asarray((idx[:, None] == idx[None, :]) * value, dtype=BF16)


def _head_perm():
    return np.concatenate([np.arange(0, ATT_HEAD_DIM, 2), np.arange(1, ATT_HEAD_DIM, 2)])


def _q_head_order():
    return [kv * KV_GROUP + g for g in range(KV_GROUP) for kv in range(KV_HEADS)]


def _rec_weights(w, gate_b, conv_w, w0, w2, a0, a2, k_k, k_a, r_k):
    o = np.cumsum((0,) + (ML_WIDTH,) * 4 + (ML_HEADS,) * 4 + (RW_WIDTH,) * 3
                  + (RW_DECAY_RANK,) * 2 + (RW_ICLR_RANK,) * 2 + (RW_GATE_RANK,))
    q, k, v, og = (w[:, o[i]:o[i + 1]] for i in range(4))
    gates = w[:, o[4]:o[8]]
    rkv = w[:, o[8]:o[11]]
    lora = w[:, o[11]:o[16]]
    gates_pad = jnp.pad(gates, ((0, 0), (0, LANES - N_GATES)))
    w_main = jnp.concatenate([q, v, og, rkv, lora, gates_pad], axis=1).astype(BF16)
    gb = gate_b.reshape(-1)
    gb_row = jnp.pad(gb, (0, LANES - N_GATES)).reshape(1, LANES)
    zeros = jnp.zeros((RW_DECAY_RANK, RW_WIDTH), F32)
    w2cat = jnp.concatenate([jnp.concatenate([w2[0], zeros], 1), jnp.concatenate([zeros, w2[1]], 1)], 0).astype(BF16)
    a2cat = jnp.concatenate([jnp.concatenate([a2[0], zeros], 1), jnp.concatenate([zeros, a2[1]], 1)], 0).astype(BF16)
    vecs = jnp.stack([k_k, k_a, r_k, w0[0], w0[1], a0[0], a0[1], jnp.zeros_like(k_k)])
    return dict(w_main=w_main, wkt=k.T.astype(BF16), wgt=gates.T.astype(BF16), gb_row=gb_row,
                gb_col=gb.reshape(N_GATES, 1), conv=conv_w, vecs=vecs, w2cat=w2cat, a2cat=a2cat)


def _att_weights(w, win_q_norm, win_k_norm, glb_q_norm, glb_k_norm, w_out):
    hd, perm = ATT_HEAD_DIM, _head_perm()
    qn, kn = Q_HEADS * hd, KV_HEADS * hd
    o = np.cumsum((0, qn, kn, kn, qn, kn, kn))
    qw, kw, vw, qg, kg, vg = (w[:, o[i]:o[i + 1]] for i in range(6))
    q_cols = np.concatenate([h * hd + perm for h in _q_head_order()])
    k_cols = np.concatenate([h * hd + perm for h in range(KV_HEADS)])
    w_all = jnp.concatenate([qw[:, q_cols], qg[:, q_cols], kw[:, k_cols], kg[:, k_cols], vw, vg], axis=1).astype(BF16)
    scale = hd ** -0.5
    log2e = float(np.log2(np.e))
    gains = jnp.concatenate([jnp.tile(win_q_norm[perm], Q_HEADS) * scale,
                             jnp.tile(glb_q_norm[perm], Q_HEADS) * (scale * log2e),
                             jnp.tile(win_k_norm[perm], KV_HEADS), jnp.tile(glb_k_norm[perm], KV_HEADS)]).reshape(1, -1)
    out_rows = np.concatenate([h * hd + np.arange(hd) for h in _q_head_order()])
    wo = jnp.concatenate([w_out[:qn][out_rows], w_out[qn:][out_rows]], axis=0).astype(BF16)
    return w_all, gains, wo


def _rope_tables(t_lat, s):
    n_freq = ATT_HEAD_DIM // 4
    pos = jnp.arange(t_lat)
    inv_freq = ROPE_THETA ** (-jnp.arange(n_freq, dtype=F32) / n_freq)
    ang = jnp.concatenate([(pos // GRID_W).astype(F32)[:, None] * inv_freq,
                           (pos % GRID_W).astype(F32)[:, None] * inv_freq], axis=-1)
    cos, sin = jnp.cos(ang), jnp.sin(ang)
    cos_h = jnp.concatenate([cos, cos], axis=-1)
    sin_h = jnp.concatenate([-sin, sin], axis=-1)
    pad = ((0, s - t_lat), (0, 0))
    cos_h = jnp.pad(cos_h, pad, constant_values=1.0)
    sin_h = jnp.pad(sin_h, pad)
    return jnp.tile(cos_h, (1, 2))[None], jnp.tile(sin_h, (1, 2))[None]


def kernel(x, c, ctx, c_ctx, norm_mix, norm_ffn, mod_w, mod_b, out_w, ffn_w1, ffn_w2, rec_in_w, ml_gate_b, ml_norm_g,
           rw_conv, rw_w0, rw_w2, rw_a0, rw_a2, rw_g2, rw_kk, rw_ka, rw_rk, rw_norm_g, att_in_w, win_q_norm,
           win_k_norm, win_sink, glb_q_norm, glb_k_norm):
    bsz, t_lat, d = x.shape
    lc = ctx.shape[1]
    s = t_lat + lc
    depth = mod_w.shape[0]
    assert depth == 2 and bsz + 1 <= SUBLANES
    assert t_lat % ROW_TILE == 0 and lc % ROW_TILE == 0 and t_lat % lc == 0 and t_lat % GRID_W == 0
    n_lat_tiles = t_lat // ROW_TILE

    cvec = jnp.zeros((SUBLANES, d), F32).at[:bsz].set(c).at[bsz].set(c_ctx)
    mods = _modulation(cvec, mod_w, mod_b)[:, :bsz + 1].reshape(depth, bsz + 1, 1, 6 * d)
    w1 = ffn_w1.astype(BF16)
    w2 = ffn_w2.astype(BF16)

    rw = _rec_weights(rec_in_w[0], ml_gate_b[0], rw_conv[0], rw_w0[0], rw_w2[0], rw_a0[0], rw_a2[0],
                      rw_kk[0], rw_ka[0], rw_rk[0])
    q, v, o, rkv, lora, gates, kt, gates_t = _rec_inproj(
        x, ctx, mods[0], norm_mix[0].reshape(1, d), rw["w_main"], rw["wkt"], rw["wgt"], rw["gb_row"], rw["gb_col"],
        n_lat_tiles)
    ones64 = _block_diag_const(RW_WIDTH, RW_HEAD_DIM, 1.0)
    r, vr, kk, bonus, gate, lw, key, bv = _rw_prep(rkv, lora, rw["conv"], rw["vecs"], rw["w2cat"], rw["a2cat"],
                                                   rw_g2[0].astype(BF16), ones64, t_lat)
    y_f, y_b, h_f, h_b = _rec_scan(r, vr, kk, lw, key, bv, q, v, kt, gates, gates_t, t_lat)
    consts = [norm_ffn[0].reshape(1, d), ml_norm_g[0].reshape(1, -1), rw_norm_g[0].reshape(1, -1),
              _block_diag_const(ML_WIDTH, ML_HEAD_DIM, 1.0 / ML_HEAD_DIM),
              _block_diag_const(RW_WIDTH, RW_HEAD_DIM, 1.0 / RW_HEAD_DIM),
              out_w[0].astype(BF16), w1[0], w2[0]]
    tokens = [h_f, h_b, o, y_f, y_b, bonus, gate]
    xs = _post_call(functools.partial(_rec_post_kernel, n_lat_tiles=n_lat_tiles), "rec_post", tokens, (x, ctx),
                    mods[0], consts, s, n_lat_tiles)

    w_att, qk_gain, wo = _att_weights(att_in_w[0], win_q_norm[0], win_k_norm[0], glb_q_norm[0], glb_k_norm[0], out_w[1])
    cos, sin = _rope_tables(t_lat, s)
    sq_avg = _block_diag_const(Q_HEADS * ATT_HEAD_DIM, ATT_HEAD_DIM, 1.0 / ATT_HEAD_DIM)
    qw, qg, kw, kg, vw, vg0, vg1 = _att_inproj(xs, mods[1], norm_mix[1].reshape(1, d), w_att, qk_gain, cos, sin,
                                               sq_avg, n_lat_tiles)
    sink_col = jnp.repeat(win_sink[0][np.array([kv * KV_GROUP + g for kv in range(KV_HEADS) for g in range(KV_GROUP)])],
                          Q_BLOCK).reshape(-1, 1)
    o_win, o_glb = _attention(qw, kw, vw, sink_col, qg, kg, vg0, vg1, t_lat)
    consts = [norm_ffn[1].reshape(1, d), wo, w1[1], w2[1]]
    tm = next(t for t in POST_LATENT_TILES if t_lat % t == 0)
    return _post_call(_att_post_kernel, "att_post", [o_win, o_glb], (xs,), mods[1], consts, t_lat, t_lat // tm, tm)
```

```python
import functools

import numpy as np
import jax
import jax.numpy as jnp
from jax import lax
from jax.experimental import pallas as pl
from jax.experimental.pallas import tpu as pltpu

F32 = jnp.float32
BF16 = jnp.bfloat16

GRID_W = 64
NORM_EPS = 1e-6
GN_EPS = 64e-5
ROPE_THETA = 10000.0
ML_HEAD_DIM = 128
ML_HEADS = 4
ML_WIDTH = ML_HEADS * ML_HEAD_DIM
ML_CHUNK = 128
RW_HEAD_DIM = 64
RW_HEADS = 8
RW_WIDTH = RW_HEADS * RW_HEAD_DIM
RW_CHUNK = 64
RW_DECAY_RANK = 64
RW_ICLR_RANK = 64
RW_GATE_RANK = 128
RW_DECAY_SCALE = float(np.exp(-0.5))
ATT_HEAD_DIM = 64
Q_HEADS = 8
KV_HEADS = 2
KV_GROUP = Q_HEADS // KV_HEADS
KV_WIDTH = KV_HEADS * ATT_HEAD_DIM
WINDOW = 128
Q_BLOCK = 128

LANES = 128
SUBLANES = 8
VMEM_LIMIT_BYTES = 56 * 1024 * 1024
ROW_TILE = 256
HALO_ROWS = 16
POST_LATENT_TILES = (512, 256)
MASK_VALUE = -1e30


def _params(*semantics):
    return pltpu.CompilerParams(dimension_semantics=semantics, vmem_limit_bytes=VMEM_LIMIT_BYTES)


def _resident(shape):
    zeros = (0,) * len(shape)
    return pl.BlockSpec(shape, lambda *_: zeros, pipeline_mode=pl.Buffered(1))


def _dot(a, b):
    return jnp.dot(a, b, preferred_element_type=F32)


def _dot_nt(a, b):
    return lax.dot_general(a, b, (((1,), (1,)), ((), ())), preferred_element_type=F32)


def _dot_tn(a, b):
    return lax.dot_general(a, b, (((0,), (0,)), ((), ())), preferred_element_type=F32)


def _split(x, n):
    parts, rest = [], x
    for _ in range(n):
        p = rest.astype(BF16)
        parts.append(p)
        rest = rest - p.astype(F32)
    return parts


def _dot_split_lhs(x, m, n):
    return sum(_dot(p, m) for p in _split(x, n))


def _dot_split_rhs(m, x, n):
    return sum(_dot(m, p) for p in _split(x, n))


def _dot_f32(a, b):
    ah, al = _split(a, 2)
    bh, bl = _split(b, 2)
    return _dot(ah, bh) + _dot(ah, bl) + _dot(al, bh)


def _log_sigmoid(x):
    return jnp.minimum(x, 0.0) - jnp.log1p(jnp.exp(-jnp.abs(x)))


def _norm_mod(x, gain, shift, scale):
    y = x * lax.rsqrt(jnp.mean(x * x, axis=-1, keepdims=True) + NORM_EPS)
    return (y * gain) * (1.0 + scale) + shift


def _group_layernorm(y, avg, gain):
    yc = y - _dot(y.astype(BF16), avg)
    var = _dot((yc * yc).astype(BF16), avg)
    return yc * lax.rsqrt(var + GN_EPS) * gain


def _mod_kernel(cv_ref, w_ref, b_ref, o_ref):
    cv = cv_ref[...]
    o_ref[0] = _dot_f32(cv * jax.nn.sigmoid(cv), w_ref[0]) + b_ref[0]


def _modulation(cvec, mod_w, mod_b):
    depth, d, n = mod_w.shape
    tn = n // 4
    return pl.pallas_call(
        _mod_kernel,
        grid=(depth, n // tn),
        in_specs=[pl.BlockSpec((SUBLANES, d), lambda l, j: (0, 0)),
                  pl.BlockSpec((1, d, tn), lambda l, j: (l, 0, j)),
                  pl.BlockSpec((1, 1, tn), lambda l, j: (l, 0, j))],
        out_specs=pl.BlockSpec((1, SUBLANES, tn), lambda l, j: (l, 0, j)),
        out_shape=jax.ShapeDtypeStruct((depth, SUBLANES, n), F32),
        compiler_params=_params("parallel", "parallel"),
        name="adaln_modulation",
    )(cvec, mod_w, mod_b.reshape(depth, 1, n))


REC_Q, REC_V, REC_O, REC_RKV, REC_LORA, REC_GATE, REC_END = 0, 512, 1024, 1536, 3072, 3456, 3584
N_GATES = 4 * ML_HEADS


def _token_rows(x_ref, ctx_ref, n_lat_tiles):
    return jnp.where(pl.program_id(1) >= n_lat_tiles, ctx_ref[0], x_ref[0])


def _token_specs(tm, d, n_lat_tiles):
    return [pl.BlockSpec((1, tm, d), lambda b, i: (b, jnp.minimum(i, n_lat_tiles - 1), 0)),
            pl.BlockSpec((1, tm, d), lambda b, i: (b, jnp.maximum(i - n_lat_tiles, 0), 0))]


def _rec_inproj_kernel(x_ref, ctx_ref, mod_ref, g_ref, w_ref, wkt_ref, wgt_ref, gb_ref, gbt_ref,
                       q_ref, v_ref, o_ref, rkv_ref, lora_ref, gate_ref, kt_ref, gatet_ref, *, n_lat_tiles):
    d = x_ref.shape[-1]
    mod = mod_ref[0]
    h = _norm_mod(_token_rows(x_ref, ctx_ref, n_lat_tiles), g_ref[...], mod[:, 0:d], mod[:, d:2 * d]).astype(BF16)
    q_ref[0] = _dot(h, w_ref[:, REC_Q:REC_V]).astype(BF16)
    v_ref[0] = _dot(h, w_ref[:, REC_V:REC_O]).astype(BF16)
    o_ref[0] = _dot(h, w_ref[:, REC_O:REC_RKV]).astype(BF16)
    rkv_ref[0] = _dot(h, w_ref[:, REC_RKV:REC_LORA]).astype(BF16)
    lora_ref[0] = _dot(h, w_ref[:, REC_LORA:REC_GATE])
    gate_ref[0] = _dot(h, w_ref[:, REC_GATE:REC_END]) + gb_ref[...]
    kt_ref[0] = _dot_nt(wkt_ref[...], h).astype(BF16)
    gatet_ref[0] = _dot_nt(wgt_ref[...], h) + gbt_ref[...]


def _rec_inproj(x, ctx, mods, gain, w_main, wkt, wgt, gb_row, gb_col, n_lat_tiles):
    bsz, _, d = x.shape
    s = x.shape[1] + ctx.shape[1]
    tm = ROW_TILE
    tok = lambda c: pl.BlockSpec((1, tm, c), lambda b, i: (b, i, 0))
    mod_row = lambda b, i: (jnp.where(i >= n_lat_tiles, bsz, b), 0, 0)
    return pl.pallas_call(
        functools.partial(_rec_inproj_kernel, n_lat_tiles=n_lat_tiles),
        grid=(bsz, s // tm),
        in_specs=_token_specs(tm, d, n_lat_tiles) + [
                  pl.BlockSpec((1, 1, mods.shape[-1]), mod_row),
                  _resident((1, d)), _resident(w_main.shape), _resident(wkt.shape), _resident(wgt.shape),
                  _resident(gb_row.shape), _resident(gb_col.shape)],
        out_specs=[tok(ML_WIDTH), tok(ML_WIDTH), tok(ML_WIDTH), tok(3 * RW_WIDTH), tok(REC_GATE - REC_LORA),
                   tok(LANES),
                   pl.BlockSpec((1, ML_WIDTH, tm), lambda b, i: (b, 0, i)),
                   pl.BlockSpec((1, N_GATES, tm), lambda b, i: (b, 0, i))],
        out_shape=[jax.ShapeDtypeStruct((bsz, s, ML_WIDTH), BF16),
                   jax.ShapeDtypeStruct((bsz, s, ML_WIDTH), BF16),
                   jax.ShapeDtypeStruct((bsz, s, ML_WIDTH), BF16),
                   jax.ShapeDtypeStruct((bsz, s, 3 * RW_WIDTH), BF16),
                   jax.ShapeDtypeStruct((bsz, s, REC_GATE - REC_LORA), F32),
                   jax.ShapeDtypeStruct((bsz, s, LANES), F32),
                   jax.ShapeDtypeStruct((bsz, ML_WIDTH, s), BF16),
                   jax.ShapeDtypeStruct((bsz, N_GATES, s), F32)],
        compiler_params=_params("parallel", "parallel"),
        name="rec_inproj",
    )(x, ctx, mods, gain, w_main, wkt, wgt, gb_row, gb_col)


def _mlstm_stages(qf_ref, vf_ref, ktf_ref, gf_ref, gtf_ref, qb_ref, vb_ref, ktb_ref, gb_ref, gtb_ref,
                  hf_ref, hb_ref, c_ref, m_ref):
    L, dh = ML_CHUNK, ML_HEAD_DIM
    t_idx = lax.broadcasted_iota(jnp.int32, (L, L), 0)
    s_idx = lax.broadcasted_iota(jnp.int32, (L, L), 1)
    ones_col = (lax.broadcasted_iota(jnp.int32, (L, LANES), 1) == 0).astype(BF16)
    scale = dh ** -0.5
    streams = ((qf_ref, vf_ref, ktf_ref, gf_ref, gtf_ref, hf_ref),
               (qb_ref, vb_ref, ktb_ref, gb_ref, gtb_ref, hb_ref))
    chains = []
    for d, (q_ref, v_ref, kt_ref, g_ref, gt_ref, h_ref) in enumerate(streams):
        earlier = (s_idx <= t_idx) if d == 0 else (s_idx >= t_idx)
        tri = earlier.astype(BF16)
        tri_t = ((t_idx <= s_idx) if d == 0 else (t_idx >= s_idx)).astype(BF16)
        gt = gt_ref[0]
        lf_rows = _log_sigmoid(gt)
        b_cols = _dot_split_rhs(tri, _log_sigmoid(g_ref[0]), 3)
        b_rows = _dot_split_lhs(lf_rows, tri_t, 3)
        tot = jnp.sum(lf_rows, axis=1, keepdims=True)
        q, v, kt = q_ref[0], v_ref[0], kt_ref[0]
        for hh in range(ML_HEADS):
            ic, fc = 2 * ML_HEADS * d + hh, 2 * ML_HEADS * d + ML_HEADS + hh
            chains.append(dict(j=d * ML_HEADS + hh, hh=hh, h_ref=h_ref, earlier=earlier,
                               b_col=b_cols[:, fc:fc + 1], b_row=b_rows[fc:fc + 1, :], i_row=gt[ic:ic + 1, :],
                               b_end=tot[fc:fc + 1, :], q=q[:, hh * dh:(hh + 1) * dh], kt=kt[hh * dh:(hh + 1) * dh, :],
                               v_aug=jnp.concatenate([v[:, hh * dh:(hh + 1) * dh], ones_col], axis=1)))
    yield
    for ch in chains:
        ch["cst"] = c_ref[ch["j"]]
        ch["qk"] = _dot(ch["q"], ch["kt"])
        ch["qc"] = _dot(ch["q"], ch["cst"].astype(BF16))
    yield
    for ch in chains:
        m_prev = m_ref[ch["j"]:ch["j"] + 1, 0:1]
        d_in = jnp.where(ch["earlier"], ch["b_col"] - ch["b_row"] + ch["i_row"], MASK_VALUE)
        d_prev = ch["b_col"] + m_prev
        m_t = jnp.maximum(d_prev, jnp.max(d_in, axis=1, keepdims=True))
        ch["s"] = (ch["qk"] * (scale * jnp.exp(d_in - m_t))).astype(BF16)
        ch["w_prev"], ch["floor"] = jnp.exp(d_prev - m_t), jnp.exp(-m_t)
        d_end = ch["b_end"] - ch["b_row"] + ch["i_row"]
        m_new = jnp.maximum(ch["b_end"] + m_prev, jnp.max(d_end, axis=1, keepdims=True))
        ch["kw"] = (ch["kt"].astype(F32) * (jnp.exp(d_end - m_new) * scale)).astype(BF16)
        ch["decay"] = jnp.exp(ch["b_end"] + m_prev - m_new)
        m_ref[ch["j"]:ch["j"] + 1, :] = jnp.broadcast_to(m_new, (1, LANES))
    yield
    for ch in chains:
        acc = _dot(ch["s"], ch["v_aug"]) + ch["w_prev"] * ch["qc"]
        den = jnp.maximum(jnp.abs(acc[:, dh:dh + 1]), ch["floor"])
        ch["h_ref"][0, :, ch["hh"] * dh:(ch["hh"] + 1) * dh] = acc[:, :dh] / den
    yield
    for ch in chains:
        c_ref[ch["j"]] = ch["decay"] * ch["cst"] + _dot(ch["kw"], ch["v_aug"])


def _chunk_order(n_lat, n_ctx):
    n = n_lat + n_ctx
    fwd = lambda c: (c + n_lat) % n
    bwd = lambda c: n - 1 - c
    return fwd, bwd


def _rw_prep_kernel(z_ref, zp_ref, zn_ref, lora_ref, conv_ref, vec_ref, w2_ref, a2_ref, g2_ref, ones_ref,
                    r_ref, v_ref, kk_ref, bonus_ref, gate_ref, lw_ref, key_ref, bv_ref, *, t_lat, s_tot):
    tm, w = z_ref.shape[1], RW_WIDTH
    row0 = pl.program_id(1) * tm
    has_prev = jnp.where(jnp.logical_and(row0 != 0, row0 != t_lat), 1.0, 0.0)
    has_next = jnp.where(jnp.logical_and(row0 + tm != t_lat, row0 + tm != s_tot), 1.0, 0.0)
    z = z_ref[0].astype(F32)
    prev_row = zp_ref[0][HALO_ROWS - 1:HALO_ROWS, :].astype(F32) * has_prev
    next_row = zn_ref[0][0:1, :].astype(F32) * has_next
    ridx = lax.broadcasted_iota(jnp.int32, z.shape, 0)
    z_prev = jnp.where(ridx == 0, prev_row, pltpu.roll(z, 1, 0))
    z_next = jnp.where(ridx == tm - 1, next_row, pltpu.roll(z, tm - 1, 0))
    cw = conv_ref[...]
    zc = cw[0:1] * z_prev + cw[1:2] * z + cw[2:3] * z_next
    r, kr, vr = zc[:, :w], zc[:, w:2 * w], zc[:, 2 * w:]
    vec = vec_ref[...]
    k_k, k_a, r_k = vec[0:1], vec[1:2], vec[2:3]
    ones = ones_ref[...]
    kkr = kr * k_k
    kk = kkr * lax.rsqrt(_dot((kkr * kkr).astype(BF16), ones) + NORM_EPS)
    r_ref[0] = r.astype(BF16)
    v_ref[0] = vr.astype(BF16)
    kk_ref[0] = kk.astype(BF16)
    bonus_ref[0] = (_dot((r * kr * r_k).astype(BF16), ones) * vr).astype(BF16)
    lora = lora_ref[0]
    dec = _dot(jnp.tanh(lora[:, 0:LANES]).astype(BF16), w2_ref[...])
    icl = _dot(lora[:, LANES:2 * LANES].astype(BF16), a2_ref[...])
    gate_ref[0] = _dot(jax.nn.sigmoid(lora[:, 2 * LANES:3 * LANES]).astype(BF16), g2_ref[...]).astype(BF16)
    for d in range(2):
        a = jax.nn.sigmoid(vec[5 + d:6 + d] + icl[:, d * w:(d + 1) * w])
        lw_ref[d, 0] = -RW_DECAY_SCALE * jax.nn.sigmoid(vec[3 + d:4 + d] + dec[:, d * w:(d + 1) * w])
        key_ref[d, 0] = (kr * (1.0 + (a - 1.0) * k_a)).astype(BF16)
        bv_ref[d, 0] = (kk * a).astype(BF16)


def _rw_prep(rkv, lora, conv_w, vecs, w2cat, a2cat, g2, ones64, t_lat):
    bsz, s, _ = rkv.shape
    tm, w = ROW_TILE, RW_WIDTH
    nsub = tm // HALO_ROWS
    tok = lambda c: pl.BlockSpec((1, tm, c), lambda b, i: (b, i, 0))
    dirtok = pl.BlockSpec((2, 1, tm, w), lambda b, i: (0, b, i, 0))
    last = s // HALO_ROWS - 1
    return pl.pallas_call(
        functools.partial(_rw_prep_kernel, t_lat=t_lat, s_tot=s),
        grid=(bsz, s // tm),
        in_specs=[tok(3 * w),
                  pl.BlockSpec((1, HALO_ROWS, 3 * w), lambda b, i: (b, jnp.maximum(i * nsub - 1, 0), 0)),
                  pl.BlockSpec((1, HALO_ROWS, 3 * w), lambda b, i: (b, jnp.minimum((i + 1) * nsub, last), 0)),
                  tok(lora.shape[-1]),
                  _resident(conv_w.shape), _resident(vecs.shape), _resident(w2cat.shape), _resident(a2cat.shape),
                  _resident(g2.shape), _resident(ones64.shape)],
        out_specs=[tok(w)] * 5 + [dirtok] * 3,
        out_shape=[jax.ShapeDtypeStruct((bsz, s, w), BF16)] * 5
        + [jax.ShapeDtypeStruct((2, bsz, s, w), dt) for dt in (F32, BF16, BF16)],
        compiler_params=_params("parallel", "parallel"),
        name="rwkv_prep",
    )(rkv, rkv, rkv, lora, conv_w, vecs, w2cat, a2cat, g2, ones64)


def _rw_scan_stages(rf_ref, vf_ref, kkf_ref, lwf_ref, keyf_ref, bvf_ref, rb_ref, vb_ref, kkb_ref, lwb_ref, keyb_ref,
                    bvb_ref, yf_ref, yb_ref, s_ref, step):
    C, n = RW_CHUNK, RW_HEAD_DIM
    t2 = lax.broadcasted_iota(jnp.int32, (2 * C, 2 * C), 0) % C
    s2 = lax.broadcasted_iota(jnp.int32, (2 * C, 2 * C), 1) % C
    lower = lax.broadcasted_iota(jnp.int32, (2 * C, 2 * C), 0) >= C

    chains = []
    rows_of = (slice(step * C, (step + 1) * C), slice((1 - step) * C, (2 - step) * C))
    for d, (r_ref, v_ref, kk_ref, lw_ref, key_ref, bv_ref) in enumerate(
            ((rf_ref, vf_ref, kkf_ref, lwf_ref, keyf_ref, bvf_ref), (rb_ref, vb_ref, kkb_ref, lwb_ref, keyb_ref, bvb_ref))):
        rows = rows_of[d]
        rel = (s2 - t2) if d == 0 else (t2 - s2)
        keep = rel < jnp.where(lower, 1, 0)
        tri = (rel[:C, :C] <= 0).astype(BF16)
        lw = lw_ref[0, 0, rows, :]
        cum = _dot_split_rhs(tri, lw, 3)
        tot = jnp.sum(lw, axis=0, keepdims=True)
        r, v, kk, key, bv = (a.astype(F32) for a in (r_ref[0, rows, :], v_ref[0, rows, :], kk_ref[0, rows, :],
                                                     key_ref[0, 0, rows, :], bv_ref[0, 0, rows, :]))
        e_neg = jnp.exp(-cum)
        e_end = jnp.exp(tot - cum)
        alpha = (kk * jnp.exp(cum - lw)).astype(BF16)
        rho = (r * jnp.exp(cum)).astype(BF16)
        beta = (bv * e_neg).astype(BF16)
        kappa = (key * e_neg).astype(BF16)
        beta_end = (bv * e_end).astype(BF16)
        kappa_end = (key * e_end).astype(BF16)
        gamma = jnp.exp(tot)
        vb = v.astype(BF16)
        for h in range(RW_HEADS):
            sl = slice(h * n, (h + 1) * n)
            chains.append(dict(j=d * RW_HEADS + h, keep=keep, a=alpha[:, sl], rho=rho[:, sl], v=vb[:, sl],
                               lhs=jnp.concatenate([alpha[:, sl], rho[:, sl]], axis=0),
                               rhs=jnp.concatenate([beta[:, sl], kappa[:, sl]], axis=0),
                               end=jnp.concatenate([beta_end[:, sl], kappa_end[:, sl]], axis=0),
                               gamma=gamma[:, sl]))
    yield
    for ch in chains:
        ch["big"] = jnp.where(ch["keep"], _dot_nt(ch["lhs"], ch["rhs"]), 0.0).astype(BF16)
    yield
    for ch in chains:
        ch["lmv"] = _dot(ch["big"][:, C:], ch["v"])
    yield
    tt = lax.broadcasted_iota(jnp.int32, (C, C), 0)
    ss = lax.broadcasted_iota(jnp.int32, (C, C), 1)
    eye = (tt == ss).astype(F32)
    for ch in chains:
        ch["lab"] = ch["big"][:C, :C]
        ch["inv"] = eye - jnp.where((tt >> 1) == (ss >> 1), ch["lab"], 0.0).astype(F32)
    for k in range(1, 6):
        couples = jnp.logical_and((tt >> (k + 1)) == (ss >> (k + 1)), (tt >> k) != (ss >> k))
        for ch in chains:
            ch["ed"] = _dot(jnp.where(couples, ch["lab"], 0.0).astype(BF16), ch["inv"].astype(BF16)).astype(BF16)
        yield
        for ch in chains:
            ch["inv"] = ch["inv"] - _dot(ch["inv"].astype(BF16), ch["ed"])
        yield
    for ch in chains:
        x = -jnp.concatenate([ch["a"].astype(F32), ch["lmv"][:C]], axis=1)
        ch["x"] = _dot(ch["inv"].astype(BF16), x.astype(BF16))
    yield
    for ch in chains:
        ch["st"] = s_ref[ch["j"]]
        ch["ws"] = _dot_nt(jnp.concatenate([ch["x"][:, :n].astype(BF16), ch["rho"]], axis=0), ch["st"].astype(BF16))
    yield
    for ch in chains:
        ch["u"] = (ch["ws"][:C] + ch["x"][:, n:]).astype(BF16)
    for ch in chains:
        ch["y"] = ch["ws"][C:] + _dot(ch["big"][C:, :C], ch["u"]) + ch["lmv"][C:]
    yield
    for ch in chains:
        upd = _dot_tn(jnp.concatenate([ch["u"], ch["v"]], axis=0), ch["end"])
        s_ref[ch["j"]] = ch["st"] * ch["gamma"] + upd
    yf_ref[0, rows_of[0], :] = jnp.concatenate([ch["y"] for ch in chains[:RW_HEADS]], axis=1)
    yb_ref[0, rows_of[1], :] = jnp.concatenate([ch["y"] for ch in chains[RW_HEADS:]], axis=1)


RW_STAGES_PER_ML_STAGE = 4
RW_STATE_FREE_STAGES = 14


def _rec_scan_kernel(*refs):
    rw_in, ml_in = refs[:12], refs[12:22]
    yf_ref, yb_ref, hf_ref, hb_ref, s_ref, c_ref, m_ref = refs[22:]

    @pl.when(pl.program_id(1) == 0)
    def _():
        s_ref[...] = jnp.zeros_like(s_ref)
        c_ref[...] = jnp.zeros_like(c_ref)
        m_ref[...] = jnp.zeros_like(m_ref)

    ml = _mlstm_stages(*ml_in, hf_ref, hb_ref, c_ref, m_ref)
    first, second = (_rw_scan_stages(*rw_in, yf_ref, yb_ref, s_ref, step) for step in range(2))
    for n_stage in range(RW_STATE_FREE_STAGES):
        next(first)
        next(second)
        if (n_stage + 1) % RW_STAGES_PER_ML_STAGE == 0:
            next(ml, None)
    for tail in (first, second):
        next(tail, None)
        next(ml, None)
        for _ in tail:
            pass
    for _ in ml:
        pass


def _rec_scan(r, v, kk, lw, key, bv, q, vm, kt, gates, gates_t, t_lat):
    bsz, s, w = r.shape
    L = ML_CHUNK
    orders = _chunk_order(t_lat // L, (s - t_lat) // L)
    rw_specs, ml_specs, out_rw, out_ml = [], [], [], []
    for d, order in enumerate(orders):
        tok = lambda c, order=order: pl.BlockSpec((1, L, c), lambda b, i: (b, order(i), 0))
        dirtok = pl.BlockSpec((1, 1, L, w), lambda b, i, order=order, d=d: (d, b, order(i), 0))
        timelast = lambda c, order=order: pl.BlockSpec((1, c, L), lambda b, i: (b, 0, order(i)))
        rw_specs += [tok(w), tok(w), tok(w), dirtok, dirtok, dirtok]
        ml_specs += [tok(ML_WIDTH), tok(ML_WIDTH), timelast(ML_WIDTH), tok(LANES), timelast(N_GATES)]
        out_rw.append(tok(w))
        out_ml.append(tok(ML_WIDTH))
    return pl.pallas_call(
        _rec_scan_kernel,
        grid=(bsz, s // L),
        in_specs=rw_specs + ml_specs,
        out_specs=out_rw + out_ml,
        out_shape=[jax.ShapeDtypeStruct((bsz, s, w), F32)] * 2 + [jax.ShapeDtypeStruct((bsz, s, ML_WIDTH), F32)] * 2,
        scratch_shapes=[pltpu.VMEM((2 * RW_HEADS, RW_HEAD_DIM, RW_HEAD_DIM), F32),
                        pltpu.VMEM((2 * ML_HEADS, ML_HEAD_DIM, ML_HEAD_DIM + LANES), F32),
                        pltpu.VMEM((2 * ML_HEADS, LANES), F32)],
        compiler_params=_params("parallel", "arbitrary"),
        name="recurrent_scan",
    )(r, v, kk, lw, key, bv, r, v, kk, lw, key, bv, q, vm, kt, gates, gates_t, q, vm, kt, gates, gates_t)


def _residual_mlp(mix, x, mod, g_ffn, wo_ref, w1_ref, w2_ref, out_ref):
    d = x.shape[-1]
    x1 = x + mod[:, 2 * d:3 * d] * _dot(mix, wo_ref[...])
    h2 = _norm_mod(x1, g_ffn, mod[:, 3 * d:4 * d], mod[:, 4 * d:5 * d]).astype(BF16)
    acc = jnp.zeros_like(x1)
    for k in range(w1_ref.shape[1] // d):
        hid = jnp.maximum(_dot(h2, w1_ref[:, k * d:(k + 1) * d]), 0.0)
        acc = acc + _dot((hid * hid).astype(BF16), w2_ref[k * d:(k + 1) * d, :])
    out_ref[0] = x1 + mod[:, 5 * d:6 * d] * acc


def _rec_post_kernel(hf_ref, hb_ref, o_ref, yf_ref, yb_ref, bonus_ref, gate_ref, x_ref, ctx_ref, mod_ref, gf_ref,
                     mlg_ref, rwg_ref, avg_ml_ref, avg_rw_ref, wo_ref, w1_ref, w2_ref, out_ref, *, n_lat_tiles):
    ml = _group_layernorm(hf_ref[0] + hb_ref[0], avg_ml_ref[...], mlg_ref[...]) * jax.nn.sigmoid(o_ref[0].astype(F32))
    rw = (_group_layernorm(yf_ref[0] + yb_ref[0], avg_rw_ref[...], rwg_ref[...]) + bonus_ref[0]) * gate_ref[0]
    mix = jnp.concatenate([ml, rw], axis=1).astype(BF16)
    x = _token_rows(x_ref, ctx_ref, n_lat_tiles)
    _residual_mlp(mix, x, mod_ref[0], gf_ref[...], wo_ref, w1_ref, w2_ref, out_ref)


def _att_post_kernel(ow_ref, og_ref, x_ref, mod_ref, gf_ref, wo_ref, w1_ref, w2_ref, out_ref):
    mix = jnp.concatenate([ow_ref[0], og_ref[0]], axis=1)
    _residual_mlp(mix, x_ref[0], mod_ref[0], gf_ref[...], wo_ref, w1_ref, w2_ref, out_ref)


def _post_call(kernel, name, token_inputs, residual, mods, consts, n_rows, n_lat_tiles, tm=ROW_TILE):
    bsz, _, d = residual[0].shape
    mod_row = lambda b, i: (jnp.where(i >= n_lat_tiles, bsz, b), 0, 0)
    tok = lambda c: pl.BlockSpec((1, tm, c), lambda b, i: (b, i, 0))
    in_specs = [tok(arr.shape[-1]) for arr in token_inputs]
    in_specs += _token_specs(tm, d, n_lat_tiles) if len(residual) == 2 else [tok(d)]
    in_specs += [pl.BlockSpec((1, 1, mods.shape[-1]), mod_row)]
    in_specs += [_resident(c.shape) for c in consts]
    args = list(token_inputs) + list(residual)
    return pl.pallas_call(
        kernel,
        grid=(bsz, n_rows // tm),
        in_specs=in_specs,
        out_specs=pl.BlockSpec((1, tm, d), lambda b, i: (b, i, 0)),
        out_shape=jax.ShapeDtypeStruct((bsz, n_rows, d), F32),
        compiler_params=_params("parallel", "parallel"),
        name=name,
    )(*args, mods, *consts)


ATT_QW, ATT_QG, ATT_KW, ATT_KG, ATT_VW, ATT_VG, ATT_END = 0, 512, 1024, 1152, 1280, 1408, 1536
ATT_QK_END = ATT_VW


def _att_inproj_kernel(x_ref, mod_ref, g_ref, w_ref, qkg_ref, cos_ref, sin_ref, sq_ref,
                       qw_ref, qg_ref, kw_ref, kg_ref, vw_ref, vg0_ref, vg1_ref):
    d = x_ref.shape[-1]
    mod = mod_ref[0]
    h = _norm_mod(x_ref[0], g_ref[...], mod[:, 0:d], mod[:, d:2 * d]).astype(BF16)
    half = ATT_HEAD_DIM // 2
    cos2, sin2 = cos_ref[0], sin_ref[0]

    def norm_rope(lo, hi):
        z = _dot(h, w_ref[:, lo:hi])
        n = hi - lo
        sq = sq_ref[...]
        z = z * lax.rsqrt(_dot((z * z).astype(BF16), sq[:n, :n]) + NORM_EPS) * qkg_ref[:, lo:hi]
        reps = n // LANES
        cos = jnp.concatenate([cos2] * reps, axis=1)
        sin = jnp.concatenate([sin2] * reps, axis=1)
        lane = lax.broadcasted_iota(jnp.int32, z.shape, 1)
        partner = jnp.where(lane % ATT_HEAD_DIM < half, pltpu.roll(z, n - half, 1), pltpu.roll(z, half, 1))
        return (z * cos + partner * sin).astype(BF16)

    qw_ref[0] = norm_rope(ATT_QW, ATT_QG)
    qg_ref[0] = norm_rope(ATT_QG, ATT_KW)
    kk = norm_rope(ATT_KW, ATT_VW)
    kw_ref[0] = kk[:, :KV_WIDTH]
    kg_ref[0] = kk[:, KV_WIDTH:]
    vv = _dot(h, w_ref[:, ATT_VW:ATT_END])
    vw_ref[0] = vv[:, :KV_WIDTH].astype(BF16)
    vg = vv[:, KV_WIDTH:]
    lane = lax.broadcasted_iota(jnp.int32, vg.shape, 1)
    vg0_ref[0] = jnp.where(lane < ATT_HEAD_DIM, vg, jnp.where(lane == ATT_HEAD_DIM, 1.0, 0.0)).astype(BF16)
    vg1_ref[0] = jnp.where(lane >= ATT_HEAD_DIM, vg, jnp.where(lane == 0, 1.0, 0.0)).astype(BF16)


def _att_inproj(xs, mods, gain, w, qk_gain, cos, sin, sq_avg, n_lat_tiles):
    bsz, s, d = xs.shape
    tm = ROW_TILE
    tok = lambda c: pl.BlockSpec((1, tm, c), lambda b, i: (b, i, 0))
    mod_row = lambda b, i: (jnp.where(i >= n_lat_tiles, bsz, b), 0, 0)
    rope = pl.BlockSpec((1, tm, LANES), lambda b, i: (0, i, 0))
    qw = Q_HEADS * ATT_HEAD_DIM
    return pl.pallas_call(
        _att_inproj_kernel,
        grid=(bsz, s // tm),
        in_specs=[tok(d), pl.BlockSpec((1, 1, mods.shape[-1]), mod_row), _resident((1, d)), _resident(w.shape),
                  _resident(qk_gain.shape), rope, rope, _resident(sq_avg.shape)],
        out_specs=[tok(qw), tok(qw)] + [tok(KV_WIDTH)] * 5,
        out_shape=[jax.ShapeDtypeStruct((bsz, s, c), BF16) for c in (qw, qw) + (KV_WIDTH,) * 5],
        compiler_params=_params("parallel", "parallel"),
        name="att_inproj",
    )(xs, mods, gain, w, qk_gain, cos, sin, sq_avg)


def _stack_queries(q):
    lane = lax.broadcasted_iota(jnp.int32, (q.shape[0], LANES), 1)
    zero = jnp.zeros((q.shape[0], LANES), q.dtype)
    slabs = [q[:, g * LANES:(g + 1) * LANES] for g in range(KV_GROUP)]
    lo = [jnp.where(lane < ATT_HEAD_DIM, sl, zero) for sl in slabs]
    hi = [jnp.where(lane >= ATT_HEAD_DIM, sl, zero) for sl in slabs]
    return jnp.concatenate(lo + hi, axis=0)


def _unstack_outputs(o, rows):
    lane = lax.broadcasted_iota(jnp.int32, (rows, LANES), 1)
    slabs = [jnp.where(lane < ATT_HEAD_DIM, o[g * rows:(g + 1) * rows], o[(KV_GROUP + g) * rows:(KV_GROUP + g + 1) * rows])
             for g in range(KV_GROUP)]
    return jnp.concatenate(slabs, axis=1)


def _attn_kernel(qw_ref, kp_ref, ko_ref, kn_ref, kc_ref, vp_ref, vo_ref, vn_ref, vc_ref, sink_ref,
                 qg_ref, k_ref, v0_ref, v1_ref, ow_ref, og_ref, *, tk):
    L = Q_BLOCK
    i, nb = pl.program_id(1), pl.num_programs(1)
    lc = kc_ref.shape[1]
    qs_w = _stack_queries(qw_ref[0])
    keys = jnp.concatenate([kp_ref[0], ko_ref[0], kn_ref[0], kc_ref[0]], axis=0)
    vals = jnp.concatenate([vp_ref[0], vo_ref[0], vn_ref[0], vc_ref[0]], axis=0)
    s_w = _dot_nt(qs_w, keys)

    qs = _stack_queries(qg_ref[0])
    rows = qs.shape[0]
    half = rows // 2
    n_tiles = k_ref.shape[1] // tk
    scores = lambda j: _dot_nt(qs, k_ref[0, j * tk:(j + 1) * tk, :])
    s_next = scores(0)

    nk = 3 * L + lc
    row = lax.broadcasted_iota(jnp.int32, (L, nk), 0)
    col = lax.broadcasted_iota(jnp.int32, (L, nk), 1)
    off = col - L - row
    valid = jnp.logical_and(off <= WINDOW, off >= -WINDOW)
    valid = jnp.logical_and(valid, col >= jnp.where(i > 0, 0, L))
    valid = jnp.logical_and(valid, col < jnp.where(i < nb - 1, 3 * L, 2 * L))
    valid = jnp.logical_or(valid, col >= 3 * L)
    valid = jnp.concatenate([valid] * (2 * KV_GROUP), axis=0)
    s_w = jnp.where(valid, s_w, MASK_VALUE)
    sink = sink_ref[...]
    m_w = jnp.maximum(jnp.max(s_w, axis=1, keepdims=True), sink)
    p_w = jnp.exp(s_w - m_w)
    den_w = jnp.sum(p_w, axis=1, keepdims=True) + jnp.exp(sink - m_w)
    ow_ref[0] = _unstack_outputs(_dot(p_w.astype(BF16), vals) / den_w, L).astype(BF16)

    m = jnp.full((rows, 1), MASK_VALUE, F32)
    acc = jnp.zeros((rows, LANES), F32)
    for j in range(n_tiles):
        s = s_next
        if j + 1 < n_tiles:
            s_next = scores(j + 1)
        m_new = jnp.maximum(m, jnp.max(s, axis=1, keepdims=True))
        p = jnp.exp2(s - m_new).astype(BF16)
        pv = jnp.concatenate([_dot(p[:half], v0_ref[0, j * tk:(j + 1) * tk, :]),
                              _dot(p[half:], v1_ref[0, j * tk:(j + 1) * tk, :])], axis=0)
        acc = jnp.exp2(m - m_new) * acc + pv
        m = m_new
    den = jnp.concatenate([acc[:half, ATT_HEAD_DIM:ATT_HEAD_DIM + 1], acc[half:, 0:1]], axis=0)
    og_ref[0] = _unstack_outputs(acc / den, L).astype(BF16)


GLB_KEY_TILES = (2816, 1408, 768, 512, 384, 256, 128)


def _attention(qw, kw, vw, sink_col, qg, kg, vg0, vg1, t_lat):
    bsz, s, width = qw.shape
    L = Q_BLOCK
    nb = t_lat // L
    lc = s - t_lat
    tk = next(t for t in GLB_KEY_TILES if s % t == 0)
    qblk = pl.BlockSpec((1, L, width), lambda b, i: (b, i, 0))
    kv = lambda f: pl.BlockSpec((1, L, KV_WIDTH), lambda b, i: (b, f(i), 0))
    ctx = pl.BlockSpec((1, lc, KV_WIDTH), lambda b, i: (b, t_lat // lc, 0))
    nbr = [kv(lambda i: jnp.maximum(i - 1, 0)), kv(lambda i: i), kv(lambda i: jnp.minimum(i + 1, nb - 1)), ctx]
    whole = pl.BlockSpec((1, s, KV_WIDTH), lambda b, i: (b, 0, 0))
    return pl.pallas_call(
        functools.partial(_attn_kernel, tk=tk),
        grid=(bsz, nb),
        in_specs=[qblk] + nbr + nbr + [_resident(sink_col.shape), qblk, whole, whole, whole],
        out_specs=[qblk, qblk],
        out_shape=[jax.ShapeDtypeStruct((bsz, t_lat, width), BF16)] * 2,
        compiler_params=_params("parallel", "parallel"),
        name="attention",
    )(qw, kw, kw, kw, kw, vw, vw, vw, vw, sink_col, qg, kg, vg0, vg1)


def _block_diag_const(width, group, value):
    idx = np.arange(width) // group
    return jnp.asarray((idx[:, None] == idx[None, :]) * value, dtype=BF16)


def _head_perm():
    return np.concatenate([np.arange(0, ATT_HEAD_DIM, 2), np.arange(1, ATT_HEAD_DIM, 2)])


def _q_head_order():
    return [kv * KV_GROUP + g for g in range(KV_GROUP) for kv in range(KV_HEADS)]


def _rec_weights(w, gate_b, conv_w, w0, w2, a0, a2, k_k, k_a, r_k):
    o = np.cumsum((0,) + (ML_WIDTH,) * 4 + (ML_HEADS,) * 4 + (RW_WIDTH,) * 3
                  + (RW_DECAY_RANK,) * 2 + (RW_ICLR_RANK,) * 2 + (RW_GATE_RANK,))
    q, k, v, og = (w[:, o[i]:o[i + 1]] for i in range(4))
    gates = w[:, o[4]:o[8]]
    rkv = w[:, o[8]:o[11]]
    lora = w[:, o[11]:o[16]]
    gates_pad = jnp.pad(gates, ((0, 0), (0, LANES - N_GATES)))
    w_main = jnp.concatenate([q, v, og, rkv, lora, gates_pad], axis=1).astype(BF16)
    gb = gate_b.reshape(-1)
    gb_row = jnp.pad(gb, (0, LANES - N_GATES)).reshape(1, LANES)
    zeros = jnp.zeros((RW_DECAY_RANK, RW_WIDTH), F32)
    w2cat = jnp.concatenate([jnp.concatenate([w2[0], zeros], 1), jnp.concatenate([zeros, w2[1]], 1)], 0).astype(BF16)
    a2cat = jnp.concatenate([jnp.concatenate([a2[0], zeros], 1), jnp.concatenate([zeros, a2[1]], 1)], 0).astype(BF16)
    vecs = jnp.stack([k_k, k_a, r_k, w0[0], w0[1], a0[0], a0[1], jnp.zeros_like(k_k)])
    return dict(w_main=w_main, wkt=k.T.astype(BF16), wgt=gates.T.astype(BF16), gb_row=gb_row,
                gb_col=gb.reshape(N_GATES, 1), conv=conv_w, vecs=vecs, w2cat=w2cat, a2cat=a2cat)


def _att_weights(w, win_q_norm, win_k_norm, glb_q_norm, glb_k_norm, w_out):
    hd, perm = ATT_HEAD_DIM, _head_perm()
    qn, kn = Q_HEADS * hd, KV_HEADS * hd
    o = np.cumsum((0, qn, kn, kn, qn, kn, kn))
    qw, kw, vw, qg, kg, vg = (w[:, o[i]:o[i + 1]] for i in range(6))
    q_cols = np.concatenate([h * hd + perm for h in _q_head_order()])
    k_cols = np.concatenate([h * hd + perm for h in range(KV_HEADS)])
    w_all = jnp.concatenate([qw[:, q_cols], qg[:, q_cols], kw[:, k_cols], kg[:, k_cols], vw, vg], axis=1).astype(BF16)
    scale = hd ** -0.5
    log2e = float(np.log2(np.e))
    gains = jnp.concatenate([jnp.tile(win_q_norm[perm], Q_HEADS) * scale,
                             jnp.tile(glb_q_norm[perm], Q_HEADS) * (scale * log2e),
                             jnp.tile(win_k_norm[perm], KV_HEADS), jnp.tile(glb_k_norm[perm], KV_HEADS)]).reshape(1, -1)
    out_rows = np.concatenate([h * hd + np.arange(hd) for h in _q_head_order()])
    wo = jnp.concatenate([w_out[:qn][out_rows], w_out[qn:][out_rows]], axis=0).astype(BF16)
    return w_all, gains, wo


def _rope_tables(t_lat, s):
    n_freq = ATT_HEAD_DIM // 4
    pos = jnp.arange(t_lat)
    inv_freq = ROPE_THETA ** (-jnp.arange(n_freq, dtype=F32) / n_freq)
    ang = jnp.concatenate([(pos // GRID_W).astype(F32)[:, None] * inv_freq,
                           (pos % GRID_W).astype(F32)[:, None] * inv_freq], axis=-1)
    cos, sin = jnp.cos(ang), jnp.sin(ang)
    cos_h = jnp.concatenate([cos, cos], axis=-1)
    sin_h = jnp.concatenate([-sin, sin], axis=-1)
    pad = ((0, s - t_lat), (0, 0))
    cos_h = jnp.pad(cos_h, pad, constant_values=1.0)
    sin_h = jnp.pad(sin_h, pad)
    return jnp.tile(cos_h, (1, 2))[None], jnp.tile(sin_h, (1, 2))[None]


def kernel(x, c, ctx, c_ctx, norm_mix, norm_ffn, mod_w, mod_b, out_w, ffn_w1, ffn_w2, rec_in_w, ml_gate_b, ml_norm_g,
           rw_conv, rw_w0, rw_w2, rw_a0, rw_a2, rw_g2, rw_kk, rw_ka, rw_rk, rw_norm_g, att_in_w, win_q_norm,
           win_k_norm, win_sink, glb_q_norm, glb_k_norm):
    bsz, t_lat, d = x.shape
    lc = ctx.shape[1]
    s = t_lat + lc
    depth = mod_w.shape[0]
    assert depth == 2 and bsz + 1 <= SUBLANES
    assert t_lat % ROW_TILE == 0 and lc % ROW_TILE == 0 and t_lat % lc == 0 and t_lat % GRID_W == 0
    n_lat_tiles = t_lat // ROW_TILE

    cvec = jnp.zeros((SUBLANES, d), F32).at[:bsz].set(c).at[bsz].set(c_ctx)
    mods = _modulation(cvec, mod_w, mod_b)[:, :bsz + 1].reshape(depth, bsz + 1, 1, 6 * d)
    w1 = ffn_w1.astype(BF16)
    w2 = ffn_w2.astype(BF16)

    rw = _rec_weights(rec_in_w[0], ml_gate_b[0], rw_conv[0], rw_w0[0], rw_w2[0], rw_a0[0], rw_a2[0],
                      rw_kk[0], rw_ka[0], rw_rk[0])
    q, v, o, rkv, lora, gates, kt, gates_t = _rec_inproj(
        x, ctx, mods[0], norm_mix[0].reshape(1, d), rw["w_main"], rw["wkt"], rw["wgt"], rw["gb_row"], rw["gb_col"],
        n_lat_tiles)
    ones64 = _block_diag_const(RW_WIDTH, RW_HEAD_DIM, 1.0)
    r, vr, kk, bonus, gate, lw, key, bv = _rw_prep(rkv, lora, rw["conv"], rw["vecs"], rw["w2cat"], rw["a2cat"],
                                                   rw_g2[0].astype(BF16), ones64, t_lat)
    y_f, y_b, h_f, h_b = _rec_scan(r, vr, kk, lw, key, bv, q, v, kt, gates, gates_t, t_lat)
    consts = [norm_ffn[0].reshape(1, d), ml_norm_g[0].reshape(1, -1), rw_norm_g[0].reshape(1, -1),
              _block_diag_const(ML_WIDTH, ML_HEAD_DIM, 1.0 / ML_HEAD_DIM),
              _block_diag_const(RW_WIDTH, RW_HEAD_DIM, 1.0 / RW_HEAD_DIM),
              out_w[0].astype(BF16), w1[0], w2[0]]
    tokens = [h_f, h_b, o, y_f, y_b, bonus, gate]
    xs = _post_call(functools.partial(_rec_post_kernel, n_lat_tiles=n_lat_tiles), "rec_post", tokens, (x, ctx),
                    mods[0], consts, s, n_lat_tiles)

    w_att, qk_gain, wo = _att_weights(att_in_w[0], win_q_norm[0], win_k_norm[0], glb_q_norm[0], glb_k_norm[0], out_w[1])
    cos, sin = _rope_tables(t_lat, s)
    sq_avg = _block_diag_const(Q_HEADS * ATT_HEAD_DIM, ATT_HEAD_DIM, 1.0 / ATT_HEAD_DIM)
    qw, qg, kw, kg, vw, vg0, vg1 = _att_inproj(xs, mods[1], norm_mix[1].reshape(1, d), w_att, qk_gain, cos, sin,
                                               sq_avg, n_lat_tiles)
    sink_col = jnp.repeat(win_sink[0][np.array([kv * KV_GROUP + g for kv in range(KV_HEADS) for g in range(KV_GROUP)])],
                          Q_BLOCK).reshape(-1, 1)
    o_win, o_glb = _attention(qw, kw, vw, sink_col, qg, kg, vg0, vg1, t_lat)
    consts = [norm_ffn[1].reshape(1, d), wo, w1[1], w2[1]]
    tm = next(t for t in POST_LATENT_TILES if t_lat % t == 0)
    return _post_call(_att_post_kernel, "att_post", [o_win, o_glb], (xs,), mods[1], consts, t_lat, t_lat // tm, tm)
```
